```python
import math
import jax
import jax.numpy as jnp
from jax import lax
import numpy as np

D_MODEL = 1024
BATCH = 8
SEQ = 4096
DEPTH = 1
DEC_BATCH = 128
DEC_SEQ = 4
PAST_LEN = 8192
PAGE_SIZE = 128

SSM_HEADS = 8
SSM_HEAD_DIM = 64
D_SSM = SSM_HEADS * SSM_HEAD_DIM
SSM_GROUPS = 2
D_STATE = 64
CONV_W = 4
CONV_DIM = D_SSM + 2 * SSM_GROUPS * D_STATE
SSD_CHUNK = 128
NSA_HEADS = 8
NSA_KV_HEADS = 2
HEAD_DIM = 64
D_NSA = NSA_HEADS * HEAD_DIM
KV_DIM = NSA_KV_HEADS * HEAD_DIM
N_KV_PROJ = 6
CMP_STRIDE = 16
CMP_BLOCK = 2 * CMP_STRIDE
CMP_HIDDEN = 128
SEL_BLOCK = 64
SEL_TOP_N = 16
WINDOW = 512
Q_BLOCK = 64
N_BRANCH = 3
ROPE_THETA = 10000.0
D_FF = 2816
D_MIX = D_SSM + D_NSA
IN_SPLITS = (D_SSM, CONV_DIM, SSM_HEADS, D_NSA, N_KV_PROJ * KV_DIM, NSA_HEADS * N_BRANCH)
IN_OFFSETS = tuple(int(v) for v in np.cumsum(IN_SPLITS)[:-1])
D_IN = sum(IN_SPLITS)
ALPHA = (2.0 * DEPTH) ** 0.25
BETA = (8.0 * DEPTH) ** -0.25
LN_EPS = 1e-5
RMS_EPS = 1e-5
NEG_INF = -1e30

kernel_name = "hymba_ssd_nsa_macaron_deepnorm_step"


def layer_norm(x, g, b):
    xf = x.astype(jnp.float32)
    mu = xf.mean(-1, keepdims=True)
    var = jnp.square(xf - mu).mean(-1, keepdims=True)
    return ((xf - mu) * lax.rsqrt(var + LN_EPS) * g + b).astype(x.dtype)


def swiglu(x, w_gate, w_up, w_down):
    return (jax.nn.silu(x @ w_gate) * (x @ w_up)) @ w_down


def rope(x, pos):
    half = HEAD_DIM // 2
    inv = ROPE_THETA ** (-jnp.arange(half, dtype=jnp.float32) / half)
    ang = pos.astype(jnp.float32)[:, None] * inv[None, :]
    cos = jnp.cos(ang)[None, :, None, :]
    sin = jnp.sin(ang)[None, :, None, :]
    xf = x.astype(jnp.float32)
    x1, x2 = xf[..., :half], xf[..., half:]
    return jnp.concatenate([x1 * cos - x2 * sin, x2 * cos + x1 * sin], axis=-1).astype(x.dtype)


def masked_softmax(s, mask):
    return jax.nn.softmax(jnp.where(mask, s, NEG_INF), axis=-1) * mask


def causal_dwconv(x_hist, w, b):
    y = lax.conv_general_dilated(x_hist, w[:, None, :], window_strides=(1,), padding='VALID',
                                 dimension_numbers=('NWC', 'WIO', 'NWC'),
                                 feature_group_count=x_hist.shape[-1])
    return y + b


def project_in(h, lw, pos):
    bsz, t = h.shape[:2]
    z, xbc, dt_raw, q, kv, gate = jnp.split(h @ lw['w_in'], IN_OFFSETS, axis=-1)
    q = rope(q.reshape(bsz, t, NSA_HEADS, HEAD_DIM), pos)
    kv = kv.reshape(bsz, t, N_KV_PROJ, NSA_KV_HEADS, HEAD_DIM)
    k_c, v_c, k_s, v_s, k_w, v_w = [kv[:, :, i] for i in range(N_KV_PROJ)]
    k_c, k_s, k_w = rope(k_c, pos), rope(k_s, pos), rope(k_w, pos)
    gate = (gate + lw['b_gate']).reshape(bsz, t, NSA_HEADS, N_BRANCH)
    return z, xbc, dt_raw, q, k_c, v_c, k_s, v_s, k_w, v_w, gate


def ssd_scan(x, dt, a, bm, cm, h0):
    bsz, t = x.shape[:2]
    lc = min(SSD_CHUNK, t)
    nc = t // lc

    def to_chunks(u):
        return jnp.moveaxis(u.reshape(bsz, nc, lc, *u.shape[2:]), 1, 0)

    causal = jnp.tril(jnp.ones((lc, lc), bool))

    def step(h, inp):
        xc, dtc, bc, cc = inp
        cum = jnp.cumsum(dtc * a, axis=1)
        seg = cum[:, :, None, :] - cum[:, None, :, :]
        decay = jnp.exp(jnp.where(causal[None, :, :, None], seg, -jnp.inf))
        xdt = xc * dtc[..., None]
        w = jnp.einsum('blhn,bshn->blsh', cc, bc) * decay
        y = jnp.einsum('blsh,bshp->blhp', w, xdt)
        y = y + jnp.einsum('blhn,bhpn->blhp', cc, h) * jnp.exp(cum)[..., None]
        tail = jnp.exp(cum[:, -1:, :] - cum)
        h_new = h * jnp.exp(cum[:, -1])[:, :, None, None] + jnp.einsum('bshn,bsh,bshp->bhpn', bc, tail, xdt)
        return h_new, y

    h, ys = lax.scan(step, h0, (to_chunks(x), to_chunks(dt), to_chunks(bm), to_chunks(cm)))
    return jnp.moveaxis(ys, 0, 1).reshape(x.shape), h


def ssd_mix(z, xbc_hist, dt_raw, h0, lw):
    f32 = jnp.float32
    bsz, t = z.shape[:2]
    xbc = jax.nn.silu(causal_dwconv(xbc_hist, lw['conv_w'], lw['conv_b']))
    xs, bm, cm = jnp.split(xbc, [D_SSM, D_SSM + SSM_GROUPS * D_STATE], axis=-1)
    rep = SSM_HEADS // SSM_GROUPS
    xs = xs.reshape(bsz, t, SSM_HEADS, SSM_HEAD_DIM).astype(f32)
    bm = jnp.repeat(bm.reshape(bsz, t, SSM_GROUPS, D_STATE), rep, axis=2).astype(f32)
    cm = jnp.repeat(cm.reshape(bsz, t, SSM_GROUPS, D_STATE), rep, axis=2).astype(f32)
    dt = jax.nn.softplus(dt_raw.astype(f32) + lw['dt_bias'])
    a = -jnp.exp(lw['a_log'].astype(f32))
    y, h = ssd_scan(xs, dt, a, bm, cm, h0.astype(f32))
    y = (y + lw['d_skip'][:, None] * xs).reshape(bsz, t, D_SSM) * jax.nn.silu(z.astype(f32))
    y = y * lax.rsqrt(jnp.mean(y * y, axis=-1, keepdims=True) + RMS_EPS) * lw['ssm_norm_w']
    return y.astype(z.dtype), h.astype(h0.dtype)


def compress(k, w1, w2, pe):
    bsz, length = k.shape[:2]
    n_str = -(-length // CMP_STRIDE)
    k = jnp.pad(k, ((0, 0), (0, n_str * CMP_STRIDE - length), (0, 0), (0, 0)))
    ch = k.reshape(bsz, n_str, CMP_STRIDE, NSA_KV_HEADS, HEAD_DIM)
    first = jnp.einsum('bnsgd,sdh->bngh', ch, w1[:CMP_STRIDE])
    second = jnp.einsum('bnsgd,sdh->bngh', ch, w1[CMP_STRIDE:])
    pre = first[:, :-1] + second[:, 1:] + jnp.einsum('sd,sdh->h', pe, w1)
    return jax.nn.gelu(pre) @ w2


def sel_blocks(k):
    bsz, length = k.shape[:2]
    ns = -(-length // SEL_BLOCK)
    k = jnp.pad(k, ((0, 0), (0, ns * SEL_BLOCK - length), (0, 0), (0, 0)))
    return k.reshape(bsz, ns, SEL_BLOCK, NSA_KV_HEADS, HEAD_DIM)


def nsa_keys(k_c, v_c, k_s, v_s, lw):
    kc = compress(k_c, lw['cmp_k_w1'], lw['cmp_k_w2'], lw['cmp_k_pe'])
    vc = compress(v_c, lw['cmp_v_w1'], lw['cmp_v_w2'], lw['cmp_v_pe'])
    cmp_end = jnp.arange(kc.shape[1]) * CMP_STRIDE + CMP_BLOCK - 1
    return kc, vc, cmp_end, sel_blocks(k_s), sel_blocks(v_s)


def nsa_attend(q, q_pos, kc, vc, cmp_end, ks_blk, vs_blk, kw, vw, kw_pos, gate):
    f32 = jnp.float32
    bsz, tq = q.shape[:2]
    rep = NSA_HEADS // NSA_KV_HEADS
    qg = q.reshape(bsz, tq, NSA_KV_HEADS, rep, HEAD_DIM).astype(f32) * HEAD_DIM ** -0.5
    m_c = cmp_end[None, :] <= q_pos[:, None]
    p_c = masked_softmax(jnp.einsum('btgrd,bcgd->btgrc', qg, kc.astype(f32)), m_c[None, :, None, None, :])
    o_c = jnp.einsum('btgrc,bcgd->btgrd', p_c, vc.astype(f32))
    nc, ns = kc.shape[1], ks_blk.shape[1]
    c_start = jnp.arange(nc) * CMP_STRIDE
    s_start = jnp.arange(ns) * SEL_BLOCK
    overlap = ((c_start[:, None] + CMP_BLOCK > s_start[None, :]) &
               (c_start[:, None] < s_start[None, :] + SEL_BLOCK)).astype(f32)
    imp = jnp.einsum('btgrc,cj->btgj', p_c, overlap)
    blk = jnp.arange(ns)[None, :]
    cur = (q_pos // SEL_BLOCK)[:, None]
    valid = (s_start[None, :] <= q_pos[:, None])[None, :, None, :]
    forced = ((blk == 0) | (blk == cur) | (blk == cur - 1))[None, :, None, :]
    score = jnp.where(valid, jnp.where(forced, jnp.inf, imp), -jnp.inf)
    _, idx = lax.top_k(score, min(SEL_TOP_N, ns))
    n_sel = idx.shape[-1]
    bi = jnp.arange(bsz)[:, None, None, None]
    gidx = jnp.arange(NSA_KV_HEADS)[None, None, :, None]
    k_sel = ks_blk[bi, idx, :, gidx].reshape(bsz, tq, NSA_KV_HEADS, n_sel * SEL_BLOCK, HEAD_DIM)
    v_sel = vs_blk[bi, idx, :, gidx].reshape(bsz, tq, NSA_KV_HEADS, n_sel * SEL_BLOCK, HEAD_DIM)
    kpos = (idx[..., None] * SEL_BLOCK + jnp.arange(SEL_BLOCK)).reshape(bsz, tq, NSA_KV_HEADS, n_sel * SEL_BLOCK)
    m_s = (kpos <= q_pos[None, :, None, None])[:, :, :, None, :]
    p_s = masked_softmax(jnp.einsum('btgrd,btgkd->btgrk', qg, k_sel.astype(f32)), m_s)
    o_s = jnp.einsum('btgrk,btgkd->btgrd', p_s, v_sel.astype(f32))
    m_w = ((kw_pos[None, :] <= q_pos[:, None]) & (kw_pos[None, :] > q_pos[:, None] - WINDOW)
           & (kw_pos[None, :] >= 0))
    p_w = masked_softmax(jnp.einsum('btgrd,bsgd->btgrs', qg, kw.astype(f32)), m_w[None, :, None, None, :])
    o_w = jnp.einsum('btgrs,bsgd->btgrd', p_w, vw.astype(f32))
    g = jax.nn.sigmoid(gate.astype(f32)).reshape(bsz, tq, NSA_KV_HEADS, rep, N_BRANCH)
    o = g[..., 0:1] * o_c + g[..., 1:2] * o_s + g[..., 2:3] * o_w
    return o.reshape(bsz, tq, D_NSA).astype(q.dtype)


def nsa_prompt(q, kc, vc, cmp_end, ks_blk, vs_blk, k_w, v_w, gate):
    bsz, t = q.shape[:2]
    nb = t // Q_BLOCK
    kw_pad = jnp.pad(k_w, ((0, 0), (WINDOW, 0), (0, 0), (0, 0)))
    vw_pad = jnp.pad(v_w, ((0, 0), (WINDOW, 0), (0, 0), (0, 0)))
    qb = jnp.moveaxis(q.reshape(bsz, nb, Q_BLOCK, NSA_HEADS, HEAD_DIM), 1, 0)
    gb = jnp.moveaxis(gate.reshape(bsz, nb, Q_BLOCK, NSA_HEADS, N_BRANCH), 1, 0)

    def block(args):
        qi, gi, i = args
        start = i * Q_BLOCK
        q_pos = start + jnp.arange(Q_BLOCK)
        kwi = lax.dynamic_slice_in_dim(kw_pad, start, WINDOW + Q_BLOCK, axis=1)
        vwi = lax.dynamic_slice_in_dim(vw_pad, start, WINDOW + Q_BLOCK, axis=1)
        kw_pos = start - WINDOW + jnp.arange(WINDOW + Q_BLOCK)
        return nsa_attend(qi, q_pos, kc, vc, cmp_end, ks_blk, vs_blk, kwi, vwi, kw_pos, gi)

    o = lax.map(block, (qb, gb, jnp.arange(nb)))
    return jnp.moveaxis(o, 0, 1).reshape(bsz, t, D_NSA)


def mix_prompt(h, lw):
    bsz, t = h.shape[:2]
    pos = jnp.arange(t)
    z, xbc, dt_raw, q, k_c, v_c, k_s, v_s, k_w, v_w, gate = project_in(h, lw, pos)
    xbc_hist = jnp.pad(xbc, ((0, 0), (CONV_W - 1, 0), (0, 0)))
    h0 = jnp.zeros((bsz, SSM_HEADS, SSM_HEAD_DIM, D_STATE), h.dtype)
    y_ssd, h_ssm = ssd_mix(z, xbc_hist, dt_raw, h0, lw)
    kc, vc, cmp_end, ks_blk, vs_blk = nsa_keys(k_c, v_c, k_s, v_s, lw)
    o_nsa = nsa_prompt(q, kc, vc, cmp_end, ks_blk, vs_blk, k_w, v_w, gate)
    out = jnp.concatenate([y_ssd, o_nsa], axis=-1) @ lw['w_out']
    wb = min(WINDOW, t)
    state = (k_c, v_c, k_s, v_s, k_w[:, -wb:], v_w[:, -wb:], h_ssm, xbc_hist[:, -(CONV_W - 1):])
    return out, state


def gather_pages(cache, l, page_table):
    g = cache[l, page_table]
    return g.reshape(page_table.shape[0], -1, NSA_KV_HEADS, HEAD_DIM)


def mix_sample(h, lw, l, cache_k_cmp, cache_v_cmp, cache_k_slc, cache_v_slc, cache_k_win, cache_v_win,
               state_ssm, state_conv, page_table):
    s_len = h.shape[1]
    pos = PAST_LEN + jnp.arange(s_len)
    z, xbc, dt_raw, q, k_c, v_c, k_s, v_s, k_w, v_w, gate = project_in(h, lw, pos)
    xbc_hist = jnp.concatenate([state_conv[l], xbc], axis=1)
    y_ssd, h_ssm = ssd_mix(z, xbc_hist, dt_raw, state_ssm[l], lw)
    kc, vc, cmp_end, ks_blk, vs_blk = nsa_keys(
        jnp.concatenate([gather_pages(cache_k_cmp, l, page_table), k_c], axis=1),
        jnp.concatenate([gather_pages(cache_v_cmp, l, page_table), v_c], axis=1),
        jnp.concatenate([gather_pages(cache_k_slc, l, page_table), k_s], axis=1),
        jnp.concatenate([gather_pages(cache_v_slc, l, page_table), v_s], axis=1), lw)
    wb = cache_k_win.shape[2]
    kw = jnp.concatenate([cache_k_win[l], k_w], axis=1)
    vw = jnp.concatenate([cache_v_win[l], v_w], axis=1)
    kw_pos = PAST_LEN - wb + jnp.arange(wb + s_len)
    o_nsa = nsa_attend(q, pos, kc, vc, cmp_end, ks_blk, vs_blk, kw, vw, kw_pos, gate)
    out = jnp.concatenate([y_ssd, o_nsa], axis=-1) @ lw['w_out']
    state = (k_c, v_c, k_s, v_s, kw[:, -wb:], vw[:, -wb:], h_ssm, xbc_hist[:, -(CONV_W - 1):])
    return out, state


def layer_forward(x, lw, mix_fn, *mix_args):
    h = layer_norm(ALPHA * x + 0.5 * swiglu(x, lw['ffn1_w_gate'], lw['ffn1_w_up'], lw['ffn1_w_down']),
                   lw['ln1_g'], lw['ln1_b'])
    mixed, state = mix_fn(h, lw, *mix_args)
    h = layer_norm(ALPHA * h + mixed, lw['ln2_g'], lw['ln2_b'])
    h = layer_norm(ALPHA * h + 0.5 * swiglu(h, lw['ffn2_w_gate'], lw['ffn2_w_up'], lw['ffn2_w_down']),
                   lw['ln3_g'], lw['ln3_b'])
    return h, state


def setup_inputs(seed: int = 0) -> dict:
    key = jax.random.key(seed)
    ks = jax.random.split(key, 40)
    f32 = jnp.float32
    n_pages = PAST_LEN // PAGE_SIZE
    n_used = DEC_BATCH * n_pages
    n_pool = n_used + max(1, n_used // 4)
    win_buf = min(WINDOW, PAST_LEN)

    def nrm(k, shape, s):
        return s * jax.random.normal(k, shape, f32)

    page_shape = (DEPTH, n_pool, PAGE_SIZE, NSA_KV_HEADS, HEAD_DIM)
    win_shape = (DEPTH, DEC_BATCH, win_buf, NSA_KV_HEADS, HEAD_DIM)
    dt = jnp.exp(jax.random.uniform(ks[12], (DEPTH, SSM_HEADS), f32) * (math.log(0.1) - math.log(0.001))
                 + math.log(0.001))
    page_table = jax.random.permutation(ks[10], n_pool)[:n_used].reshape(DEC_BATCH, n_pages).astype(jnp.int32)
    return {
        'x_prompt': nrm(ks[0], (BATCH, SEQ, D_MODEL), 1.0),
        'x_sample': nrm(ks[1], (DEC_BATCH, DEC_SEQ, D_MODEL), 1.0),
        'cache_k_cmp': nrm(ks[2], page_shape, 1.0),
        'cache_v_cmp': nrm(ks[3], page_shape, 1.0),
        'cache_k_slc': nrm(ks[4], page_shape, 1.0),
        'cache_v_slc': nrm(ks[5], page_shape, 1.0),
        'cache_k_win': nrm(ks[6], win_shape, 1.0),
        'cache_v_win': nrm(ks[7], win_shape, 1.0),
        'state_ssm': nrm(ks[8], (DEPTH, DEC_BATCH, SSM_HEADS, SSM_HEAD_DIM, D_STATE), 0.5),
        'state_conv': nrm(ks[9], (DEPTH, DEC_BATCH, CONV_W - 1, CONV_DIM), 1.0),
        'page_table': page_table,
        'w_in': nrm(ks[11], (DEPTH, D_MODEL, D_IN), D_MODEL ** -0.5),
        'b_gate': nrm(ks[13], (DEPTH, NSA_HEADS * N_BRANCH), 0.1),
        'conv_w': nrm(ks[14], (DEPTH, CONV_W, CONV_DIM), CONV_W ** -0.5),
        'conv_b': nrm(ks[15], (DEPTH, CONV_DIM), 0.02),
        'dt_bias': dt + jnp.log(-jnp.expm1(-dt)),
        'a_log': jnp.log(jax.random.uniform(ks[16], (DEPTH, SSM_HEADS), f32, 1.0, 16.0)),
        'd_skip': 1.0 + nrm(ks[17], (DEPTH, SSM_HEADS), 0.1),
        'ssm_norm_w': 1.0 + nrm(ks[18], (DEPTH, D_SSM), 0.02),
        'cmp_k_w1': nrm(ks[19], (DEPTH, CMP_BLOCK, HEAD_DIM, CMP_HIDDEN), (CMP_BLOCK * HEAD_DIM) ** -0.5),
        'cmp_k_w2': nrm(ks[20], (DEPTH, CMP_HIDDEN, HEAD_DIM), CMP_HIDDEN ** -0.5),
        'cmp_k_pe': nrm(ks[21], (DEPTH, CMP_BLOCK, HEAD_DIM), 0.1),
        'cmp_v_w1': nrm(ks[22], (DEPTH, CMP_BLOCK, HEAD_DIM, CMP_HIDDEN), (CMP_BLOCK * HEAD_DIM) ** -0.5),
        'cmp_v_w2': nrm(ks[23], (DEPTH, CMP_HIDDEN, HEAD_DIM), CMP_HIDDEN ** -0.5),
        'cmp_v_pe': nrm(ks[24], (DEPTH, CMP_BLOCK, HEAD_DIM), 0.1),
        'w_out': nrm(ks[25], (DEPTH, D_MIX, D_MODEL), BETA * D_MIX ** -0.5),
        'ln1_g': 1.0 + nrm(ks[26], (DEPTH, D_MODEL), 0.02),
        'ln1_b': nrm(ks[27], (DEPTH, D_MODEL), 0.02),
        'ln2_g': 1.0 + nrm(ks[28], (DEPTH, D_MODEL), 0.02),
        'ln2_b': nrm(ks[29], (DEPTH, D_MODEL), 0.02),
        'ln3_g': 1.0 + nrm(ks[30], (DEPTH, D_MODEL), 0.02),
        'ln3_b': nrm(ks[31], (DEPTH, D_MODEL), 0.02),
        'ffn1_w_gate': nrm(ks[32], (DEPTH, D_MODEL, D_FF), D_MODEL ** -0.5),
        'ffn1_w_up': nrm(ks[33], (DEPTH, D_MODEL, D_FF), D_MODEL ** -0.5),
        'ffn1_w_down': nrm(ks[34], (DEPTH, D_FF, D_MODEL), BETA * D_FF ** -0.5),
        'ffn2_w_gate': nrm(ks[35], (DEPTH, D_MODEL, D_FF), D_MODEL ** -0.5),
        'ffn2_w_up': nrm(ks[36], (DEPTH, D_MODEL, D_FF), D_MODEL ** -0.5),
        'ffn2_w_down': nrm(ks[37], (DEPTH, D_FF, D_MODEL), BETA * D_FF ** -0.5),
    }


def reference(x_prompt, x_sample, cache_k_cmp, cache_v_cmp, cache_k_slc, cache_v_slc, cache_k_win,
              cache_v_win, state_ssm, state_conv, page_table, w_in, b_gate, conv_w, conv_b, dt_bias, a_log,
              d_skip, ssm_norm_w, cmp_k_w1, cmp_k_w2, cmp_k_pe, cmp_v_w1, cmp_v_w2, cmp_v_pe, w_out,
              ln1_g, ln1_b, ln2_g, ln2_b, ln3_g, ln3_b, ffn1_w_gate, ffn1_w_up, ffn1_w_down,
              ffn2_w_gate, ffn2_w_up, ffn2_w_down):
    weights = dict(w_in=w_in, b_gate=b_gate, conv_w=conv_w, conv_b=conv_b, dt_bias=dt_bias, a_log=a_log,
                   d_skip=d_skip, ssm_norm_w=ssm_norm_w, cmp_k_w1=cmp_k_w1, cmp_k_w2=cmp_k_w2,
                   cmp_k_pe=cmp_k_pe, cmp_v_w1=cmp_v_w1, cmp_v_w2=cmp_v_w2, cmp_v_pe=cmp_v_pe, w_out=w_out,
                   ln1_g=ln1_g, ln1_b=ln1_b, ln2_g=ln2_g, ln2_b=ln2_b, ln3_g=ln3_g, ln3_b=ln3_b,
                   ffn1_w_gate=ffn1_w_gate, ffn1_w_up=ffn1_w_up, ffn1_w_down=ffn1_w_down,
                   ffn2_w_gate=ffn2_w_gate, ffn2_w_up=ffn2_w_up, ffn2_w_down=ffn2_w_down)
    y_p, y_s = x_prompt, x_sample
    p_states, s_states = [], []
    for l in range(DEPTH):
        lw = {name: arr[l] for name, arr in weights.items()}
        y_p, st_p = layer_forward(y_p, lw, mix_prompt)
        y_s, st_s = layer_forward(y_s, lw, mix_sample, l, cache_k_cmp, cache_v_cmp, cache_k_slc, cache_v_slc,
                                  cache_k_win, cache_v_win, state_ssm, state_conv, page_table)
        p_states.append(st_p)
        s_states.append(st_s)
    p_kc, p_vc, p_ks, p_vs, p_kw, p_vw, p_ssm, p_conv = [jnp.stack(a) for a in zip(*p_states)]
    s_kc, s_vc, s_ks, s_vs, s_kw, s_vw, s_ssm, s_conv = [jnp.stack(a) for a in zip(*s_states)]
    return (y_p, y_s, p_kc, s_kc, p_vc, s_vc, p_ks, s_ks, p_vs, s_vs, p_kw, s_kw, p_vw, s_vw,
            p_ssm, s_ssm, p_conv, s_conv)
```

```python
import functools
import math

import numpy as np
import jax
import jax.numpy as jnp
from jax import lax
from jax.experimental import pallas as pl
from jax.experimental.pallas import tpu as pltpu

F32 = jnp.float32
BF16 = jnp.bfloat16
HI = lax.Precision.HIGHEST

HD = 64
N_SSM_HEADS = 8
N_SSM_GROUPS = 2
D_SSM = 512
CONV_DIM = 768
CONV_TAPS = 4
N_HEADS = 8
N_KV = 2
REP = N_HEADS // N_KV
D_NSA = 512
KV_DIM = N_KV * HD
CMP_STRIDE = 16
CMP_BLOCK = 32
CMP_HIDDEN = 128
SEL_BLOCK = 64
SEL_TOP_N = 16
WINDOW = 512
PAGE = 128
ROPE_THETA = 10000.0
LN_EPS = 1e-5
RMS_EPS = 1e-5
NEG = -1e30
FORCED_SCORE = 1e30
INVALID_SCORE = -1.0

LANES = 128
SUBLANES = 8
VMEM_LIMIT = 56 * 1024 * 1024

NT_DIMS = (((1,), (1,)), ((), ()))
TN_DIMS = (((0,), (0,)), ((), ()))


def _params(sem):
    return pltpu.CompilerParams(dimension_semantics=sem, vmem_limit_bytes=VMEM_LIMIT)


def _dot(a, b, precision=None):
    return jnp.dot(a, b, preferred_element_type=F32, precision=precision)


def _dot_nt(a, b, precision=None):
    return lax.dot_general(a, b, NT_DIMS, preferred_element_type=F32, precision=precision)


def _dot_tn(a, b, precision=None):
    return lax.dot_general(a, b, TN_DIMS, preferred_element_type=F32, precision=precision)


def _iota(shape, dim):
    return lax.broadcasted_iota(jnp.int32, shape, dim)


def _layer_norm(y, g, b):
    mu = jnp.mean(y, axis=-1, keepdims=True)
    yc = y - mu
    var = jnp.mean(yc * yc, axis=-1, keepdims=True)
    return yc * lax.rsqrt(var + LN_EPS) * g + b


def _ffn_ln_kernel(x_ref, wg_ref, wu_ref, wd_ref, g_ref, b_ref, o_ref, xb_ref, acc_ref, *, alpha):
    f = pl.program_id(1)

    @pl.when(f == 0)
    def _():
        xb_ref[...] = x_ref[...].astype(BF16)
        acc_ref[...] = jnp.zeros_like(acc_ref)

    xb = xb_ref[...]
    gate = _dot(xb, wg_ref[...])
    up = _dot(xb, wu_ref[...])
    act = (jax.nn.silu(gate) * up).astype(BF16)
    acc_ref[...] += _dot(act, wd_ref[...])

    @pl.when(f == pl.num_programs(1) - 1)
    def _():
        y = alpha * x_ref[...] + 0.5 * acc_ref[...]
        o_ref[...] = _layer_norm(y, g_ref[...], b_ref[...])


def _ffn_ln(x, wg, wu, wd, g, b, alpha):
    m, d = x.shape
    dff = wg.shape[1]
    tm = min(512, m)
    tf = dff // 2 if (dff // 2) % LANES == 0 else dff
    grid = (m // tm, dff // tf)
    return pl.pallas_call(
        functools.partial(_ffn_ln_kernel, alpha=alpha),
        grid=grid,
        in_specs=[
            pl.BlockSpec((tm, d), lambda i, f: (i, 0)),
            pl.BlockSpec((d, tf), lambda i, f: (0, f)),
            pl.BlockSpec((d, tf), lambda i, f: (0, f)),
            pl.BlockSpec((tf, d), lambda i, f: (f, 0)),
            pl.BlockSpec((1, d), lambda i, f: (0, 0)),
            pl.BlockSpec((1, d), lambda i, f: (0, 0)),
        ],
        out_specs=pl.BlockSpec((tm, d), lambda i, f: (i, 0)),
        out_shape=jax.ShapeDtypeStruct((m, d), F32),
        scratch_shapes=[pltpu.VMEM((tm, d), BF16), pltpu.VMEM((tm, d), F32)],
        compiler_params=_params(("parallel", "arbitrary")),
        name="ffn_ln",
    )(x, wg, wu, wd, g, b)


OFF_Z, OFF_XBC, OFF_Q, OFF_KV, OFF_MISC, W_IN_COLS = 0, 512, 1280, 1792, 2560, 2688


def _in_proj_kernel(h_ref, w_ref, bias_ref, cos_ref, sin_ref,
                    z_ref, xbc_ref, misc_ref, kv_ref, qb_ref, kvb_ref, *, q_scale):
    hb = h_ref[...].astype(BF16)
    z_ref[...] = _dot(hb, w_ref[:, OFF_Z:OFF_XBC])
    xbc_ref[...] = _dot(hb, w_ref[:, OFF_XBC:OFF_Q])
    misc_ref[...] = _dot(hb, w_ref[:, OFF_MISC:W_IN_COLS]) + bias_ref[...]
    cos = cos_ref[...]
    sin = sin_ref[...]
    lane = _iota(cos.shape, 1)
    first_half = (lane & (HD - 1)) < (HD // 2)

    def rope(x):
        rot = jnp.where(first_half, pltpu.roll(x, LANES - HD // 2, 1), pltpu.roll(x, HD // 2, 1))
        return x * cos + rot * sin

    for c in range(D_NSA // LANES):
        lo = OFF_Q + c * LANES
        q = rope(_dot(hb, w_ref[:, lo:lo + LANES]))
        qb_ref[:, c * LANES:(c + 1) * LANES] = (q * q_scale).astype(BF16)
    for c in range(6):
        lo = OFF_KV + c * LANES
        x = _dot(hb, w_ref[:, lo:lo + LANES])
        if c % 2 == 0:
            x = rope(x)
        kv_ref[:, c * LANES:(c + 1) * LANES] = x
        kvb_ref[:, c * LANES:(c + 1) * LANES] = x.astype(BF16)


def _in_proj(h, w, bias, cos, sin):
    m, d = h.shape
    tm = min(512, m, cos.shape[0])
    n_pos = cos.shape[0] // tm
    row = lambda i: (i, 0)
    fixed = lambda i: (0, 0)
    out_shapes = (
        jax.ShapeDtypeStruct((m, D_SSM), F32),
        jax.ShapeDtypeStruct((m, CONV_DIM), F32),
        jax.ShapeDtypeStruct((m, LANES), F32),
        jax.ShapeDtypeStruct((m, 6 * KV_DIM), F32),
        jax.ShapeDtypeStruct((m, D_NSA), BF16),
        jax.ShapeDtypeStruct((m, 6 * KV_DIM), BF16),
    )
    return pl.pallas_call(
        functools.partial(_in_proj_kernel, q_scale=HD ** -0.5),
        grid=(m // tm,),
        in_specs=[
            pl.BlockSpec((tm, d), row),
            pl.BlockSpec((d, W_IN_COLS), fixed),
            pl.BlockSpec((1, LANES), fixed),
            pl.BlockSpec((tm, LANES), lambda i: (i % n_pos, 0)),
            pl.BlockSpec((tm, LANES), lambda i: (i % n_pos, 0)),
        ],
        out_specs=tuple(pl.BlockSpec((tm, s.shape[1]), row) for s in out_shapes),
        out_shape=out_shapes,
        compiler_params=_params(("parallel",)),
        name="in_proj",
    )(h, w, bias, cos, sin)


def _ssd_kernel(z_ref, xbc_ref, misc_ref, misct_ref, hist_ref, h0_ref, convw_ref, convb_ref,
                dtb_row_ref, alog_row_ref, dtb_col_ref, alog_col_ref, dskip_ref, normw_ref,
                y_ref, hout_ref, xbuf, hstate, ybuf, *, chunk, n_valid):
    c = pl.program_id(1)
    L = chunk
    H = SUBLANES

    @pl.when(c == 0)
    def _():
        xbuf[0:H, :] = hist_ref[0]
        hstate[...] = h0_ref[0]

    x = xbc_ref[0]
    xbuf[H:H + L, :] = x
    conv = convb_ref[...] + convw_ref[CONV_TAPS - 1:CONV_TAPS, :] * x
    for k in range(1, CONV_TAPS):
        conv = conv + convw_ref[CONV_TAPS - 1 - k:CONV_TAPS - k, :] * xbuf[H - k:H - k + L, :]
    xbuf[0:H, :] = x[L - H:L, :]
    xact = jax.nn.silu(conv)
    bm = xact[:, D_SSM:D_SSM + N_SSM_GROUPS * HD]
    cm = xact[:, D_SSM + N_SSM_GROUPS * HD:]

    dt = jax.nn.softplus(misc_ref[0] + dtb_row_ref[...])
    dtt = jax.nn.softplus(misct_ref[0, 0:SUBLANES, :] + dtb_col_ref[...])
    if n_valid < L:
        dt = jnp.where(_iota(dt.shape, 0) < n_valid, dt, 0.0)
        dtt = jnp.where(_iota(dtt.shape, 1) < n_valid, dtt, 0.0)
    a_row = -jnp.exp(alog_row_ref[...])
    a_col = -jnp.exp(alog_col_ref[...])
    lane_ok = _iota(dt.shape, 1) < N_SSM_HEADS
    da = jnp.where(lane_ok, dt * a_row, 0.0)
    dat = dtt * a_col
    ri = _iota((L, L), 0)
    ci = _iota((L, L), 1)
    causal = ri >= ci
    tri = jnp.where(causal, 1.0, 0.0).astype(F32)
    cum = _dot(tri, da, HI)
    cumt = _dot_nt(dat, tri, HI)
    cum_last = cum[L - 1:L, :]

    rep = N_SSM_HEADS // N_SSM_GROUPS
    for g in range(N_SSM_GROUPS):
        cm_g = cm[:, g * HD:(g + 1) * HD]
        bm_g = bm[:, g * HD:(g + 1) * HD]
        cb = _dot_nt(cm_g, bm_g, HI)
        for r in range(rep):
            h = g * rep + r
            ch = cum[:, h:h + 1]
            seg = ch - cumt[h:h + 1, :]
            decay = jnp.where(causal, jnp.exp(jnp.where(causal, seg, 0.0)), 0.0)
            xs_h = xact[:, h * HD:(h + 1) * HD]
            xdt = xs_h * dt[:, h:h + 1]
            hprev = hstate[h]
            y_h = _dot(cb * decay, xdt, HI) + _dot_nt(cm_g, hprev, HI) * jnp.exp(ch)
            cl = cum_last[:, h:h + 1]
            tail = jnp.exp(cl - ch)
            hstate[h] = hprev * jnp.exp(cl) + _dot_tn(xdt * tail, bm_g, HI)
            ybuf[:, h * HD:(h + 1) * HD] = y_h

    xs = xact[:, :D_SSM]
    y = (ybuf[...] + dskip_ref[...] * xs) * jax.nn.silu(z_ref[0])
    y = y * lax.rsqrt(jnp.mean(y * y, axis=-1, keepdims=True) + RMS_EPS) * normw_ref[...]
    y_ref[0] = y

    @pl.when(c == pl.num_programs(1) - 1)
    def _():
        hout_ref[0] = hstate[...]


def _ssd(z, xbc, misc, misct, hist, h0, lw, chunk, n_valid):
    bsz, t, _ = z.shape
    nc = t // chunk
    seq = lambda b, c: (b, c, 0)
    per_b3 = lambda b, c: (b, 0, 0)
    per_b4 = lambda b, c: (b, 0, 0, 0)
    fixed = lambda b, c: (0, 0)
    return pl.pallas_call(
        functools.partial(_ssd_kernel, chunk=chunk, n_valid=n_valid),
        grid=(bsz, nc),
        in_specs=[
            pl.BlockSpec((1, chunk, D_SSM), seq),
            pl.BlockSpec((1, chunk, CONV_DIM), seq),
            pl.BlockSpec((1, chunk, LANES), seq),
            pl.BlockSpec((1, LANES, chunk), lambda b, c: (b, 0, c)),
            pl.BlockSpec((1, SUBLANES, CONV_DIM), per_b3),
            pl.BlockSpec((1, N_SSM_HEADS, HD, HD), per_b4),
            pl.BlockSpec((CONV_TAPS, CONV_DIM), fixed),
            pl.BlockSpec((1, CONV_DIM), fixed),
            pl.BlockSpec((1, LANES), fixed),
            pl.BlockSpec((1, LANES), fixed),
            pl.BlockSpec((SUBLANES, 1), fixed),
            pl.BlockSpec((SUBLANES, 1), fixed),
            pl.BlockSpec((1, D_SSM), fixed),
            pl.BlockSpec((1, D_SSM), fixed),
        ],
        out_specs=(
            pl.BlockSpec((1, chunk, D_SSM), seq),
            pl.BlockSpec((1, N_SSM_HEADS, HD, HD), per_b4),
        ),
        out_shape=(
            jax.ShapeDtypeStruct((bsz, t, D_SSM), F32),
            jax.ShapeDtypeStruct((bsz, N_SSM_HEADS, HD, HD), F32),
        ),
        scratch_shapes=[
            pltpu.VMEM((SUBLANES + chunk, CONV_DIM), F32),
            pltpu.VMEM((N_SSM_HEADS, HD, HD), F32),
            pltpu.VMEM((chunk, D_SSM), F32),
        ],
        compiler_params=_params(("parallel", "arbitrary")),
        name="ssd",
    )(z, xbc, misc, misct, hist, h0, lw["conv_w"], lw["conv_b_row"], lw["dtb_row"], lw["alog_row"],
      lw["dtb_col"], lw["alog_col"], lw["dskip_row"], lw["normw_row"])


def _compress_math(x, tail_second, w1_ref, w2_ref, pe_ref, sec_buf):
    n = x.shape[0]
    half = N_KV * CMP_HIDDEN
    out = _dot(x, w1_ref[...])
    sec_buf[0:n, :] = out[:, half:]
    sec_buf[n:n + SUBLANES, :] = tail_second
    pe_out = _dot(pe_ref[...], w1_ref[...])
    pe_term = pe_out[0:1, :half] + pe_out[SUBLANES:SUBLANES + 1, half:]
    pre = out[:, :half] + sec_buf[1:n + 1, :] + pe_term
    return _dot(jax.nn.gelu(pre).astype(BF16), w2_ref[...])


def _compress_kernel(x_ref, w1_ref, w2_ref, pe_ref, o_ref, sec_buf):
    x = x_ref[0]
    tail_second = jnp.zeros((SUBLANES, N_KV * CMP_HIDDEN), F32)
    o_ref[0] = _compress_math(x, tail_second, w1_ref, w2_ref, pe_ref, sec_buf)


def _compress(xb, w1big, w2big, pe2):
    bsz, n, width = xb.shape
    fixed = lambda b: (0, 0)
    return pl.pallas_call(
        _compress_kernel,
        grid=(bsz,),
        in_specs=[
            pl.BlockSpec((1, n, width), lambda b: (b, 0, 0)),
            pl.BlockSpec(w1big.shape, fixed),
            pl.BlockSpec(w2big.shape, fixed),
            pl.BlockSpec(pe2.shape, fixed),
        ],
        out_specs=pl.BlockSpec((1, n, KV_DIM), lambda b: (b, 0, 0)),
        out_shape=jax.ShapeDtypeStruct((bsz, n, KV_DIM), F32),
        scratch_shapes=[pltpu.VMEM((n + SUBLANES, N_KV * CMP_HIDDEN), F32)],
        compiler_params=_params(("parallel",)),
        name="compress",
    )(xb, w1big, w2big, pe2)


def _page_copy(cache_ref, buf, sem, page, slot, rows):
    return pltpu.make_async_copy(cache_ref.at[page], buf.at[pl.ds(slot * rows, rows)], sem)


def _gather_pages(pt_ref, b, cache_ref, buf, sem, n_pages, rows):
    def start(p, carry):
        _page_copy(cache_ref, buf, sem, pt_ref[b, p], p, rows).start()
        return carry

    def wait(p, carry):
        _page_copy(cache_ref, buf, sem, 0, p, rows).wait()
        return carry

    lax.fori_loop(0, n_pages, start, 0)
    lax.fori_loop(0, n_pages, wait, 0)


def _compress_paged_kernel(pt_ref, cache_ref, tail_ref, w1_ref, w2_ref, pe_ref, o_ref,
                           buf, sec_buf, sem, *, n_pages):
    b = pl.program_id(0)
    rows = PAGE // CMP_STRIDE
    _gather_pages(pt_ref, b, cache_ref, buf, sem, n_pages, rows)
    half = N_KV * CMP_HIDDEN
    tail_out = _dot(tail_ref[0].astype(BF16), w1_ref[...])
    o_ref[0] = _compress_math(buf[...].astype(BF16), tail_out[:, half:], w1_ref, w2_ref, pe_ref, sec_buf)


def _compress_paged(page_table, cache, tail, w1big, w2big, pe2):
    dbz, n_pages = page_table.shape
    rows = PAGE // CMP_STRIDE
    n = n_pages * rows
    width = cache.shape[-1]
    fixed = lambda b, pt: (0, 0)
    grid_spec = pltpu.PrefetchScalarGridSpec(
        num_scalar_prefetch=1,
        grid=(dbz,),
        in_specs=[
            pl.BlockSpec(memory_space=pl.ANY),
            pl.BlockSpec((1, SUBLANES, width), lambda b, pt: (b, 0, 0)),
            pl.BlockSpec(w1big.shape, fixed),
            pl.BlockSpec(w2big.shape, fixed),
            pl.BlockSpec(pe2.shape, fixed),
        ],
        out_specs=pl.BlockSpec((1, n, KV_DIM), lambda b, pt: (b, 0, 0)),
        scratch_shapes=[
            pltpu.VMEM((n, width), F32),
            pltpu.VMEM((n + SUBLANES, N_KV * CMP_HIDDEN), F32),
            pltpu.SemaphoreType.DMA(()),
        ],
    )
    return pl.pallas_call(
        functools.partial(_compress_paged_kernel, n_pages=n_pages),
        grid_spec=grid_spec,
        out_shape=jax.ShapeDtypeStruct((dbz, n, KV_DIM), F32),
        compiler_params=_params(("arbitrary",)),
        name="compress_paged",
    )(page_table, cache, tail, w1big, w2big, pe2)


def _select_blocks(imp_t, tpos, n_blocks, score_buf):
    j = _iota(imp_t.shape, 0)
    valid = (j * SEL_BLOCK <= tpos) & (j < n_blocks)
    cur = tpos // SEL_BLOCK
    forced = (j == 0) | (j == cur) | (j == cur - 1)
    score = jnp.where(valid, jnp.where(forced, FORCED_SCORE, imp_t), INVALID_SCORE)
    score_buf[...] = score

    def body(i, cnt):
        row = score_buf[pl.ds(i, 1), :]
        above = jnp.where(row > score, 1.0, 0.0)
        tie = jnp.where(row == score, jnp.where(j > i, 1.0, 0.0), 0.0)
        return cnt + above + tie

    cnt = lax.fori_loop(0, n_blocks, body, jnp.zeros(imp_t.shape, F32))
    return jnp.where(valid & (cnt < float(min(SEL_TOP_N, n_blocks))), 1.0, 0.0)


def _cmp_attn_kernel(q_ref, kc_ref, vc_ref, ovl_ref, oc_ref, selt_ref, score_buf, *, tq, n_blocks):
    t0 = pl.program_id(1) * tq
    ncp = kc_ref.shape[2]
    qpos = t0 + _iota((tq, ncp), 0)
    cidx = _iota((tq, ncp), 1)
    maskc = (cidx * CMP_STRIDE + (CMP_BLOCK - 1)) <= qpos
    maskf = jnp.where(maskc, 1.0, 0.0)
    nsp = ovl_ref.shape[0]
    tpos = t0 + _iota((nsp, tq), 1)
    for g in range(N_KV):
        kc = kc_ref[0, g]
        vc = vc_ref[0, g]
        psum = jnp.zeros((tq, ncp), F32)
        for r in range(REP):
            h = g * REP + r
            s = _dot_nt(q_ref[0, h], kc)
            sm = jnp.where(maskc, s, NEG)
            e = jnp.exp(sm - jnp.max(sm, axis=-1, keepdims=True))
            p = e / jnp.sum(e, axis=-1, keepdims=True) * maskf
            oc_ref[0, h] = _dot(p.astype(BF16), vc)
            psum = psum + p
        imp_t = _dot_nt(ovl_ref[...], psum, HI)
        selt_ref[0, g] = _select_blocks(imp_t, tpos, n_blocks, score_buf)


def _cmp_attn(qh, kch, vch, ovl_t, n_blocks, tq):
    bsz, nh, t, _ = qh.shape
    ncp = kch.shape[2]
    nsp = ovl_t.shape[0]
    return pl.pallas_call(
        functools.partial(_cmp_attn_kernel, tq=tq, n_blocks=n_blocks),
        grid=(bsz, t // tq),
        in_specs=[
            pl.BlockSpec((1, nh, tq, HD), lambda b, i: (b, 0, i, 0)),
            pl.BlockSpec((1, N_KV, ncp, HD), lambda b, i: (b, 0, 0, 0)),
            pl.BlockSpec((1, N_KV, ncp, HD), lambda b, i: (b, 0, 0, 0)),
            pl.BlockSpec((nsp, ncp), lambda b, i: (0, 0)),
        ],
        out_specs=(
            pl.BlockSpec((1, nh, tq, HD), lambda b, i: (b, 0, i, 0)),
            pl.BlockSpec((1, N_KV, nsp, tq), lambda b, i: (b, 0, 0, i)),
        ),
        out_shape=(
            jax.ShapeDtypeStruct((bsz, nh, t, HD), F32),
            jax.ShapeDtypeStruct((bsz, N_KV, nsp, t), F32),
        ),
        scratch_shapes=[pltpu.VMEM((nsp, tq), F32)],
        compiler_params=_params(("parallel", "parallel")),
        name="cmp_attn_select",
    )(qh, kch, vch, ovl_t)


def _softmax_parts(s, ok):
    sm = jnp.where(ok, s, NEG)
    m = jnp.max(sm, axis=-1, keepdims=True)
    e = jnp.exp(sm - m)
    return m, e


def _sel_win_kernel(q_ref, ks_ref, vs_ref, kw_ref, vw_ref, sel_ref, e_ref, oc_ref, misc_ref, o_ref,
                    *, tq, tk):
    g = pl.program_id(1)
    t0 = pl.program_id(2) * tq
    qpos_k = t0 + _iota((tq, tk), 0)
    lane_k = _iota((tq, tk), 1)
    sel = sel_ref[0, 0]
    n_kt = (t0 + tq + tk - 1) // tk

    def body(kt, carry):
        k0 = pl.multiple_of(kt * tk, tk)
        k = ks_ref[0, 0, pl.ds(k0, tk), :]
        v = vs_ref[0, 0, pl.ds(k0, tk), :]
        blk = _dot(sel, e_ref[kt])
        ok = jnp.where(k0 + lane_k <= qpos_k, blk, 0.0) > 0.5
        new = []
        for r in range(REP):
            m, l, acc = carry[r]
            s = _dot_nt(q_ref[0, r], k)
            sm = jnp.where(ok, s, NEG)
            m_new = jnp.maximum(m, jnp.max(sm, axis=-1, keepdims=True))
            alpha = jnp.exp(m - m_new)
            e = jnp.exp(sm - m_new)
            l = alpha * l + jnp.sum(e, axis=-1, keepdims=True)
            acc = alpha * acc + _dot(e.astype(BF16), v)
            new.append((m_new, l, acc))
        return tuple(new)

    init = tuple((jnp.full((tq, 1), NEG, F32), jnp.zeros((tq, 1), F32), jnp.zeros((tq, HD), F32))
                 for _ in range(REP))
    sel_state = lax.fori_loop(0, n_kt, body, init)

    wk = WINDOW + tq
    start = pl.multiple_of(jnp.maximum(t0 - WINDOW, 0), tq)
    kw = kw_ref[0, 0, pl.ds(start, wk), :]
    vw = vw_ref[0, 0, pl.ds(start, wk), :]
    kpos = start + _iota((tq, wk), 1)
    qpos_w = t0 + _iota((tq, wk), 0)
    ok_w = jnp.where(kpos <= qpos_w, jnp.where(kpos > qpos_w - WINDOW, 1.0, 0.0), 0.0) > 0.5

    gates = jax.nn.sigmoid(misc_ref[0])
    gate_lane = _iota(gates.shape, 1) - (N_SSM_HEADS + 3 * REP * g)
    for r in range(REP):
        m, l, acc = sel_state[r]
        o_s = acc / l
        _, e = _softmax_parts(_dot_nt(q_ref[0, r], kw), ok_w)
        o_w = _dot(e.astype(BF16), vw) / jnp.sum(e, axis=-1, keepdims=True)
        gh = [jnp.sum(jnp.where(gate_lane == 3 * r + br, gates, 0.0), axis=-1, keepdims=True)
              for br in range(3)]
        o = gh[0] * oc_ref[0, r] + gh[1] * o_s + gh[2] * o_w
        o_ref[0, r] = o.astype(o_ref.dtype)


def _sel_win_attn(qh, ksh, vsh, kwh, vwh, sel, e3, och, misc, tq, tk):
    bsz, nh, t, _ = qh.shape
    nsp = sel.shape[-1]
    per_bg = lambda b, g, i: (b, g, 0, 0)
    return pl.pallas_call(
        functools.partial(_sel_win_kernel, tq=tq, tk=tk),
        grid=(bsz, N_KV, t // tq),
        in_specs=[
            pl.BlockSpec((1, REP, tq, HD), lambda b, g, i: (b, g, i, 0)),
            pl.BlockSpec((1, 1, t, HD), per_bg),
            pl.BlockSpec((1, 1, t, HD), per_bg),
            pl.BlockSpec((1, 1, t, HD), per_bg),
            pl.BlockSpec((1, 1, t, HD), per_bg),
            pl.BlockSpec((1, 1, tq, nsp), lambda b, g, i: (b, g, i, 0)),
            pl.BlockSpec(e3.shape, lambda b, g, i: (0, 0, 0)),
            pl.BlockSpec((1, REP, tq, HD), lambda b, g, i: (b, g, i, 0)),
            pl.BlockSpec((1, tq, LANES), lambda b, g, i: (b, i, 0)),
        ],
        out_specs=pl.BlockSpec((1, REP, tq, HD), lambda b, g, i: (b, g, i, 0)),
        out_shape=jax.ShapeDtypeStruct((bsz, nh, t, HD), BF16),
        compiler_params=_params(("parallel", "parallel", "arbitrary")),
        name="sel_win_attn",
    )(qh, ksh, vsh, kwh, vwh, sel, e3, och, misc)


def _out_proj_kernel(y_ref, o_ref, h_ref, w_ref, g_ref, b_ref, out_ref, *, alpha):
    mixed = _dot(y_ref[...].astype(BF16), w_ref[0:D_SSM, :]) + _dot(o_ref[...], w_ref[D_SSM:, :])
    out_ref[...] = _layer_norm(alpha * h_ref[...] + mixed, g_ref[...], b_ref[...])


def _out_proj_ln(y, o, h, w, g, b, alpha):
    m, d = h.shape
    tm = min(512, m)
    row = lambda i: (i, 0)
    fixed = lambda i: (0, 0)
    return pl.pallas_call(
        functools.partial(_out_proj_kernel, alpha=alpha),
        grid=(m // tm,),
        in_specs=[
            pl.BlockSpec((tm, D_SSM), row),
            pl.BlockSpec((tm, D_NSA), row),
            pl.BlockSpec((tm, d), row),
            pl.BlockSpec(w.shape, fixed),
            pl.BlockSpec((1, d), fixed),
            pl.BlockSpec((1, d), fixed),
        ],
        out_specs=pl.BlockSpec((tm, d), row),
        out_shape=jax.ShapeDtypeStruct((m, d), F32),
        compiler_params=_params(("parallel",)),
        name="out_proj_ln",
    )(y, o, h, w, g, b)


def _row_group(shape):
    return _iota(shape, 0) // (shape[0] // N_KV)


def _cmp_attn_sample_kernel(q_ref, kc_ref, vc_ref, ssum_ref, oc_ref, psum_ref, *, past, s_len):
    q = q_ref[0]
    rows = q.shape[0]
    kc = kc_ref[0].astype(BF16)
    vc = vc_ref[0].astype(BF16)
    nc = kc.shape[0]
    t_row = (_iota((rows, nc), 0) // REP) % s_len
    cidx = _iota((rows, nc), 1)
    maskc = (cidx * CMP_STRIDE + (CMP_BLOCK - 1)) <= past + t_row
    s = _dot_nt(q, kc)
    sm = jnp.where(maskc, s, NEG)
    e = jnp.exp(sm - jnp.max(sm, axis=-1, keepdims=True))
    p = e / jnp.sum(e, axis=-1, keepdims=True) * jnp.where(maskc, 1.0, 0.0)
    o = _dot(p.astype(BF16), vc)
    own = (_iota(o.shape, 1) // HD) == _row_group(o.shape)
    oc_ref[0] = jnp.where(own, o, 0.0)
    psum_ref[0] = _dot(ssum_ref[...], p, HI)


def _cmp_attn_sample(qbd, kc, vc, ssum, past, s_len):
    dbz, rows, _ = qbd.shape
    nc = kc.shape[1]
    ng = ssum.shape[0]
    per_b = lambda b: (b, 0, 0)
    return pl.pallas_call(
        functools.partial(_cmp_attn_sample_kernel, past=past, s_len=s_len),
        grid=(dbz,),
        in_specs=[
            pl.BlockSpec((1, rows, KV_DIM), per_b),
            pl.BlockSpec((1, nc, KV_DIM), per_b),
            pl.BlockSpec((1, nc, KV_DIM), per_b),
            pl.BlockSpec(ssum.shape, lambda b: (0, 0)),
        ],
        out_specs=(pl.BlockSpec((1, rows, KV_DIM), per_b), pl.BlockSpec((1, ng, nc), per_b)),
        out_shape=(jax.ShapeDtypeStruct((dbz, rows, KV_DIM), F32),
                   jax.ShapeDtypeStruct((dbz, ng, nc), F32)),
        compiler_params=_params(("parallel",)),
        name="cmp_attn_sample",
    )(qbd, kc, vc, ssum)


def _select_sample_kernel(psum_ref, ovl_ref, selt_ref, score_buf, *, past, s_len, n_blocks):
    imp_t = _dot_nt(ovl_ref[...], psum_ref[...], HI)
    tpos = past + _iota(imp_t.shape, 1) % s_len
    selt_ref[...] = _select_blocks(imp_t, tpos, n_blocks, score_buf)


def _select_sample(psum_all, ovl_t, past, s_len, n_blocks):
    nsp = ovl_t.shape[0]
    cols = psum_all.shape[0]
    return pl.pallas_call(
        functools.partial(_select_sample_kernel, past=past, s_len=s_len, n_blocks=n_blocks),
        out_shape=jax.ShapeDtypeStruct((nsp, cols), F32),
        scratch_shapes=[pltpu.VMEM((nsp, cols), F32)],
        compiler_params=pltpu.CompilerParams(vmem_limit_bytes=VMEM_LIMIT),
        name="select_sample",
    )(psum_all, ovl_t)


def _sel_win_sample_kernel(pt_ref, kcache_ref, vcache_ref, q_ref, sel_ref, e_ref, kt_ref, vt_ref,
                           kw_ref, vw_ref, oc_ref, gate_ref, o_ref, kbuf, vbuf, ksem, vsem,
                           *, n_pages, s_len, n_chunks, chunk):
    b = pl.program_id(0)

    def start(p, carry):
        _page_copy(kcache_ref, kbuf, ksem, pt_ref[b, p], p, PAGE).start()
        _page_copy(vcache_ref, vbuf, vsem, pt_ref[b, p], p, PAGE).start()
        return carry

    def wait(p, carry):
        _page_copy(kcache_ref, kbuf, ksem, 0, p, PAGE).wait()
        _page_copy(vcache_ref, vbuf, vsem, 0, p, PAGE).wait()
        return carry

    lax.fori_loop(0, n_pages, start, 0)
    lax.fori_loop(0, n_pages, wait, 0)

    q = q_ref[0]
    rows = q.shape[0]
    sel = sel_ref[0]
    sel_main = sel[:, :LANES]
    past = n_pages * PAGE
    t_col = (_iota((rows, 1), 0) // REP) % s_len

    m = jnp.full((rows, 1), NEG, F32)
    l = jnp.zeros((rows, 1), F32)
    acc = jnp.zeros((rows, KV_DIM), F32)
    for c in range(n_chunks):
        k = kbuf[c * chunk:(c + 1) * chunk, :].astype(BF16)
        v = vbuf[c * chunk:(c + 1) * chunk, :].astype(BF16)
        blk = _dot(sel_main, e_ref[c])
        kpos = c * chunk + _iota((rows, chunk), 1)
        ok = jnp.where(kpos <= past + t_col, blk, 0.0) > 0.5
        sm = jnp.where(ok, _dot_nt(q, k), NEG)
        m_new = jnp.maximum(m, jnp.max(sm, axis=-1, keepdims=True))
        alpha = jnp.exp(m - m_new)
        e = jnp.exp(sm - m_new)
        l = alpha * l + jnp.sum(e, axis=-1, keepdims=True)
        acc = alpha * acc + _dot(e.astype(BF16), v)
        m = m_new
    kt = kt_ref[0].astype(BF16)
    vt = vt_ref[0].astype(BF16)
    lane_t = _iota((rows, kt.shape[0]), 1)
    sel_new = sel[:, LANES:LANES + 1].astype(F32)
    ok = jnp.where(lane_t <= t_col, jnp.where(lane_t < s_len, sel_new, 0.0), 0.0) > 0.5
    sm = jnp.where(ok, _dot_nt(q, kt), NEG)
    m_new = jnp.maximum(m, jnp.max(sm, axis=-1, keepdims=True))
    alpha = jnp.exp(m - m_new)
    e = jnp.exp(sm - m_new)
    l = alpha * l + jnp.sum(e, axis=-1, keepdims=True)
    acc = alpha * acc + _dot(e.astype(BF16), vt)
    o_s = acc / l

    kw = kw_ref[0].astype(BF16)
    vw = vw_ref[0].astype(BF16)
    nw = kw.shape[0]
    wb = nw - SUBLANES
    iw = _iota((rows, nw), 1)
    ok_w = jnp.where(iw <= wb + t_col,
                     jnp.where(iw > wb + t_col - WINDOW, jnp.where(iw < wb + s_len, 1.0, 0.0), 0.0),
                     0.0) > 0.5
    _, e = _softmax_parts(_dot_nt(q, kw), ok_w)
    o_w = _dot(e.astype(BF16), vw) / jnp.sum(e, axis=-1, keepdims=True)

    gates = jax.nn.sigmoid(gate_ref[0])
    o = gates[:, 0:1] * oc_ref[0] + gates[:, 1:2] * o_s + gates[:, 2:3] * o_w
    own = (_iota(o.shape, 1) // HD) == _row_group(o.shape)
    o = jnp.where(own, o, 0.0)
    o_ref[0] = o[:, :HD] + o[:, HD:]


def _sel_win_sample(page_table, kcache, vcache, qbd, selx, e4, ktail, vtail, kwin, vwin, oc, graw, s_len):
    dbz, n_pages = page_table.shape
    rows = qbd.shape[1]
    n_chunks, _, chunk = e4.shape
    per_b = lambda b, pt: (b, 0, 0)
    blk = lambda a: pl.BlockSpec((1,) + a.shape[1:], per_b)
    grid_spec = pltpu.PrefetchScalarGridSpec(
        num_scalar_prefetch=1,
        grid=(dbz,),
        in_specs=[
            pl.BlockSpec(memory_space=pl.ANY),
            pl.BlockSpec(memory_space=pl.ANY),
            blk(qbd), blk(selx),
            pl.BlockSpec(e4.shape, lambda b, pt: (0, 0, 0)),
            blk(ktail), blk(vtail), blk(kwin), blk(vwin), blk(oc), blk(graw),
        ],
        out_specs=pl.BlockSpec((1, rows, HD), per_b),
        scratch_shapes=[
            pltpu.VMEM((n_pages * PAGE, KV_DIM), F32),
            pltpu.VMEM((n_pages * PAGE, KV_DIM), F32),
            pltpu.SemaphoreType.DMA(()),
            pltpu.SemaphoreType.DMA(()),
        ],
    )
    return pl.pallas_call(
        functools.partial(_sel_win_sample_kernel, n_pages=n_pages, s_len=s_len,
                          n_chunks=n_chunks, chunk=chunk),
        grid_spec=grid_spec,
        out_shape=jax.ShapeDtypeStruct((dbz, rows, HD), F32),
        compiler_params=_params(("arbitrary",)),
        name="sel_win_sample",
    )(page_table, kcache, vcache, qbd, selx, e4, ktail, vtail, kwin, vwin, oc, graw)


def _rope_tables(pos):
    half = HD // 2
    inv = ROPE_THETA ** (-jnp.arange(half, dtype=F32) / half)
    ang = pos.astype(F32)[:, None] * inv[None, :]
    cos = jnp.cos(ang)
    sin = jnp.sin(ang)
    reps = LANES // HD
    cos_l = jnp.tile(jnp.concatenate([cos, cos], axis=-1), (1, reps))
    sin_l = jnp.tile(jnp.concatenate([-sin, sin], axis=-1), (1, reps))
    return cos_l, sin_l


def _overlap_t(nc, ncp, ns, nsp):
    c_start = np.arange(ncp) * CMP_STRIDE
    s_start = np.arange(nsp) * SEL_BLOCK
    ovl = ((c_start[None, :] + CMP_BLOCK > s_start[:, None]) & (c_start[None, :] < s_start[:, None] + SEL_BLOCK))
    ovl = ovl & (np.arange(ncp)[None, :] < nc) & (np.arange(nsp)[:, None] < ns)
    return jnp.asarray(ovl.astype(np.float32))


def _expander(n_rows, n_chunks, chunk, dtype=BF16):
    key_block = (np.arange(n_chunks)[:, None] * chunk + np.arange(chunk)[None, :]) // SEL_BLOCK
    e = key_block[:, None, :] == np.arange(n_rows)[None, :, None]
    return jnp.asarray(e.astype(np.float32)).astype(dtype)


def _round_up(x, m):
    return -(-x // m) * m


def _compress_weights(w1, w2, pe):
    eye = jnp.eye(N_KV, dtype=F32)
    halves = []
    for half in range(2):
        wh = w1[half * CMP_STRIDE:(half + 1) * CMP_STRIDE]
        big = jnp.einsum("sdh,gk->sgdkh", wh, eye)
        halves.append(big.reshape(CMP_STRIDE * KV_DIM, N_KV * CMP_HIDDEN))
    w1big = jnp.concatenate(halves, axis=1).astype(BF16)
    w2big = jnp.einsum("hd,gk->ghkd", w2, eye).reshape(N_KV * CMP_HIDDEN, KV_DIM).astype(BF16)
    pe_rows = []
    for half in range(2):
        ph = pe[half * CMP_STRIDE:(half + 1) * CMP_STRIDE]
        row = jnp.broadcast_to(ph[:, None, :], (CMP_STRIDE, N_KV, HD)).reshape(1, CMP_STRIDE * KV_DIM)
        pe_rows.append(jnp.broadcast_to(row, (SUBLANES, CMP_STRIDE * KV_DIM)))
    pe2 = jnp.concatenate(pe_rows, axis=0).astype(BF16)
    return w1big, w2big, pe2


def _layer_weights(w, l):
    d_in_z, d_in_xbc, d_in_dt, d_in_q, d_in_kv, d_in_gate = D_SSM, CONV_DIM, N_SSM_HEADS, D_NSA, 6 * KV_DIM, N_HEADS * 3
    offs = np.cumsum([0, d_in_z, d_in_xbc, d_in_dt, d_in_q, d_in_kv, d_in_gate])
    w_in = w["w_in"][l]
    seg = lambda i: w_in[:, offs[i]:offs[i + 1]]
    pad = LANES - d_in_dt - d_in_gate
    w_in_r = jnp.concatenate([seg(0), seg(1), seg(3), seg(4), seg(2), seg(5),
                              jnp.zeros((w_in.shape[0], pad), F32)], axis=1).astype(BF16)
    bias_misc = jnp.concatenate([jnp.zeros((d_in_dt,), F32), w["b_gate"][l], jnp.zeros((pad,), F32)])[None, :]
    lane_pad = lambda v: jnp.concatenate([v, jnp.zeros((LANES - v.shape[0],), F32)])[None, :]
    lw = {
        "w_in": w_in_r,
        "bias_misc": bias_misc,
        "conv_w": w["conv_w"][l],
        "conv_b_row": w["conv_b"][l][None, :],
        "dtb_row": lane_pad(w["dt_bias"][l]),
        "alog_row": lane_pad(w["a_log"][l]),
        "dtb_col": w["dt_bias"][l][:, None],
        "alog_col": w["a_log"][l][:, None],
        "dskip_row": jnp.repeat(w["d_skip"][l], HD)[None, :],
        "normw_row": w["ssm_norm_w"][l][None, :],
        "w_out": w["w_out"][l].astype(BF16),
    }
    lw["cmp_k"] = _compress_weights(w["cmp_k_w1"][l], w["cmp_k_w2"][l], w["cmp_k_pe"][l])
    lw["cmp_v"] = _compress_weights(w["cmp_v_w1"][l], w["cmp_v_w2"][l], w["cmp_v_pe"][l])
    for i in (1, 2, 3):
        lw[f"ln{i}"] = (w[f"ln{i}_g"][l][None, :], w[f"ln{i}_b"][l][None, :])
    for i in (1, 2):
        lw[f"ffn{i}"] = (w[f"ffn{i}_w_gate"][l].astype(BF16), w[f"ffn{i}_w_up"][l].astype(BF16),
                         w[f"ffn{i}_w_down"][l].astype(BF16))
    return lw


def _split_kv(kv, bsz, t):
    kv = kv.reshape(bsz, t, 6, N_KV, HD)
    return [kv[:, :, i] for i in range(6)]


def _heads_major(x, bsz, t, n):
    return x.reshape(bsz, t, n, HD).transpose(0, 2, 1, 3)


def _mix_prompt(h, lw, bsz, t):
    m = bsz * t
    cos, sin = _rope_tables(jnp.arange(t))
    z, xbc, misc, kv, qb, kvb = _in_proj(h, lw["w_in"], lw["bias_misc"], cos, sin)
    k_c, v_c, k_s, v_s, k_w, v_w = _split_kv(kv, bsz, t)

    chunk = min(128, t)
    misc3 = misc.reshape(bsz, t, LANES)
    xbc3 = xbc.reshape(bsz, t, CONV_DIM)
    y_ssd, h_ssm = _ssd(z.reshape(bsz, t, D_SSM), xbc3, misc3, misc3.transpose(0, 2, 1),
                        jnp.zeros((bsz, SUBLANES, CONV_DIM), F32),
                        jnp.zeros((bsz, N_SSM_HEADS, HD, HD), F32), lw, chunk, chunk)
    conv_state = xbc3[:, t - (CONV_TAPS - 1):]

    kvb3 = kvb.reshape(bsz, t, 6, KV_DIM)
    n_str = t // CMP_STRIDE
    nc = n_str - 1
    ns = -(-t // SEL_BLOCK)
    nsp = _round_up(ns, SUBLANES)
    kc = _compress(kvb3[:, :, 0].reshape(bsz, n_str, CMP_STRIDE * KV_DIM), *lw["cmp_k"])
    vc = _compress(kvb3[:, :, 1].reshape(bsz, n_str, CMP_STRIDE * KV_DIM), *lw["cmp_v"])
    kch = _heads_major(kc, bsz, n_str, N_KV).astype(BF16)
    vch = _heads_major(vc, bsz, n_str, N_KV).astype(BF16)
    qh = _heads_major(qb, bsz, t, N_HEADS)
    tq = min(128, t)
    och, selt = _cmp_attn(qh, kch, vch, _overlap_t(nc, n_str, ns, nsp), ns, tq)
    sel = selt.transpose(0, 1, 3, 2).astype(BF16)
    tk = min(512, t)
    e3 = _expander(nsp, t // tk, tk)
    hm = lambda i: _heads_major(kvb3[:, :, i], bsz, t, N_KV)
    o = _sel_win_attn(qh, hm(2), hm(3), hm(4), hm(5), sel, e3, och, misc3, tq, tk)
    o_nsa = o.transpose(0, 2, 1, 3).reshape(m, D_NSA)

    wb = min(WINDOW, t)
    state = (k_c, v_c, k_s, v_s, k_w[:, t - wb:], v_w[:, t - wb:], h_ssm, conv_state)
    return y_ssd.reshape(m, D_SSM), o_nsa, state


def _mix_sample(h, lw, l, dbz, s_len, caches, state_ssm, state_conv, page_table):
    cache_k_cmp, cache_v_cmp, cache_k_slc, cache_v_slc, cache_k_win, cache_v_win = caches
    m = dbz * s_len
    n_pages = page_table.shape[1]
    past = n_pages * PAGE
    pos = past + jnp.arange(s_len)
    tm = min(512, m)
    cos, sin = _rope_tables(jnp.tile(pos, tm // s_len))
    z, xbc, misc, kv, qb, kvb = _in_proj(h, lw["w_in"], lw["bias_misc"], cos, sin)
    k_c, v_c, k_s, v_s, k_w, v_w = _split_kv(kv, dbz, s_len)

    chunk = SUBLANES
    padt = lambda a: jnp.pad(a, ((0, 0), (0, chunk - s_len), (0, 0)))
    misc3 = misc.reshape(dbz, s_len, LANES)
    xbc3 = xbc.reshape(dbz, s_len, CONV_DIM)
    hist = jnp.pad(state_conv[l], ((0, 0), (SUBLANES - (CONV_TAPS - 1), 0), (0, 0)))
    y_ssd, h_ssm = _ssd(padt(z.reshape(dbz, s_len, D_SSM)), padt(xbc3), padt(misc3),
                        padt(misc3).transpose(0, 2, 1), hist, state_ssm[l], lw, chunk, s_len)
    y_ssd = y_ssd[:, :s_len]
    conv_state = jnp.concatenate([state_conv[l], xbc3], axis=1)[:, -(CONV_TAPS - 1):]

    width = CMP_STRIDE * KV_DIM
    n_pool = cache_k_cmp.shape[1]
    rows_pp = PAGE // CMP_STRIDE

    def tail_rows(new):
        flat = new.reshape(dbz, 1, s_len * KV_DIM)
        return jnp.pad(flat, ((0, 0), (0, SUBLANES - 1), (0, width - s_len * KV_DIM)))

    kc = _compress_paged(page_table, cache_k_cmp[l].reshape(n_pool, rows_pp, width), tail_rows(k_c), *lw["cmp_k"])
    vc = _compress_paged(page_table, cache_v_cmp[l].reshape(n_pool, rows_pp, width), tail_rows(v_c), *lw["cmp_v"])
    nc = kc.shape[1]
    total = past + s_len
    ns = -(-total // SEL_BLOCK)
    nsp = _round_up(ns, SUBLANES)

    rows = N_KV * s_len * REP
    qg = qb.reshape(dbz, s_len, N_KV, REP, HD).transpose(0, 2, 1, 3, 4)
    qbd = jnp.einsum("bgtrd,gk->bgtrkd", qg, jnp.eye(N_KV, dtype=BF16)).reshape(dbz, rows, KV_DIM)
    ng = N_KV * s_len
    ssum = jnp.asarray((np.arange(ng)[:, None] == np.arange(rows)[None, :] // REP).astype(np.float32))
    oc, psum = _cmp_attn_sample(qbd, kc, vc, ssum, past, s_len)
    selt = _select_sample(psum.reshape(dbz * ng, nc), _overlap_t(nc, nc, ns, nsp), past, s_len, ns)
    sel = selt.T.reshape(dbz, ng, 1, nsp)
    sel = jnp.broadcast_to(sel, (dbz, ng, REP, nsp)).reshape(dbz, rows, nsp)
    n_cached = past // SEL_BLOCK
    selx = jnp.concatenate([sel[:, :, :n_cached], jnp.zeros((dbz, rows, LANES - n_cached), F32),
                            sel[:, :, n_cached:n_cached + 1], jnp.zeros((dbz, rows, LANES - 1), F32)],
                           axis=-1).astype(BF16)
    chunk_k = min(2048, past)
    e4 = _expander(LANES, past // chunk_k, chunk_k)

    def new_block(new):
        return jnp.pad(new.reshape(dbz, s_len, KV_DIM), ((0, 0), (0, LANES - s_len), (0, 0)))

    kw_full = jnp.concatenate([cache_k_win[l].reshape(dbz, -1, KV_DIM), k_w.reshape(dbz, s_len, KV_DIM)], axis=1)
    vw_full = jnp.concatenate([cache_v_win[l].reshape(dbz, -1, KV_DIM), v_w.reshape(dbz, s_len, KV_DIM)], axis=1)
    wb = cache_k_win.shape[2]
    padw = lambda a: jnp.pad(a, ((0, 0), (0, SUBLANES - s_len), (0, 0)))
    gate = misc3[:, :, N_SSM_HEADS:N_SSM_HEADS + 3 * N_HEADS].reshape(dbz, s_len, N_KV, REP, 3)
    graw = jnp.pad(gate.transpose(0, 2, 1, 3, 4).reshape(dbz, rows, 3), ((0, 0), (0, 0), (0, LANES - 3)))
    o = _sel_win_sample(page_table, cache_k_slc[l].reshape(n_pool, PAGE, KV_DIM),
                        cache_v_slc[l].reshape(n_pool, PAGE, KV_DIM), qbd, selx, e4,
                        new_block(k_s), new_block(v_s), padw(kw_full), padw(vw_full), oc, graw, s_len)
    o_nsa = o.reshape(dbz, N_KV, s_len, REP, HD).transpose(0, 2, 1, 3, 4).reshape(m, D_NSA).astype(BF16)

    state = (k_c, v_c, k_s, v_s,
             kw_full[:, -wb:].reshape(dbz, wb, N_KV, HD), vw_full[:, -wb:].reshape(dbz, wb, N_KV, HD),
             h_ssm, conv_state)
    return y_ssd.reshape(m, D_SSM), o_nsa, state


def _layer(x, lw, alpha, mix_fn):
    h1 = _ffn_ln(x, *lw["ffn1"], *lw["ln1"], alpha)
    y_ssd, o_nsa, state = mix_fn(h1)
    h2 = _out_proj_ln(y_ssd, o_nsa, h1, lw["w_out"], *lw["ln2"], alpha)
    return _ffn_ln(h2, *lw["ffn2"], *lw["ln3"], alpha), state


def kernel(x_prompt, x_sample, cache_k_cmp, cache_v_cmp, cache_k_slc, cache_v_slc, cache_k_win, cache_v_win, state_ssm, state_conv, page_table, w_in, b_gate, conv_w, conv_b, dt_bias, a_log, d_skip, ssm_norm_w, cmp_k_w1, cmp_k_w2, cmp_k_pe, cmp_v_w1, cmp_v_w2, cmp_v_pe, w_out, ln1_g, ln1_b, ln2_g, ln2_b, ln3_g, ln3_b, ffn1_w_gate, ffn1_w_up, ffn1_w_down, ffn2_w_gate, ffn2_w_up, ffn2_w_down):
    weights = dict(w_in=w_in, b_gate=b_gate, conv_w=conv_w, conv_b=conv_b, dt_bias=dt_bias, a_log=a_log,
                   d_skip=d_skip, ssm_norm_w=ssm_norm_w, cmp_k_w1=cmp_k_w1, cmp_k_w2=cmp_k_w2,
                   cmp_k_pe=cmp_k_pe, cmp_v_w1=cmp_v_w1, cmp_v_w2=cmp_v_w2, cmp_v_pe=cmp_v_pe, w_out=w_out,
                   ln1_g=ln1_g, ln1_b=ln1_b, ln2_g=ln2_g, ln2_b=ln2_b, ln3_g=ln3_g, ln3_b=ln3_b,
                   ffn1_w_gate=ffn1_w_gate, ffn1_w_up=ffn1_w_up, ffn1_w_down=ffn1_w_down,
                   ffn2_w_gate=ffn2_w_gate, ffn2_w_up=ffn2_w_up, ffn2_w_down=ffn2_w_down)
    depth = w_in.shape[0]
    bsz, t, d = x_prompt.shape
    dbz, s_len, _ = x_sample.shape
    alpha = (2.0 * depth) ** 0.25
    caches = (cache_k_cmp, cache_v_cmp, cache_k_slc, cache_v_slc, cache_k_win, cache_v_win)
    y_p = x_prompt.reshape(bsz * t, d)
    y_s = x_sample.reshape(dbz * s_len, d)
    p_states, s_states = [], []
    for l in range(depth):
        lw = _layer_weights(weights, l)
        y_p, st_p = _layer(y_p, lw, alpha, lambda h: _mix_prompt(h, lw, bsz, t))
        y_s, st_s = _layer(y_s, lw, alpha, lambda h: _mix_sample(h, lw, l, dbz, s_len, caches, state_ssm,
                                                                 state_conv, page_table))
        p_states.append(st_p)
        s_states.append(st_s)
    p_st = [jnp.stack(a) for a in zip(*p_states)]
    s_st = [jnp.stack(a) for a in zip(*s_states)]
    outs = [y_p.reshape(bsz, t, d), y_s.reshape(dbz, s_len, d)]
    for p, s in zip(p_st[:6], s_st[:6]):
        outs += [p, s]
    outs += [p_st[6], s_st[6], p_st[7], s_st[7]]
    return tuple(outs)
```

```python
import functools
import math

import numpy as np
import jax
import jax.numpy as jnp
from jax import lax
from jax.experimental import pallas as pl
from jax.experimental.pallas import tpu as pltpu

F32 = jnp.float32
BF16 = jnp.bfloat16
HI = lax.Precision.HIGHEST

HD = 64
N_SSM_HEADS = 8
N_SSM_GROUPS = 2
D_SSM = 512
CONV_DIM = 768
CONV_TAPS = 4
N_HEADS = 8
N_KV = 2
REP = N_HEADS // N_KV
D_NSA = 512
KV_DIM = N_KV * HD
N_KV_PROJ = 6
CMP_STRIDE = 16
CMP_BLOCK = 32
CMP_HIDDEN = 128
SEL_BLOCK = 64
SEL_TOP_N = 16
WINDOW = 512
PAGE = 128
ROPE_THETA = 10000.0
LN_EPS = 1e-5
RMS_EPS = 1e-5
NEG = -1e30
FORCED_SCORE = 1e30
INVALID_SCORE = -1.0

LANES = 128
SUBLANES = 8
VMEM_LIMIT = 56 * 1024 * 1024

NT_DIMS = (((1,), (1,)), ((), ()))
TN_DIMS = (((0,), (0,)), ((), ()))


def _params(sem):
    return pltpu.CompilerParams(dimension_semantics=sem, vmem_limit_bytes=VMEM_LIMIT)


def _dot(a, b, precision=None):
    return jnp.dot(a, b, preferred_element_type=F32, precision=precision)


def _dot_nt(a, b, precision=None):
    return lax.dot_general(a, b, NT_DIMS, preferred_element_type=F32, precision=precision)


def _dot_tn(a, b, precision=None):
    return lax.dot_general(a, b, TN_DIMS, preferred_element_type=F32, precision=precision)


def _iota(shape, dim):
    return lax.broadcasted_iota(jnp.int32, shape, dim)


def _layer_norm(y, g, b):
    mu = jnp.mean(y, axis=-1, keepdims=True)
    yc = y - mu
    var = jnp.mean(yc * yc, axis=-1, keepdims=True)
    return yc * lax.rsqrt(var + LN_EPS) * g + b


def _ffn_ln_kernel(x_ref, wg_ref, wu_ref, wd_ref, g_ref, b_ref, o_ref, xb_ref, acc_ref, *, alpha):
    f = pl.program_id(1)

    @pl.when(f == 0)
    def _():
        xb_ref[...] = x_ref[...].astype(BF16)
        acc_ref[...] = jnp.zeros_like(acc_ref)

    xb = xb_ref[...]
    gate = _dot(xb, wg_ref[...])
    up = _dot(xb, wu_ref[...])
    act = (jax.nn.silu(gate) * up).astype(BF16)
    acc_ref[...] += _dot(act, wd_ref[...])

    @pl.when(f == pl.num_programs(1) - 1)
    def _():
        y = alpha * x_ref[...] + 0.5 * acc_ref[...]
        o_ref[...] = _layer_norm(y, g_ref[...], b_ref[...])


def _ffn_ln(x, wg, wu, wd, g, b, alpha):
    m, d = x.shape
    dff = wg.shape[1]
    tm = min(512, m)
    tf = dff // 2 if (dff // 2) % LANES == 0 else dff
    grid = (m // tm, dff // tf)
    return pl.pallas_call(
        functools.partial(_ffn_ln_kernel, alpha=alpha),
        grid=grid,
        in_specs=[
            pl.BlockSpec((tm, d), lambda i, f: (i, 0)),
            pl.BlockSpec((d, tf), lambda i, f: (0, f)),
            pl.BlockSpec((d, tf), lambda i, f: (0, f)),
            pl.BlockSpec((tf, d), lambda i, f: (f, 0)),
            pl.BlockSpec((1, d), lambda i, f: (0, 0)),
            pl.BlockSpec((1, d), lambda i, f: (0, 0)),
        ],
        out_specs=pl.BlockSpec((tm, d), lambda i, f: (i, 0)),
        out_shape=jax.ShapeDtypeStruct((m, d), F32),
        scratch_shapes=[pltpu.VMEM((tm, d), BF16), pltpu.VMEM((tm, d), F32)],
        compiler_params=_params(("parallel", "arbitrary")),
        name="ffn_ln",
    )(x, wg, wu, wd, g, b)


OFF_Z, OFF_XBC, OFF_Q, OFF_MISC, W_IN_COLS = 0, 512, 1280, 1792, 1920


def _in_proj_kernel(h_ref, w_ref, wkv_ref, bias_ref, cos_ref, sin_ref, cost_ref, sint_ref,
                    z_ref, xbc_ref, misc_ref, qh_ref, kvt_ref, kvtb_ref, *, q_scale):
    hb = h_ref[...].astype(BF16)
    z_ref[...] = _dot(hb, w_ref[:, OFF_Z:OFF_XBC])
    xbc_ref[...] = _dot(hb, w_ref[:, OFF_XBC:OFF_Q])
    misc_ref[...] = _dot(hb, w_ref[:, OFF_MISC:W_IN_COLS]) + bias_ref[...]
    cos = cos_ref[...]
    sin = sin_ref[...]
    first_half = (_iota(cos.shape, 1) & (HD - 1)) < (HD // 2)
    for c in range(D_NSA // LANES):
        lo = OFF_Q + c * LANES
        x = _dot(hb, w_ref[:, lo:lo + LANES])
        rot = jnp.where(first_half, pltpu.roll(x, LANES - HD // 2, 1), pltpu.roll(x, HD // 2, 1))
        q = ((x * cos + rot * sin) * q_scale).astype(BF16)
        for j in range(LANES // HD):
            qh_ref[0, c * (LANES // HD) + j] = q[:, j * HD:(j + 1) * HD]
    kvt = _dot_nt(wkv_ref[...], hb)
    cost = cost_ref[...]
    sint = sint_ref[...]
    half = HD // 2
    for i in range(N_KV_PROJ):
        blk = kvt[i * KV_DIM:(i + 1) * KV_DIM, :]
        if i % 2 == 0:
            parts = []
            for g in range(N_KV):
                x1 = blk[g * HD:g * HD + half, :]
                x2 = blk[g * HD + half:(g + 1) * HD, :]
                parts += [x1 * cost - x2 * sint, x2 * cost + x1 * sint]
            blk = jnp.concatenate(parts, axis=0)
        kvt_ref[0, i * KV_DIM:(i + 1) * KV_DIM, :] = blk
        kvtb_ref[0, i * KV_DIM:(i + 1) * KV_DIM, :] = blk.astype(BF16)


def _in_proj(h, w, wkv, bias, tables, n_seq, t_seq):
    cos, sin, cost, sint = tables
    m, d = h.shape
    tm = min(512, t_seq)
    n_pos = t_seq // tm
    row = lambda i: (i, 0)
    fixed = lambda i: (0, 0)
    out_shapes = (
        jax.ShapeDtypeStruct((m, D_SSM), F32),
        jax.ShapeDtypeStruct((m, CONV_DIM), F32),
        jax.ShapeDtypeStruct((m, LANES), F32),
        jax.ShapeDtypeStruct((n_seq, N_HEADS, t_seq, HD), BF16),
        jax.ShapeDtypeStruct((n_seq, N_KV_PROJ * KV_DIM, t_seq), F32),
        jax.ShapeDtypeStruct((n_seq, N_KV_PROJ * KV_DIM, t_seq), BF16),
    )
    seq_t = lambda i: (i // n_pos, 0, i % n_pos)
    return pl.pallas_call(
        functools.partial(_in_proj_kernel, q_scale=HD ** -0.5),
        grid=(m // tm,),
        in_specs=[
            pl.BlockSpec((tm, d), row),
            pl.BlockSpec(w.shape, fixed),
            pl.BlockSpec(wkv.shape, fixed),
            pl.BlockSpec((1, LANES), fixed),
            pl.BlockSpec((tm, LANES), lambda i: (i % n_pos, 0)),
            pl.BlockSpec((tm, LANES), lambda i: (i % n_pos, 0)),
            pl.BlockSpec((HD // 2, tm), lambda i: (0, i % n_pos)),
            pl.BlockSpec((HD // 2, tm), lambda i: (0, i % n_pos)),
        ],
        out_specs=(
            pl.BlockSpec((tm, D_SSM), row),
            pl.BlockSpec((tm, CONV_DIM), row),
            pl.BlockSpec((tm, LANES), row),
            pl.BlockSpec((1, N_HEADS, tm, HD), lambda i: (i // n_pos, 0, i % n_pos, 0)),
            pl.BlockSpec((1, N_KV_PROJ * KV_DIM, tm), seq_t),
            pl.BlockSpec((1, N_KV_PROJ * KV_DIM, tm), seq_t),
        ),
        out_shape=out_shapes,
        compiler_params=_params(("parallel",)),
        name="in_proj",
    )(h, w, wkv, bias, cos, sin, cost, sint)


def _ssd_kernel(z_ref, xbc_ref, misc_ref, misct_ref, convw_ref, convb_ref,
                dtb_row_ref, alog_row_ref, dtb_col_ref, alog_col_ref, dskip_ref, normw_ref,
                y_ref, hout_ref, xbuf, hstate, ybuf, *, chunk):
    c = pl.program_id(1)
    L = chunk
    H = SUBLANES

    @pl.when(c == 0)
    def _():
        xbuf[0:H, :] = jnp.zeros((H, CONV_DIM), F32)
        hstate[...] = jnp.zeros_like(hstate)

    x = xbc_ref[0]
    xbuf[H:H + L, :] = x
    conv = convb_ref[...] + convw_ref[CONV_TAPS - 1:CONV_TAPS, :] * x
    for k in range(1, CONV_TAPS):
        conv = conv + convw_ref[CONV_TAPS - 1 - k:CONV_TAPS - k, :] * xbuf[H - k:H - k + L, :]
    xbuf[0:H, :] = x[L - H:L, :]
    xact = jax.nn.silu(conv)
    bm = xact[:, D_SSM:D_SSM + N_SSM_GROUPS * HD]
    cm = xact[:, D_SSM + N_SSM_GROUPS * HD:]

    dt = jax.nn.softplus(misc_ref[0] + dtb_row_ref[...])
    dtt = jax.nn.softplus(misct_ref[0, 0:SUBLANES, :] + dtb_col_ref[...])
    a_row = -jnp.exp(alog_row_ref[...])
    a_col = -jnp.exp(alog_col_ref[...])
    lane_ok = _iota(dt.shape, 1) < N_SSM_HEADS
    da = jnp.where(lane_ok, dt * a_row, 0.0)
    dat = dtt * a_col
    ri = _iota((L, L), 0)
    ci = _iota((L, L), 1)
    causal = ri >= ci
    tri = jnp.where(causal, 1.0, 0.0).astype(F32)
    cum = _dot(tri, da, HI)
    cumt = _dot_nt(dat, tri, HI)
    cum_last = cum[L - 1:L, :]

    rep = N_SSM_HEADS // N_SSM_GROUPS
    for g in range(N_SSM_GROUPS):
        cm_g = cm[:, g * HD:(g + 1) * HD]
        bm_g = bm[:, g * HD:(g + 1) * HD]
        cb = _dot_nt(cm_g, bm_g, HI)
        for r in range(rep):
            h = g * rep + r
            ch = cum[:, h:h + 1]
            seg = ch - cumt[h:h + 1, :]
            decay = jnp.where(causal, jnp.exp(jnp.where(causal, seg, 0.0)), 0.0)
            xs_h = xact[:, h * HD:(h + 1) * HD]
            xdt = xs_h * dt[:, h:h + 1]
            hprev = hstate[h]
            y_h = _dot(cb * decay, xdt, HI) + _dot_nt(cm_g, hprev, HI) * jnp.exp(ch)
            cl = cum_last[:, h:h + 1]
            tail = jnp.exp(cl - ch)
            hstate[h] = hprev * jnp.exp(cl) + _dot_tn(xdt * tail, bm_g, HI)
            ybuf[:, h * HD:(h + 1) * HD] = y_h

    xs = xact[:, :D_SSM]
    y = (ybuf[...] + dskip_ref[...] * xs) * jax.nn.silu(z_ref[0])
    y = y * lax.rsqrt(jnp.mean(y * y, axis=-1, keepdims=True) + RMS_EPS) * normw_ref[...]
    y_ref[0] = y

    @pl.when(c == pl.num_programs(1) - 1)
    def _():
        hout_ref[0] = hstate[...]


def _ssd(z, xbc, misc, misct, lw, chunk):
    bsz, t, _ = z.shape
    nc = t // chunk
    seq = lambda b, c: (b, c, 0)
    per_b4 = lambda b, c: (b, 0, 0, 0)
    fixed = lambda b, c: (0, 0)
    return pl.pallas_call(
        functools.partial(_ssd_kernel, chunk=chunk),
        grid=(bsz, nc),
        in_specs=[
            pl.BlockSpec((1, chunk, D_SSM), seq),
            pl.BlockSpec((1, chunk, CONV_DIM), seq),
            pl.BlockSpec((1, chunk, LANES), seq),
            pl.BlockSpec((1, LANES, chunk), lambda b, c: (b, 0, c)),
            pl.BlockSpec((CONV_TAPS, CONV_DIM), fixed),
            pl.BlockSpec((1, CONV_DIM), fixed),
            pl.BlockSpec((1, LANES), fixed),
            pl.BlockSpec((1, LANES), fixed),
            pl.BlockSpec((SUBLANES, 1), fixed),
            pl.BlockSpec((SUBLANES, 1), fixed),
            pl.BlockSpec((1, D_SSM), fixed),
            pl.BlockSpec((1, D_SSM), fixed),
        ],
        out_specs=(
            pl.BlockSpec((1, chunk, D_SSM), seq),
            pl.BlockSpec((1, N_SSM_HEADS, HD, HD), per_b4),
        ),
        out_shape=(
            jax.ShapeDtypeStruct((bsz, t, D_SSM), F32),
            jax.ShapeDtypeStruct((bsz, N_SSM_HEADS, HD, HD), F32),
        ),
        scratch_shapes=[
            pltpu.VMEM((SUBLANES + chunk, CONV_DIM), F32),
            pltpu.VMEM((N_SSM_HEADS, HD, HD), F32),
            pltpu.VMEM((chunk, D_SSM), F32),
        ],
        compiler_params=_params(("parallel", "arbitrary")),
        name="ssd",
    )(z, xbc, misc, misct, lw["conv_w"], lw["conv_b_row"], lw["dtb_row"], lw["alog_row"],
      lw["dtb_col"], lw["alog_col"], lw["dskip_row"], lw["normw_row"])


def _ssd_sample_kernel(xh_ref, z_ref, dtraw_ref, state_ref, convw_ref, convb_ref, dtb_ref, alog_ref,
                       dskip_ref, normw_ref, y_ref, hout_ref, xact, ypre, *, s_len):
    h = pl.program_id(0)

    @pl.when(h == 0)
    def _():
        for t in range(s_len):
            conv = convb_ref[...]
            for k in range(CONV_TAPS):
                conv = conv + convw_ref[k] * xh_ref[t + k]
            xact[t] = jax.nn.silu(conv)

    a = -jnp.exp(alog_ref[pl.ds(h, 1), :])
    dts = [jax.nn.softplus(dtraw_ref[t, pl.ds(h, 1), :] + dtb_ref[pl.ds(h, 1), :]) for t in range(s_len)]
    das = [jnp.exp(dt * a) for dt in dts]
    g = h // (N_SSM_HEADS // N_SSM_GROUPS)
    b_lo = pl.multiple_of(D_SSM + g * HD, HD)
    c_lo = pl.multiple_of(D_SSM + N_SSM_GROUPS * HD + g * HD, HD)
    bs = [xact[t, pl.ds(b_lo, HD), :] for t in range(s_len)]
    cs = [xact[t, pl.ds(c_lo, HD), :] for t in range(s_len)]

    def body(p, carry):
        hs = state_ref[0, p]
        row = h * HD + p
        for t in range(s_len):
            x = xact[t, pl.ds(row, 1), :]
            hs = hs * das[t] + (x * dts[t]) * bs[t]
            y = jnp.sum(cs[t] * hs, axis=0, keepdims=True)
            ypre[t, pl.ds(row, 1), :] = y + dskip_ref[pl.ds(row, 1), :] * x
        hout_ref[0, p] = hs
        return carry

    lax.fori_loop(0, HD, body, 0)

    @pl.when(h == pl.num_programs(0) - 1)
    def _():
        for t in range(s_len):
            y = ypre[t] * jax.nn.silu(z_ref[t])
            y = y * lax.rsqrt(jnp.mean(y * y, axis=0, keepdims=True) + RMS_EPS) * normw_ref[...]
            y_ref[t] = y


def _ssd_sample(xh, zt, dtraw, state, lw, dbz, s_len):
    lane_b = lambda v: jnp.broadcast_to(v[..., None], v.shape + (dbz,))
    full = lambda a: pl.BlockSpec(a.shape, lambda h: (0,) * a.ndim)
    args = (xh, zt, dtraw, state, lane_b(lw["conv_w"]), lane_b(lw["conv_b_row"][0]), lane_b(lw["dtb_col"][:, 0]),
            lane_b(lw["alog_col"][:, 0]), lane_b(lw["dskip_row"][0]), lane_b(lw["normw_row"][0]))
    state_spec = pl.BlockSpec((1, HD, HD, dbz), lambda h: (h, 0, 0, 0))
    in_specs = [full(a) for a in args]
    in_specs[3] = state_spec
    return pl.pallas_call(
        functools.partial(_ssd_sample_kernel, s_len=s_len),
        grid=(N_SSM_HEADS,),
        in_specs=in_specs,
        out_specs=(pl.BlockSpec((s_len, D_SSM, dbz), lambda h: (0, 0, 0)), state_spec),
        out_shape=(jax.ShapeDtypeStruct((s_len, D_SSM, dbz), F32),
                   jax.ShapeDtypeStruct(state.shape, F32)),
        scratch_shapes=[pltpu.VMEM((s_len, CONV_DIM, dbz), F32), pltpu.VMEM((s_len, D_SSM, dbz), F32)],
        compiler_params=_params(("arbitrary",)),
        name="ssd_sample",
    )(*args)


def _page_group(n_pages):
    return math.gcd(n_pages, 8)


def _compress_pages(get_pages, n_pages, tail_rows, pm_ref, w1_ref, w2_ref, pe_ref, x_buf, sec_buf):
    rows_pp = PAGE // CMP_STRIDE
    n = n_pages * rows_pp
    half = N_KV * CMP_HIDDEN
    group = _page_group(n_pages)

    def scatter(i, carry):
        perm = _dot_nt(pm_ref[...], get_pages(i))
        for j in range(group):
            r0 = pl.multiple_of((i * group + j) * rows_pp, rows_pp)
            for s in range(CMP_STRIDE):
                x_buf[pl.ds(r0, rows_pp), s * KV_DIM:(s + 1) * KV_DIM] = (
                    perm[s * rows_pp:(s + 1) * rows_pp, j * KV_DIM:(j + 1) * KV_DIM])
        return carry

    lax.fori_loop(0, n_pages // group, scatter, 0)
    x_buf[n:n + SUBLANES, :] = tail_rows
    out = _dot(x_buf[...].astype(BF16), w1_ref[...])
    sec_buf[...] = out[:, half:]
    pe_out = _dot(pe_ref[...], w1_ref[...])
    pe_term = pe_out[0:1, :half] + pe_out[SUBLANES:SUBLANES + 1, half:]
    pre = out[0:n, :half] + sec_buf[1:n + 1, :] + pe_term
    return _dot(jax.nn.gelu(pre).astype(BF16), w2_ref[...])


def _compress_kernel(x_ref, pm_ref, w1_ref, w2_ref, pe_ref, o_ref, x_buf, sec_buf, *, n_pages):
    group = _page_group(n_pages)

    def get_pages(i):
        wide = x_ref[0, 0, :, pl.ds(pl.multiple_of(i * group * PAGE, group * PAGE), group * PAGE)]
        return jnp.concatenate([wide[:, j * PAGE:(j + 1) * PAGE] for j in range(group)], axis=0)

    tail_rows = jnp.zeros((SUBLANES, CMP_STRIDE * KV_DIM), F32)
    o_ref[0] = _compress_pages(get_pages, n_pages, tail_rows, pm_ref, w1_ref, w2_ref, pe_ref, x_buf, sec_buf)


def _compress(kvtb4, proj, pm, w1big, w2big, pe2):
    bsz, _, _, t = kvtb4.shape
    n_pages = t // PAGE
    n = t // CMP_STRIDE
    fixed = lambda b: (0, 0)
    return pl.pallas_call(
        functools.partial(_compress_kernel, n_pages=n_pages),
        grid=(bsz,),
        in_specs=[
            pl.BlockSpec((1, 1, KV_DIM, t), lambda b: (b, proj, 0, 0)),
            pl.BlockSpec(pm.shape, fixed),
            pl.BlockSpec(w1big.shape, fixed),
            pl.BlockSpec(w2big.shape, fixed),
            pl.BlockSpec(pe2.shape, fixed),
        ],
        out_specs=pl.BlockSpec((1, n, KV_DIM), lambda b: (b, 0, 0)),
        out_shape=jax.ShapeDtypeStruct((bsz, n, KV_DIM), F32),
        scratch_shapes=[pltpu.VMEM((n + SUBLANES, CMP_STRIDE * KV_DIM), F32),
                        pltpu.VMEM((n + SUBLANES, N_KV * CMP_HIDDEN), F32)],
        compiler_params=_params(("parallel",)),
        name="compress",
    )(kvtb4, pm, w1big, w2big, pe2)


def _page_copy(cache_ref, buf, sem, page, slot, p):
    return pltpu.make_async_copy(cache_ref.at[page], buf.at[slot, pl.ds(p * PAGE, PAGE)], sem.at[slot])


def _gather_pages(pt_ref, streams, n_pages):
    b = pl.program_id(0)
    slot = b % 2

    def issue(seq, into):
        def start(p, carry):
            page = pt_ref[seq, p]
            for cache_ref, buf, sem in streams:
                _page_copy(cache_ref, buf, sem, page, into, p).start()
            return carry
        lax.fori_loop(0, n_pages, start, 0)

    @pl.when(b == 0)
    def _():
        issue(b, slot)

    @pl.when(b + 1 < pl.num_programs(0))
    def _():
        issue(b + 1, 1 - slot)

    def wait(p, carry):
        for cache_ref, buf, sem in streams:
            _page_copy(cache_ref, buf, sem, 0, slot, p).wait()
        return carry

    lax.fori_loop(0, n_pages, wait, 0)
    return slot


def _compress_paged_kernel(pt_ref, cache_ref, tail_ref, pm_ref, w1_ref, w2_ref, pe_ref, o_ref,
                           buf, x_buf, sec_buf, sem, *, n_pages):
    slot = _gather_pages(pt_ref, [(cache_ref, buf, sem)], n_pages)
    rows = _page_group(n_pages) * PAGE
    get_pages = lambda i: buf[slot, pl.ds(pl.multiple_of(i * rows, rows), rows), :].astype(BF16)
    o_ref[0] = _compress_pages(get_pages, n_pages, tail_ref[0], pm_ref, w1_ref, w2_ref, pe_ref, x_buf, sec_buf)


def _compress_paged(page_table, cache, tail, pm, w1big, w2big, pe2):
    dbz, n_pages = page_table.shape
    n = n_pages * (PAGE // CMP_STRIDE)
    fixed = lambda b, pt: (0, 0)
    grid_spec = pltpu.PrefetchScalarGridSpec(
        num_scalar_prefetch=1,
        grid=(dbz,),
        in_specs=[
            pl.BlockSpec(memory_space=pl.ANY),
            pl.BlockSpec((1,) + tail.shape[1:], lambda b, pt: (b, 0, 0)),
            pl.BlockSpec(pm.shape, fixed),
            pl.BlockSpec(w1big.shape, fixed),
            pl.BlockSpec(w2big.shape, fixed),
            pl.BlockSpec(pe2.shape, fixed),
        ],
        out_specs=pl.BlockSpec((1, n, KV_DIM), lambda b, pt: (b, 0, 0)),
        scratch_shapes=[
            pltpu.VMEM((2, n_pages * PAGE, PAGE), F32),
            pltpu.VMEM((n + SUBLANES, CMP_STRIDE * KV_DIM), F32),
            pltpu.VMEM((n + SUBLANES, N_KV * CMP_HIDDEN), F32),
            pltpu.SemaphoreType.DMA((2,)),
        ],
    )
    return pl.pallas_call(
        functools.partial(_compress_paged_kernel, n_pages=n_pages),
        grid_spec=grid_spec,
        out_shape=jax.ShapeDtypeStruct((dbz, n, KV_DIM), F32),
        compiler_params=_params(("arbitrary",)),
        name="compress_paged",
    )(page_table, cache, tail, pm, w1big, w2big, pe2)


def _select_blocks(imp_t, tpos, n_blocks, score_buf):
    j = _iota(imp_t.shape, 0)
    valid = (j * SEL_BLOCK <= tpos) & (j < n_blocks)
    cur = tpos // SEL_BLOCK
    forced = (j == 0) | (j == cur) | (j == cur - 1)
    score = jnp.where(valid, jnp.where(forced, FORCED_SCORE, imp_t), INVALID_SCORE)
    score_buf[...] = score

    def body(i, cnt):
        row = score_buf[pl.ds(i, 1), :]
        above = jnp.where(row > score, 1.0, 0.0)
        tie = jnp.where(row == score, jnp.where(j > i, 1.0, 0.0), 0.0)
        return cnt + above + tie

    cnt = lax.fori_loop(0, n_blocks, body, jnp.zeros(imp_t.shape, F32), unroll=4)
    return jnp.where(valid & (cnt < float(min(SEL_TOP_N, n_blocks))), 1.0, 0.0)


def _nsa_prompt_kernel(q_ref, kc_ref, vc_ref, ks_ref, vs_ref, kw_ref, vw_ref, ovl_ref, e_ref, misc_ref,
                       o_ref, score_buf, *, tq, tk, n_blocks):
    g = pl.program_id(1)
    t0 = pl.program_id(2) * tq
    rows = REP * tq
    q = q_ref[0].reshape(rows, HD)

    kc = kc_ref[0, 0]
    ncp = kc.shape[0]
    qpos_c = t0 + _iota((tq, ncp), 0)
    maskc = ((_iota((tq, ncp), 1) * CMP_STRIDE + (CMP_BLOCK - 1)) <= qpos_c)[None]
    s = _dot_nt(q, kc).reshape(REP, tq, ncp)
    sm = jnp.where(maskc, s, NEG)
    e = jnp.exp(sm - jnp.max(sm, axis=-1, keepdims=True))
    p = e / jnp.sum(e, axis=-1, keepdims=True) * jnp.where(maskc, 1.0, 0.0)
    o_c = _dot(p.reshape(rows, ncp).astype(BF16), vc_ref[0, 0])
    psum = jnp.sum(p, axis=0)

    imp_t = _dot_nt(ovl_ref[...], psum, HI)
    tpos = t0 + _iota(imp_t.shape, 1)
    selt = _select_blocks(imp_t, tpos, n_blocks, score_buf).astype(BF16)

    nsp = selt.shape[0]
    eye = jnp.where(_iota((nsp, nsp), 0) == _iota((nsp, nsp), 1), 1.0, 0.0).astype(BF16)
    bias = ((_dot_tn(selt, eye) - 1.0) * (-NEG)).astype(BF16)
    q_aug = jnp.concatenate([q, jnp.concatenate([bias] * REP, axis=0)], axis=1)
    ones_k = jnp.ones((HD, tk), BF16)
    qpos_k = t0 + _iota((tq, tk), 0)
    lane_k = _iota((tq, tk), 1)
    n_full = t0 // tk

    def step(kt, carry, diagonal):
        m, acc = carry
        k0 = pl.multiple_of(kt * tk, tk)
        k_aug = jnp.concatenate([ks_ref[0, 0, :, pl.ds(k0, tk)], e_ref[kt]], axis=0)
        v_aug = jnp.concatenate([vs_ref[0, 0, :, pl.ds(k0, tk)], ones_k], axis=0)
        s = _dot(q_aug, k_aug).reshape(REP, tq, tk)
        if diagonal:
            s = jnp.where((k0 + lane_k <= qpos_k)[None], s, NEG)
        m_new = jnp.maximum(m, jnp.max(s, axis=-1, keepdims=True))
        e = jnp.exp(s - m_new)
        pv = _dot_nt(e.reshape(rows, tk).astype(BF16), v_aug).reshape(REP, tq, 2 * HD)
        return m_new, jnp.exp(m - m_new) * acc + pv

    init = (jnp.full((REP, tq, 1), NEG, F32), jnp.zeros((REP, tq, 2 * HD), F32))
    carry = lax.fori_loop(0, n_full, functools.partial(step, diagonal=False), init)
    _, acc_s = step(n_full, carry, True)
    o_s = acc_s[..., :HD] / acc_s[..., HD:]

    wk = WINDOW + tq
    start = pl.multiple_of(jnp.maximum(t0 - WINDOW, 0), tq)
    kw = kw_ref[0, 0, :, pl.ds(start, wk)]
    vw_aug = jnp.concatenate([vw_ref[0, 0, :, pl.ds(start, wk)], jnp.ones((HD, wk), BF16)], axis=0)
    kpos = start + _iota((tq, wk), 1)
    qpos_w = t0 + _iota((tq, wk), 0)
    ok_w = (jnp.where(kpos <= qpos_w, jnp.where(kpos > qpos_w - WINDOW, 1.0, 0.0), 0.0) > 0.5)[None]
    sm = jnp.where(ok_w, _dot(q, kw).reshape(REP, tq, wk), NEG)
    e = jnp.exp(sm - jnp.max(sm, axis=-1, keepdims=True))
    acc_w = _dot_nt(e.reshape(rows, wk).astype(BF16), vw_aug).reshape(REP, tq, 2 * HD)
    o_w = acc_w[..., :HD] / acc_w[..., HD:]

    gates = jax.nn.sigmoid(misc_ref[0])
    gate_lane = _iota(gates.shape, 1) - (N_SSM_HEADS + 3 * REP * g)
    o_c = o_c.reshape(REP, tq, HD)
    for r in range(REP):
        gh = [jnp.sum(jnp.where(gate_lane == 3 * r + br, gates, 0.0), axis=-1, keepdims=True)
              for br in range(3)]
        o = gh[0] * o_c[r] + gh[1] * o_s[r] + gh[2] * o_w[r]
        o_ref[0, :, r * HD:(r + 1) * HD] = o.astype(o_ref.dtype)


def _nsa_prompt(qh, kch, vch, kvtb, ovl_t, e3, misc, n_blocks, tq, tk):
    bsz, nh, t, _ = qh.shape
    ncp = kch.shape[2]
    nsp = ovl_t.shape[0]
    kv_spec = lambda proj: pl.BlockSpec((1, 1, HD, t), lambda b, g, i: (b, N_KV * proj + g, 0, 0))
    cmp_spec = pl.BlockSpec((1, 1, ncp, HD), lambda b, g, i: (b, g, 0, 0))
    return pl.pallas_call(
        functools.partial(_nsa_prompt_kernel, tq=tq, tk=tk, n_blocks=n_blocks),
        grid=(bsz, N_KV, t // tq),
        in_specs=[
            pl.BlockSpec((1, REP, tq, HD), lambda b, g, i: (b, g, i, 0)),
            cmp_spec, cmp_spec,
            kv_spec(2), kv_spec(3), kv_spec(4), kv_spec(5),
            pl.BlockSpec(ovl_t.shape, lambda b, g, i: (0, 0)),
            pl.BlockSpec(e3.shape, lambda b, g, i: (0, 0, 0)),
            pl.BlockSpec((1, tq, LANES), lambda b, g, i: (b, i, 0)),
        ],
        out_specs=pl.BlockSpec((1, tq, REP * HD), lambda b, g, i: (b, i, g)),
        out_shape=jax.ShapeDtypeStruct((bsz, t, nh * HD), BF16),
        scratch_shapes=[pltpu.VMEM((nsp, tq), F32)],
        compiler_params=_params(("parallel", "parallel", "arbitrary")),
        name="nsa_prompt",
    )(qh, kch, vch, kvtb, kvtb, kvtb, kvtb, ovl_t, e3, misc)


def _out_proj_kernel(y_ref, o_ref, h_ref, w_ref, g_ref, b_ref, out_ref, *, alpha):
    mixed = _dot(y_ref[...].astype(BF16), w_ref[0:D_SSM, :]) + _dot(o_ref[...], w_ref[D_SSM:, :])
    out_ref[...] = _layer_norm(alpha * h_ref[...] + mixed, g_ref[...], b_ref[...])


def _out_proj_ln(y, o, h, w, g, b, alpha):
    m, d = h.shape
    tm = min(512, m)
    row = lambda i: (i, 0)
    fixed = lambda i: (0, 0)
    return pl.pallas_call(
        functools.partial(_out_proj_kernel, alpha=alpha),
        grid=(m // tm,),
        in_specs=[
            pl.BlockSpec((tm, D_SSM), row),
            pl.BlockSpec((tm, D_NSA), row),
            pl.BlockSpec((tm, d), row),
            pl.BlockSpec(w.shape, fixed),
            pl.BlockSpec((1, d), fixed),
            pl.BlockSpec((1, d), fixed),
        ],
        out_specs=pl.BlockSpec((tm, d), row),
        out_shape=jax.ShapeDtypeStruct((m, d), F32),
        compiler_params=_params(("parallel",)),
        name="out_proj_ln",
    )(y, o, h, w, g, b)


def _row_group(shape):
    return _iota(shape, 0) // (shape[0] // N_KV)


def _cmp_attn_sample_kernel(q_ref, kc_ref, vc_ref, ssum_ref, oc_ref, psum_ref, *, past, s_len):
    q = q_ref[0]
    rows = q.shape[0]
    kc = kc_ref[0].astype(BF16)
    vc = vc_ref[0].astype(BF16)
    nc = kc.shape[0]
    t_row = (_iota((rows, nc), 0) // REP) % s_len
    cidx = _iota((rows, nc), 1)
    maskc = (cidx * CMP_STRIDE + (CMP_BLOCK - 1)) <= past + t_row
    s = _dot_nt(q, kc)
    sm = jnp.where(maskc, s, NEG)
    e = jnp.exp(sm - jnp.max(sm, axis=-1, keepdims=True))
    p = e / jnp.sum(e, axis=-1, keepdims=True) * jnp.where(maskc, 1.0, 0.0)
    o = _dot(p.astype(BF16), vc)
    own = (_iota(o.shape, 1) // HD) == _row_group(o.shape)
    oc_ref[0] = jnp.where(own, o, 0.0)
    psum_ref[0] = _dot(ssum_ref[...], p, HI)


def _cmp_attn_sample(qbd, kc, vc, ssum, past, s_len):
    dbz, rows, _ = qbd.shape
    nc = kc.shape[1]
    ng = ssum.shape[0]
    per_b = lambda b: (b, 0, 0)
    return pl.pallas_call(
        functools.partial(_cmp_attn_sample_kernel, past=past, s_len=s_len),
        grid=(dbz,),
        in_specs=[
            pl.BlockSpec((1, rows, KV_DIM), per_b),
            pl.BlockSpec((1, nc, KV_DIM), per_b),
            pl.BlockSpec((1, nc, KV_DIM), per_b),
            pl.BlockSpec(ssum.shape, lambda b: (0, 0)),
        ],
        out_specs=(pl.BlockSpec((1, rows, KV_DIM), per_b), pl.BlockSpec((1, ng, nc), per_b)),
        out_shape=(jax.ShapeDtypeStruct((dbz, rows, KV_DIM), F32),
                   jax.ShapeDtypeStruct((dbz, ng, nc), F32)),
        compiler_params=_params(("parallel",)),
        name="cmp_attn_sample",
    )(qbd, kc, vc, ssum)


def _select_sample_kernel(psum_ref, ovl_ref, selt_ref, score_buf, *, past, s_len, n_blocks):
    imp_t = _dot_nt(ovl_ref[...], psum_ref[...], HI)
    tpos = past + _iota(imp_t.shape, 1) % s_len
    selt_ref[...] = _select_blocks(imp_t, tpos, n_blocks, score_buf)


def _select_sample(psum_all, ovl_t, past, s_len, n_blocks):
    nsp = ovl_t.shape[0]
    cols = psum_all.shape[0]
    return pl.pallas_call(
        functools.partial(_select_sample_kernel, past=past, s_len=s_len, n_blocks=n_blocks),
        out_shape=jax.ShapeDtypeStruct((nsp, cols), F32),
        scratch_shapes=[pltpu.VMEM((nsp, cols), F32)],
        compiler_params=pltpu.CompilerParams(vmem_limit_bytes=VMEM_LIMIT),
        name="select_sample",
    )(psum_all, ovl_t)


def _online_update(state, sm, v_t):
    m, l, acc = state
    m_new = jnp.maximum(m, jnp.max(sm, axis=-1, keepdims=True))
    alpha = jnp.exp(m - m_new)
    e = jnp.exp(sm - m_new)
    l = alpha * l + jnp.sum(e, axis=-1, keepdims=True)
    return m_new, l, alpha * acc + _dot_nt(e.astype(BF16), v_t)


def _sel_win_sample_kernel(pt_ref, kcache_ref, vcache_ref, q_ref, sel_ref, e_ref, kt_ref, vt_ref,
                           kw_ref, vw_ref, kwn_ref, vwn_ref, oc_ref, gate_ref, o_ref, kbuf, vbuf, ksem, vsem,
                           *, n_pages, s_len, n_chunks, chunk):
    slot = _gather_pages(pt_ref, [(kcache_ref, kbuf, ksem), (vcache_ref, vbuf, vsem)], n_pages)

    q = q_ref[0]
    rows = q.shape[0]
    sel = sel_ref[0]
    sel_main = sel[:, :LANES]
    past = n_pages * PAGE
    t_col = (_iota((rows, 1), 0) // REP) % s_len
    pages_pc = chunk // PAGE

    def chunk_t(buf, c):
        return jnp.concatenate([buf[slot, (c * pages_pc + j) * PAGE:(c * pages_pc + j + 1) * PAGE, :]
                                for j in range(pages_pc)], axis=1).astype(BF16)

    state = (jnp.full((rows, 1), NEG, F32), jnp.zeros((rows, 1), F32), jnp.zeros((rows, KV_DIM), F32))
    for c in range(n_chunks):
        blk = _dot(sel_main, e_ref[c])
        kpos = c * chunk + _iota((rows, chunk), 1)
        ok = jnp.where(kpos <= past + t_col, blk, 0.0) > 0.5
        state = _online_update(state, jnp.where(ok, _dot(q, chunk_t(kbuf, c)), NEG), chunk_t(vbuf, c))
    lane_t = _iota((rows, kt_ref.shape[2]), 1)
    sel_new = sel[:, LANES:LANES + 1].astype(F32)
    ok = jnp.where(lane_t <= t_col, jnp.where(lane_t < s_len, sel_new, 0.0), 0.0) > 0.5
    _, l, acc = _online_update(state, jnp.where(ok, _dot(q, kt_ref[0].astype(BF16)), NEG),
                               vt_ref[0].astype(BF16))
    o_s = acc / l

    wb = kw_ref.shape[2]
    iw = _iota((rows, wb), 1)
    ok_c = iw > t_col + (wb - WINDOW)
    state = (jnp.full((rows, 1), NEG, F32), jnp.zeros((rows, 1), F32), jnp.zeros((rows, KV_DIM), F32))
    state = _online_update(state, jnp.where(ok_c, _dot(q, kw_ref[0].astype(BF16)), NEG), vw_ref[0].astype(BF16))
    ok_n = jnp.where(lane_t <= t_col, jnp.where(lane_t < s_len, 1.0, 0.0), 0.0) > 0.5
    _, l, acc = _online_update(state, jnp.where(ok_n, _dot(q, kwn_ref[0].astype(BF16)), NEG),
                               vwn_ref[0].astype(BF16))
    o_w = acc / l

    gates = jax.nn.sigmoid(gate_ref[0])
    o = gates[:, 0:1] * oc_ref[0] + gates[:, 1:2] * o_s + gates[:, 2:3] * o_w
    own = (_iota(o.shape, 1) // HD) == _row_group(o.shape)
    o = jnp.where(own, o, 0.0)
    o_ref[0] = o[:, :HD] + o[:, HD:]


def _sel_win_sample(page_table, kcache, vcache, qbd, selx, e4, ktail, vtail, kwin, vwin, kwnew, vwnew,
                    oc, graw, s_len):
    dbz, n_pages = page_table.shape
    rows = qbd.shape[1]
    n_chunks, _, chunk = e4.shape
    per_b = lambda b, pt: (b, 0, 0)
    blk = lambda a: pl.BlockSpec((1,) + a.shape[1:], per_b)
    grid_spec = pltpu.PrefetchScalarGridSpec(
        num_scalar_prefetch=1,
        grid=(dbz,),
        in_specs=[
            pl.BlockSpec(memory_space=pl.ANY),
            pl.BlockSpec(memory_space=pl.ANY),
            blk(qbd), blk(selx),
            pl.BlockSpec(e4.shape, lambda b, pt: (0, 0, 0)),
            blk(ktail), blk(vtail), blk(kwin), blk(vwin), blk(kwnew), blk(vwnew), blk(oc), blk(graw),
        ],
        out_specs=pl.BlockSpec((1, rows, HD), per_b),
        scratch_shapes=[
            pltpu.VMEM((2, n_pages * PAGE, PAGE), F32),
            pltpu.VMEM((2, n_pages * PAGE, PAGE), F32),
            pltpu.SemaphoreType.DMA((2,)),
            pltpu.SemaphoreType.DMA((2,)),
        ],
    )
    return pl.pallas_call(
        functools.partial(_sel_win_sample_kernel, n_pages=n_pages, s_len=s_len,
                          n_chunks=n_chunks, chunk=chunk),
        grid_spec=grid_spec,
        out_shape=jax.ShapeDtypeStruct((dbz, rows, HD), F32),
        compiler_params=_params(("arbitrary",)),
        name="sel_win_sample",
    )(page_table, kcache, vcache, qbd, selx, e4, ktail, vtail, kwin, vwin, kwnew, vwnew, oc, graw)


def _rope_tables(pos):
    half = HD // 2
    inv = ROPE_THETA ** (-jnp.arange(half, dtype=F32) / half)
    ang = pos.astype(F32)[:, None] * inv[None, :]
    cos = jnp.cos(ang)
    sin = jnp.sin(ang)
    reps = LANES // HD
    cos_l = jnp.tile(jnp.concatenate([cos, cos], axis=-1), (1, reps))
    sin_l = jnp.tile(jnp.concatenate([-sin, sin], axis=-1), (1, reps))
    return cos_l, sin_l, cos.T, sin.T


def _overlap_t(nc, ncp, ns, nsp):
    c_start = np.arange(ncp) * CMP_STRIDE
    s_start = np.arange(nsp) * SEL_BLOCK
    ovl = ((c_start[None, :] + CMP_BLOCK > s_start[:, None]) & (c_start[None, :] < s_start[:, None] + SEL_BLOCK))
    ovl = ovl & (np.arange(ncp)[None, :] < nc) & (np.arange(nsp)[:, None] < ns)
    return jnp.asarray(ovl.astype(np.float32))


def _expander(n_rows, n_chunks, chunk, dtype=BF16):
    key_block = (np.arange(n_chunks)[:, None] * chunk + np.arange(chunk)[None, :]) // SEL_BLOCK
    e = key_block[:, None, :] == np.arange(n_rows)[None, :, None]
    return jnp.asarray(e.astype(np.float32)).astype(dtype)


def _page_permutation():
    rows_pp = PAGE // CMP_STRIDE
    r = np.arange(PAGE)
    src = (r % rows_pp) * CMP_STRIDE + r // rows_pp
    return jnp.asarray((src[:, None] == np.arange(PAGE)[None, :]).astype(np.float32)).astype(BF16)


def _round_up(x, m):
    return -(-x // m) * m


def _compress_weights(w1, w2, pe):
    eye = jnp.eye(N_KV, dtype=F32)
    halves = []
    for half in range(2):
        wh = w1[half * CMP_STRIDE:(half + 1) * CMP_STRIDE]
        big = jnp.einsum("sdh,gk->sgdkh", wh, eye)
        halves.append(big.reshape(CMP_STRIDE * KV_DIM, N_KV * CMP_HIDDEN))
    w1big = jnp.concatenate(halves, axis=1).astype(BF16)
    w2big = jnp.einsum("hd,gk->ghkd", w2, eye).reshape(N_KV * CMP_HIDDEN, KV_DIM).astype(BF16)
    pe_rows = []
    for half in range(2):
        ph = pe[half * CMP_STRIDE:(half + 1) * CMP_STRIDE]
        row = jnp.broadcast_to(ph[:, None, :], (CMP_STRIDE, N_KV, HD)).reshape(1, CMP_STRIDE * KV_DIM)
        pe_rows.append(jnp.broadcast_to(row, (SUBLANES, CMP_STRIDE * KV_DIM)))
    pe2 = jnp.concatenate(pe_rows, axis=0).astype(BF16)
    return _page_permutation(), w1big, w2big, pe2


def _layer_weights(w, l):
    sizes = [D_SSM, CONV_DIM, N_SSM_HEADS, D_NSA, N_KV_PROJ * KV_DIM, N_HEADS * 3]
    offs = np.cumsum([0] + sizes)
    w_in = w["w_in"][l]
    seg = lambda i: w_in[:, offs[i]:offs[i + 1]]
    pad = LANES - sizes[2] - sizes[5]
    w_in_r = jnp.concatenate([seg(0), seg(1), seg(3), seg(2), seg(5),
                              jnp.zeros((w_in.shape[0], pad), F32)], axis=1).astype(BF16)
    bias_misc = jnp.concatenate([jnp.zeros((sizes[2],), F32), w["b_gate"][l], jnp.zeros((pad,), F32)])[None, :]
    lane_pad = lambda v: jnp.concatenate([v, jnp.zeros((LANES - v.shape[0],), F32)])[None, :]
    lw = {
        "w_in": w_in_r,
        "w_kv_t": seg(4).T.astype(BF16),
        "bias_misc": bias_misc,
        "conv_w": w["conv_w"][l],
        "conv_b_row": w["conv_b"][l][None, :],
        "dtb_row": lane_pad(w["dt_bias"][l]),
        "alog_row": lane_pad(w["a_log"][l]),
        "dtb_col": w["dt_bias"][l][:, None],
        "alog_col": w["a_log"][l][:, None],
        "dskip_row": jnp.repeat(w["d_skip"][l], HD)[None, :],
        "normw_row": w["ssm_norm_w"][l][None, :],
        "w_out": w["w_out"][l].astype(BF16),
    }
    lw["cmp_k"] = _compress_weights(w["cmp_k_w1"][l], w["cmp_k_w2"][l], w["cmp_k_pe"][l])
    lw["cmp_v"] = _compress_weights(w["cmp_v_w1"][l], w["cmp_v_w2"][l], w["cmp_v_pe"][l])
    for i in (1, 2, 3):
        lw[f"ln{i}"] = (w[f"ln{i}_g"][l][None, :], w[f"ln{i}_b"][l][None, :])
    for i in (1, 2):
        lw[f"ffn{i}"] = (w[f"ffn{i}_w_gate"][l].astype(BF16), w[f"ffn{i}_w_up"][l].astype(BF16),
                         w[f"ffn{i}_w_down"][l].astype(BF16))
    return lw


def _heads_major(x, bsz, t, n):
    return x.reshape(bsz, t, n, HD).transpose(0, 2, 1, 3)


def _mix_prompt(h, lw, bsz, t):
    m = bsz * t
    z, xbc, misc, qh, kvt, kvtb = _in_proj(h, lw["w_in"], lw["w_kv_t"], lw["bias_misc"],
                                           _rope_tables(jnp.arange(t)), bsz, t)
    kv5 = kvt.reshape(bsz, N_KV_PROJ, N_KV, HD, t)
    kv_rows = [kv5[:, i].transpose(0, 3, 1, 2) for i in range(N_KV_PROJ)]

    chunk = min(128, t)
    misc3 = misc.reshape(bsz, t, LANES)
    xbc3 = xbc.reshape(bsz, t, CONV_DIM)
    y_ssd, h_ssm = _ssd(z.reshape(bsz, t, D_SSM), xbc3, misc3, misc3.transpose(0, 2, 1), lw, chunk)
    conv_state = xbc3[:, t - (CONV_TAPS - 1):]

    n_str = t // CMP_STRIDE
    nc = n_str - 1
    ns = -(-t // SEL_BLOCK)
    nsp = _round_up(ns, SUBLANES)
    kvtb4 = kvtb.reshape(bsz, N_KV_PROJ, KV_DIM, t)
    kc = _compress(kvtb4, 0, *lw["cmp_k"])
    vc = _compress(kvtb4, 1, *lw["cmp_v"])
    kch = _heads_major(kc, bsz, n_str, N_KV).astype(BF16)
    vch = _heads_major(vc, bsz, n_str, N_KV).astype(BF16)
    tq = min(256, t)
    tk = min(512, t)
    e3 = _expander(nsp, t // tk, tk)
    o = _nsa_prompt(qh, kch, vch, kvtb.reshape(bsz, N_KV_PROJ * N_KV, HD, t), _overlap_t(nc, n_str, ns, nsp),
                    e3, misc3, ns, tq, tk)

    wb = min(WINDOW, t)
    state = tuple(kv_rows[:4]) + (kv_rows[4][:, t - wb:], kv_rows[5][:, t - wb:], h_ssm, conv_state)
    return y_ssd.reshape(m, D_SSM), o.reshape(m, D_NSA), state


def _mix_sample(h, lw, l, dbz, s_len, caches, state_ssm, state_conv, page_table):
    cache_k_cmp, cache_v_cmp, cache_k_slc, cache_v_slc, cache_k_win, cache_v_win = caches
    m = dbz * s_len
    n_pages = page_table.shape[1]
    past = n_pages * PAGE
    pos = past + jnp.arange(s_len)
    z, xbc, misc, qh, kvt, _ = _in_proj(h, lw["w_in"], lw["w_kv_t"], lw["bias_misc"],
                                        _rope_tables(jnp.tile(pos, dbz)), 1, m)
    kvs = kvt.reshape(N_KV_PROJ, KV_DIM, dbz, s_len)
    new_rows = [kvs[i].transpose(1, 2, 0) for i in range(N_KV_PROJ)]
    new_t = [kvs[i].transpose(1, 0, 2) for i in range(N_KV_PROJ)]
    k_c, v_c, k_s, v_s, k_w, v_w = [r.reshape(dbz, s_len, N_KV, HD) for r in new_rows]

    misc3 = misc.reshape(dbz, s_len, LANES)
    xbc3 = xbc.reshape(dbz, s_len, CONV_DIM)
    xh = jnp.concatenate([state_conv[l], xbc3], axis=1)
    y_t, h_new = _ssd_sample(xh.transpose(1, 2, 0), z.reshape(dbz, s_len, D_SSM).transpose(1, 2, 0),
                             misc3[:, :, :N_SSM_HEADS].transpose(1, 2, 0), state_ssm[l].transpose(1, 2, 3, 0),
                             lw, dbz, s_len)
    y_ssd = y_t.transpose(2, 0, 1)
    h_ssm = h_new.transpose(3, 0, 1, 2)
    conv_state = xh[:, -(CONV_TAPS - 1):]

    width = CMP_STRIDE * KV_DIM
    n_pool = cache_k_cmp.shape[1]
    pages_t = lambda cache: cache[l].transpose(0, 2, 3, 1).reshape(n_pool, KV_DIM, PAGE)

    def tail_rows(new):
        flat = new.reshape(dbz, 1, s_len * KV_DIM)
        return jnp.pad(flat, ((0, 0), (0, SUBLANES - 1), (0, width - s_len * KV_DIM)))

    kc = _compress_paged(page_table, pages_t(cache_k_cmp), tail_rows(new_rows[0]), *lw["cmp_k"])
    vc = _compress_paged(page_table, pages_t(cache_v_cmp), tail_rows(new_rows[1]), *lw["cmp_v"])
    nc = kc.shape[1]
    total = past + s_len
    ns = -(-total // SEL_BLOCK)
    nsp = _round_up(ns, SUBLANES)

    rows = N_KV * s_len * REP
    qg = qh.reshape(N_KV, REP, dbz, s_len, HD).transpose(2, 0, 3, 1, 4)
    qbd = jnp.einsum("bgtrd,gk->bgtrkd", qg, jnp.eye(N_KV, dtype=BF16)).reshape(dbz, rows, KV_DIM)
    ng = N_KV * s_len
    ssum = jnp.asarray((np.arange(ng)[:, None] == np.arange(rows)[None, :] // REP).astype(np.float32))
    oc, psum = _cmp_attn_sample(qbd, kc, vc, ssum, past, s_len)
    selt = _select_sample(psum.reshape(dbz * ng, nc), _overlap_t(nc, nc, ns, nsp), past, s_len, ns)
    sel = selt.T.reshape(dbz, ng, 1, nsp)
    sel = jnp.broadcast_to(sel, (dbz, ng, REP, nsp)).reshape(dbz, rows, nsp)
    n_cached = past // SEL_BLOCK
    selx = jnp.concatenate([sel[:, :, :n_cached], jnp.zeros((dbz, rows, LANES - n_cached), F32),
                            sel[:, :, n_cached:n_cached + 1], jnp.zeros((dbz, rows, LANES - 1), F32)],
                           axis=-1).astype(BF16)
    chunk_k = min(2048, past)
    e4 = _expander(LANES, past // chunk_k, chunk_k)
    pad_lanes = lambda a: jnp.pad(a, ((0, 0), (0, 0), (0, LANES - s_len)))
    wb = cache_k_win.shape[2]
    win_t = lambda cache: cache[l].transpose(0, 2, 3, 1).reshape(dbz, KV_DIM, wb)
    gate = misc3[:, :, N_SSM_HEADS:N_SSM_HEADS + 3 * N_HEADS].reshape(dbz, s_len, N_KV, REP, 3)
    graw = jnp.pad(gate.transpose(0, 2, 1, 3, 4).reshape(dbz, rows, 3), ((0, 0), (0, 0), (0, LANES - 3)))
    o = _sel_win_sample(page_table, pages_t(cache_k_slc), pages_t(cache_v_slc), qbd, selx, e4,
                        pad_lanes(new_t[2]), pad_lanes(new_t[3]), win_t(cache_k_win), win_t(cache_v_win),
                        pad_lanes(new_t[4]), pad_lanes(new_t[5]), oc, graw, s_len)
    o_nsa = o.reshape(dbz, N_KV, s_len, REP, HD).transpose(0, 2, 1, 3, 4).reshape(m, D_NSA).astype(BF16)

    kw_full = jnp.concatenate([cache_k_win[l], k_w], axis=1)
    vw_full = jnp.concatenate([cache_v_win[l], v_w], axis=1)
    state = (k_c, v_c, k_s, v_s, kw_full[:, -wb:], vw_full[:, -wb:], h_ssm, conv_state)
    return y_ssd.reshape(m, D_SSM), o_nsa, state


def _layer(x, lw, alpha, mix_fn):
    h1 = _ffn_ln(x, *lw["ffn1"], *lw["ln1"], alpha)
    y_ssd, o_nsa, state = mix_fn(h1)
    h2 = _out_proj_ln(y_ssd, o_nsa, h1, lw["w_out"], *lw["ln2"], alpha)
    return _ffn_ln(h2, *lw["ffn2"], *lw["ln3"], alpha), state


def kernel(x_prompt, x_sample, cache_k_cmp, cache_v_cmp, cache_k_slc, cache_v_slc, cache_k_win, cache_v_win, state_ssm, state_conv, page_table, w_in, b_gate, conv_w, conv_b, dt_bias, a_log, d_skip, ssm_norm_w, cmp_k_w1, cmp_k_w2, cmp_k_pe, cmp_v_w1, cmp_v_w2, cmp_v_pe, w_out, ln1_g, ln1_b, ln2_g, ln2_b, ln3_g, ln3_b, ffn1_w_gate, ffn1_w_up, ffn1_w_down, ffn2_w_gate, ffn2_w_up, ffn2_w_down):
    weights = dict(w_in=w_in, b_gate=b_gate, conv_w=conv_w, conv_b=conv_b, dt_bias=dt_bias, a_log=a_log,
                   d_skip=d_skip, ssm_norm_w=ssm_norm_w, cmp_k_w1=cmp_k_w1, cmp_k_w2=cmp_k_w2,
                   cmp_k_pe=cmp_k_pe, cmp_v_w1=cmp_v_w1, cmp_v_w2=cmp_v_w2, cmp_v_pe=cmp_v_pe, w_out=w_out,
                   ln1_g=ln1_g, ln1_b=ln1_b, ln2_g=ln2_g, ln2_b=ln2_b, ln3_g=ln3_g, ln3_b=ln3_b,
                   ffn1_w_gate=ffn1_w_gate, ffn1_w_up=ffn1_w_up, ffn1_w_down=ffn1_w_down,
                   ffn2_w_gate=ffn2_w_gate, ffn2_w_up=ffn2_w_up, ffn2_w_down=ffn2_w_down)
    depth = w_in.shape[0]
    bsz, t, d = x_prompt.shape
    dbz, s_len, _ = x_sample.shape
    alpha = (2.0 * depth) ** 0.25
    caches = (cache_k_cmp, cache_v_cmp, cache_k_slc, cache_v_slc, cache_k_win, cache_v_win)
    y_p = x_prompt.reshape(bsz * t, d)
    y_s = x_sample.reshape(dbz * s_len, d)
    p_states, s_states = [], []
    for l in range(depth):
        lw = _layer_weights(weights, l)
        y_p, st_p = _layer(y_p, lw, alpha, lambda h: _mix_prompt(h, lw, bsz, t))
        y_s, st_s = _layer(y_s, lw, alpha, lambda h: _mix_sample(h, lw, l, dbz, s_len, caches, state_ssm,
                                                                 state_conv, page_table))
        p_states.append(st_p)
        s_states.append(st_s)
    p_st = [jnp.stack(a) for a in zip(*p_states)]
    s_st = [jnp.stack(a) for a in zip(*s_states)]
    outs = [y_p.reshape(bsz, t, d), y_s.reshape(dbz, s_len, d)]
    for p, s in zip(p_st[:6], s_st[:6]):
        outs += [p, s]
    outs += [p_st[6], s_st[6], p_st[7], s_st[7]]
    return tuple(outs)
```

```python
import functools
import math

import numpy as np
import jax
import jax.numpy as jnp
from jax import lax
from jax.experimental import pallas as pl
from jax.experimental.pallas import tpu as pltpu

F32 = jnp.float32
BF16 = jnp.bfloat16
HI = lax.Precision.HIGHEST

HD = 64
N_SSM_HEADS = 8
N_SSM_GROUPS = 2
D_SSM = 512
CONV_DIM = 768
CONV_TAPS = 4
N_HEADS = 8
N_KV = 2
REP = N_HEADS // N_KV
D_NSA = 512
KV_DIM = N_KV * HD
N_KV_PROJ = 6
CMP_STRIDE = 16
CMP_BLOCK = 32
CMP_HIDDEN = 128
SEL_BLOCK = 64
SEL_TOP_N = 16
WINDOW = 512
PAGE = 128
ROPE_THETA = 10000.0
LN_EPS = 1e-5
RMS_EPS = 1e-5
NEG = -1e30
FORCED_SCORE = 1e30
INVALID_SCORE = -1.0

LANES = 128
SUBLANES = 8
VMEM_LIMIT = 56 * 1024 * 1024

NT_DIMS = (((1,), (1,)), ((), ()))
TN_DIMS = (((0,), (0,)), ((), ()))


def _params(sem):
    return pltpu.CompilerParams(dimension_semantics=sem, vmem_limit_bytes=VMEM_LIMIT)


def _dot(a, b, precision=None):
    return jnp.dot(a, b, preferred_element_type=F32, precision=precision)


def _dot_nt(a, b, precision=None):
    return lax.dot_general(a, b, NT_DIMS, preferred_element_type=F32, precision=precision)


def _dot_tn(a, b, precision=None):
    return lax.dot_general(a, b, TN_DIMS, preferred_element_type=F32, precision=precision)


def _iota(shape, dim):
    return lax.broadcasted_iota(jnp.int32, shape, dim)


def _layer_norm(y, g, b):
    mu = jnp.mean(y, axis=-1, keepdims=True)
    yc = y - mu
    var = jnp.mean(yc * yc, axis=-1, keepdims=True)
    return yc * lax.rsqrt(var + LN_EPS) * g + b


def _ffn_ln_kernel(x_ref, wg_ref, wu_ref, wd_ref, g_ref, b_ref, o_ref, xb_ref, acc_ref, *, alpha):
    f = pl.program_id(1)

    @pl.when(f == 0)
    def _():
        xb_ref[...] = x_ref[...].astype(BF16)
        acc_ref[...] = jnp.zeros_like(acc_ref)

    xb = xb_ref[...]
    gate = _dot(xb, wg_ref[...])
    up = _dot(xb, wu_ref[...])
    act = (jax.nn.silu(gate) * up).astype(BF16)
    acc_ref[...] += _dot(act, wd_ref[...])

    @pl.when(f == pl.num_programs(1) - 1)
    def _():
        y = alpha * x_ref[...] + 0.5 * acc_ref[...]
        o_ref[...] = _layer_norm(y, g_ref[...], b_ref[...])


def _ffn_ln(x, wg, wu, wd, g, b, alpha):
    m, d = x.shape
    dff = wg.shape[1]
    tm = min(512, m)
    tf = dff // 2 if (dff // 2) % LANES == 0 else dff
    grid = (m // tm, dff // tf)
    return pl.pallas_call(
        functools.partial(_ffn_ln_kernel, alpha=alpha),
        grid=grid,
        in_specs=[
            pl.BlockSpec((tm, d), lambda i, f: (i, 0)),
            pl.BlockSpec((d, tf), lambda i, f: (0, f)),
            pl.BlockSpec((d, tf), lambda i, f: (0, f)),
            pl.BlockSpec((tf, d), lambda i, f: (f, 0)),
            pl.BlockSpec((1, d), lambda i, f: (0, 0)),
            pl.BlockSpec((1, d), lambda i, f: (0, 0)),
        ],
        out_specs=pl.BlockSpec((tm, d), lambda i, f: (i, 0)),
        out_shape=jax.ShapeDtypeStruct((m, d), F32),
        scratch_shapes=[pltpu.VMEM((tm, d), BF16), pltpu.VMEM((tm, d), F32)],
        compiler_params=_params(("parallel", "arbitrary")),
        name="ffn_ln",
    )(x, wg, wu, wd, g, b)


OFF_Z, OFF_XBC, OFF_Q, OFF_MISC, W_IN_COLS = 0, 512, 1280, 1792, 1920


def _in_proj_kernel(h_ref, w_ref, wkv_ref, bias_ref, cos_ref, sin_ref, cost_ref, sint_ref,
                    z_ref, xbc_ref, misc_ref, qh_ref, kvt_ref, kvtb_ref, *, q_scale):
    hb = h_ref[...].astype(BF16)
    z_ref[...] = _dot(hb, w_ref[:, OFF_Z:OFF_XBC])
    xbc_ref[...] = _dot(hb, w_ref[:, OFF_XBC:OFF_Q])
    misc_ref[...] = _dot(hb, w_ref[:, OFF_MISC:W_IN_COLS]) + bias_ref[...]
    cos = cos_ref[...]
    sin = sin_ref[...]
    first_half = (_iota(cos.shape, 1) & (HD - 1)) < (HD // 2)
    for c in range(D_NSA // LANES):
        lo = OFF_Q + c * LANES
        x = _dot(hb, w_ref[:, lo:lo + LANES])
        rot = jnp.where(first_half, pltpu.roll(x, LANES - HD // 2, 1), pltpu.roll(x, HD // 2, 1))
        q = ((x * cos + rot * sin) * q_scale).astype(BF16)
        for j in range(LANES // HD):
            qh_ref[0, c * (LANES // HD) + j] = q[:, j * HD:(j + 1) * HD]
    kvt = _dot_nt(wkv_ref[...], hb)
    cost = cost_ref[...]
    sint = sint_ref[...]
    half = HD // 2
    for i in range(N_KV_PROJ):
        blk = kvt[i * KV_DIM:(i + 1) * KV_DIM, :]
        if i % 2 == 0:
            parts = []
            for g in range(N_KV):
                x1 = blk[g * HD:g * HD + half, :]
                x2 = blk[g * HD + half:(g + 1) * HD, :]
                parts += [x1 * cost - x2 * sint, x2 * cost + x1 * sint]
            blk = jnp.concatenate(parts, axis=0)
        kvt_ref[0, i * KV_DIM:(i + 1) * KV_DIM, :] = blk
        kvtb_ref[0, i * KV_DIM:(i + 1) * KV_DIM, :] = blk.astype(BF16)


def _in_proj(h, w, wkv, bias, tables, n_seq, t_seq):
    cos, sin, cost, sint = tables
    m, d = h.shape
    tm = min(512, t_seq)
    n_pos = t_seq // tm
    row = lambda i: (i, 0)
    fixed = lambda i: (0, 0)
    out_shapes = (
        jax.ShapeDtypeStruct((m, D_SSM), F32),
        jax.ShapeDtypeStruct((m, CONV_DIM), F32),
        jax.ShapeDtypeStruct((m, LANES), F32),
        jax.ShapeDtypeStruct((n_seq, N_HEADS, t_seq, HD), BF16),
        jax.ShapeDtypeStruct((n_seq, N_KV_PROJ * KV_DIM, t_seq), F32),
        jax.ShapeDtypeStruct((n_seq, N_KV_PROJ * KV_DIM, t_seq), BF16),
    )
    seq_t = lambda i: (i // n_pos, 0, i % n_pos)
    return pl.pallas_call(
        functools.partial(_in_proj_kernel, q_scale=HD ** -0.5),
        grid=(m // tm,),
        in_specs=[
            pl.BlockSpec((tm, d), row),
            pl.BlockSpec(w.shape, fixed),
            pl.BlockSpec(wkv.shape, fixed),
            pl.BlockSpec((1, LANES), fixed),
            pl.BlockSpec((tm, LANES), lambda i: (i % n_pos, 0)),
            pl.BlockSpec((tm, LANES), lambda i: (i % n_pos, 0)),
            pl.BlockSpec((HD // 2, tm), lambda i: (0, i % n_pos)),
            pl.BlockSpec((HD // 2, tm), lambda i: (0, i % n_pos)),
        ],
        out_specs=(
            pl.BlockSpec((tm, D_SSM), row),
            pl.BlockSpec((tm, CONV_DIM), row),
            pl.BlockSpec((tm, LANES), row),
            pl.BlockSpec((1, N_HEADS, tm, HD), lambda i: (i // n_pos, 0, i % n_pos, 0)),
            pl.BlockSpec((1, N_KV_PROJ * KV_DIM, tm), seq_t),
            pl.BlockSpec((1, N_KV_PROJ * KV_DIM, tm), seq_t),
        ),
        out_shape=out_shapes,
        compiler_params=_params(("parallel",)),
        name="in_proj",
    )(h, w, wkv, bias, cos, sin, cost, sint)


def _ssd_kernel(z_ref, xbc_ref, misc_ref, misct_ref, convw_ref, convb_ref,
                dtb_row_ref, alog_row_ref, dtb_col_ref, alog_col_ref, dskip_ref, normw_ref,
                y_ref, hout_ref, xbuf, hstate, ybuf, *, chunk):
    c = pl.program_id(1)
    L = chunk
    H = SUBLANES

    @pl.when(c == 0)
    def _():
        xbuf[0:H, :] = jnp.zeros((H, CONV_DIM), F32)
        hstate[...] = jnp.zeros_like(hstate)

    x = xbc_ref[0]
    xbuf[H:H + L, :] = x
    conv = convb_ref[...] + convw_ref[CONV_TAPS - 1:CONV_TAPS, :] * x
    for k in range(1, CONV_TAPS):
        conv = conv + convw_ref[CONV_TAPS - 1 - k:CONV_TAPS - k, :] * xbuf[H - k:H - k + L, :]
    xbuf[0:H, :] = x[L - H:L, :]
    xact = jax.nn.silu(conv)
    bm = xact[:, D_SSM:D_SSM + N_SSM_GROUPS * HD]
    cm = xact[:, D_SSM + N_SSM_GROUPS * HD:]

    dt = jax.nn.softplus(misc_ref[0] + dtb_row_ref[...])
    dtt = jax.nn.softplus(misct_ref[0, 0:SUBLANES, :] + dtb_col_ref[...])
    a_row = -jnp.exp(alog_row_ref[...])
    a_col = -jnp.exp(alog_col_ref[...])
    lane_ok = _iota(dt.shape, 1) < N_SSM_HEADS
    da = jnp.where(lane_ok, dt * a_row, 0.0)
    dat = dtt * a_col
    ri = _iota((L, L), 0)
    ci = _iota((L, L), 1)
    causal = ri >= ci
    tri = jnp.where(causal, 1.0, 0.0).astype(F32)
    cum = _dot(tri, da, HI)
    cumt = _dot_nt(dat, tri, HI)
    cum_last = cum[L - 1:L, :]

    rep = N_SSM_HEADS // N_SSM_GROUPS
    for g in range(N_SSM_GROUPS):
        cm_g = cm[:, g * HD:(g + 1) * HD].astype(BF16)
        bm_g = bm[:, g * HD:(g + 1) * HD].astype(BF16)
        cb = _dot_nt(cm_g, bm_g)
        for r in range(rep):
            h = g * rep + r
            ch = cum[:, h:h + 1]
            seg = ch - cumt[h:h + 1, :]
            decay = jnp.where(causal, jnp.exp(jnp.where(causal, seg, 0.0)), 0.0)
            xs_h = xact[:, h * HD:(h + 1) * HD]
            xdt = xs_h * dt[:, h:h + 1]
            hprev = hstate[h]
            y_h = (_dot((cb * decay).astype(BF16), xdt.astype(BF16))
                   + _dot_nt(cm_g, hprev.astype(BF16)) * jnp.exp(ch))
            cl = cum_last[:, h:h + 1]
            tail = jnp.exp(cl - ch)
            hstate[h] = hprev * jnp.exp(cl) + _dot_tn((xdt * tail).astype(BF16), bm_g)
            ybuf[:, h * HD:(h + 1) * HD] = y_h

    xs = xact[:, :D_SSM]
    y = (ybuf[...] + dskip_ref[...] * xs) * jax.nn.silu(z_ref[0])
    y = y * lax.rsqrt(jnp.mean(y * y, axis=-1, keepdims=True) + RMS_EPS) * normw_ref[...]
    y_ref[0] = y

    @pl.when(c == pl.num_programs(1) - 1)
    def _():
        hout_ref[0] = hstate[...]


def _ssd(z, xbc, misc, misct, lw, chunk):
    bsz, t, _ = z.shape
    nc = t // chunk
    seq = lambda b, c: (b, c, 0)
    per_b4 = lambda b, c: (b, 0, 0, 0)
    fixed = lambda b, c: (0, 0)
    return pl.pallas_call(
        functools.partial(_ssd_kernel, chunk=chunk),
        grid=(bsz, nc),
        in_specs=[
            pl.BlockSpec((1, chunk, D_SSM), seq),
            pl.BlockSpec((1, chunk, CONV_DIM), seq),
            pl.BlockSpec((1, chunk, LANES), seq),
            pl.BlockSpec((1, LANES, chunk), lambda b, c: (b, 0, c)),
            pl.BlockSpec((CONV_TAPS, CONV_DIM), fixed),
            pl.BlockSpec((1, CONV_DIM), fixed),
            pl.BlockSpec((1, LANES), fixed),
            pl.BlockSpec((1, LANES), fixed),
            pl.BlockSpec((SUBLANES, 1), fixed),
            pl.BlockSpec((SUBLANES, 1), fixed),
            pl.BlockSpec((1, D_SSM), fixed),
            pl.BlockSpec((1, D_SSM), fixed),
        ],
        out_specs=(
            pl.BlockSpec((1, chunk, D_SSM), seq),
            pl.BlockSpec((1, N_SSM_HEADS, HD, HD), per_b4),
        ),
        out_shape=(
            jax.ShapeDtypeStruct((bsz, t, D_SSM), F32),
            jax.ShapeDtypeStruct((bsz, N_SSM_HEADS, HD, HD), F32),
        ),
        scratch_shapes=[
            pltpu.VMEM((SUBLANES + chunk, CONV_DIM), F32),
            pltpu.VMEM((N_SSM_HEADS, HD, HD), F32),
            pltpu.VMEM((chunk, D_SSM), F32),
        ],
        compiler_params=_params(("parallel", "arbitrary")),
        name="ssd",
    )(z, xbc, misc, misct, lw["conv_w"], lw["conv_b_row"], lw["dtb_row"], lw["alog_row"],
      lw["dtb_col"], lw["alog_col"], lw["dskip_row"], lw["normw_row"])


def _ssd_sample_kernel(xh_ref, z_ref, dtraw_ref, state_ref, convw_ref, convb_ref, dtb_ref, alog_ref,
                       dskip_ref, normw_ref, y_ref, hout_ref, xact, ypre, *, s_len):
    h = pl.program_id(0)

    @pl.when(h == 0)
    def _():
        for t in range(s_len):
            conv = convb_ref[...]
            for k in range(CONV_TAPS):
                conv = conv + convw_ref[k] * xh_ref[t + k]
            xact[t] = jax.nn.silu(conv)

    a = -jnp.exp(alog_ref[pl.ds(h, 1), :])
    dts = [jax.nn.softplus(dtraw_ref[t, pl.ds(h, 1), :] + dtb_ref[pl.ds(h, 1), :]) for t in range(s_len)]
    das = [jnp.exp(dt * a) for dt in dts]
    g = h // (N_SSM_HEADS // N_SSM_GROUPS)
    b_lo = pl.multiple_of(D_SSM + g * HD, HD)
    c_lo = pl.multiple_of(D_SSM + N_SSM_GROUPS * HD + g * HD, HD)
    bs = [xact[t, pl.ds(b_lo, HD), :] for t in range(s_len)]
    cs = [xact[t, pl.ds(c_lo, HD), :] for t in range(s_len)]

    def body(p, carry):
        hs = state_ref[0, p]
        row = h * HD + p
        for t in range(s_len):
            x = xact[t, pl.ds(row, 1), :]
            hs = hs * das[t] + (x * dts[t]) * bs[t]
            y = jnp.sum(cs[t] * hs, axis=0, keepdims=True)
            ypre[t, pl.ds(row, 1), :] = y + dskip_ref[pl.ds(row, 1), :] * x
        hout_ref[0, p] = hs
        return carry

    lax.fori_loop(0, HD, body, 0)

    @pl.when(h == pl.num_programs(0) - 1)
    def _():
        for t in range(s_len):
            y = ypre[t] * jax.nn.silu(z_ref[t])
            y = y * lax.rsqrt(jnp.mean(y * y, axis=0, keepdims=True) + RMS_EPS) * normw_ref[...]
            y_ref[t] = y


def _ssd_sample(xh, zt, dtraw, state, lw, dbz, s_len):
    lane_b = lambda v: jnp.broadcast_to(v[..., None], v.shape + (dbz,))
    full = lambda a: pl.BlockSpec(a.shape, lambda h: (0,) * a.ndim)
    args = (xh, zt, dtraw, state, lane_b(lw["conv_w"]), lane_b(lw["conv_b_row"][0]), lane_b(lw["dtb_col"][:, 0]),
            lane_b(lw["alog_col"][:, 0]), lane_b(lw["dskip_row"][0]), lane_b(lw["normw_row"][0]))
    state_spec = pl.BlockSpec((1, HD, HD, dbz), lambda h: (h, 0, 0, 0))
    in_specs = [full(a) for a in args]
    in_specs[3] = state_spec
    return pl.pallas_call(
        functools.partial(_ssd_sample_kernel, s_len=s_len),
        grid=(N_SSM_HEADS,),
        in_specs=in_specs,
        out_specs=(pl.BlockSpec((s_len, D_SSM, dbz), lambda h: (0, 0, 0)), state_spec),
        out_shape=(jax.ShapeDtypeStruct((s_len, D_SSM, dbz), F32),
                   jax.ShapeDtypeStruct(state.shape, F32)),
        scratch_shapes=[pltpu.VMEM((s_len, CONV_DIM, dbz), F32), pltpu.VMEM((s_len, D_SSM, dbz), F32)],
        compiler_params=_params(("arbitrary",)),
        name="ssd_sample",
    )(*args)


def _page_group(n_pages):
    return math.gcd(n_pages, 8)


def _compress_pages(get_pages, n_pages, tail_rows, pm_ref, w1_ref, w2_ref, pe_ref, x_buf, sec_buf):
    rows_pp = PAGE // CMP_STRIDE
    n = n_pages * rows_pp
    half = N_KV * CMP_HIDDEN
    group = _page_group(n_pages)

    def scatter(i, carry):
        perm = _dot_nt(pm_ref[...], get_pages(i))
        for j in range(group):
            r0 = pl.multiple_of((i * group + j) * rows_pp, rows_pp)
            for s in range(CMP_STRIDE):
                x_buf[pl.ds(r0, rows_pp), s * KV_DIM:(s + 1) * KV_DIM] = (
                    perm[s * rows_pp:(s + 1) * rows_pp, j * KV_DIM:(j + 1) * KV_DIM])
        return carry

    lax.fori_loop(0, n_pages // group, scatter, 0)
    x_buf[n:n + SUBLANES, :] = tail_rows
    out = _dot(x_buf[...].astype(BF16), w1_ref[...])
    sec_buf[...] = out[:, half:]
    pe_out = _dot(pe_ref[...], w1_ref[...])
    pe_term = pe_out[0:1, :half] + pe_out[SUBLANES:SUBLANES + 1, half:]
    pre = out[0:n, :half] + sec_buf[1:n + 1, :] + pe_term
    return _dot(jax.nn.gelu(pre).astype(BF16), w2_ref[...])


def _compress_kernel(x_ref, pm_ref, w1_ref, w2_ref, pe_ref, o_ref, x_buf, sec_buf, *, n_pages):
    group = _page_group(n_pages)

    def get_pages(i):
        wide = x_ref[0, 0, :, pl.ds(pl.multiple_of(i * group * PAGE, group * PAGE), group * PAGE)]
        return jnp.concatenate([wide[:, j * PAGE:(j + 1) * PAGE] for j in range(group)], axis=0)

    tail_rows = jnp.zeros((SUBLANES, CMP_STRIDE * KV_DIM), F32)
    o_ref[0] = _compress_pages(get_pages, n_pages, tail_rows, pm_ref, w1_ref, w2_ref, pe_ref, x_buf, sec_buf)


def _compress(kvtb4, proj, pm, w1big, w2big, pe2):
    bsz, _, _, t = kvtb4.shape
    n_pages = t // PAGE
    n = t // CMP_STRIDE
    fixed = lambda b: (0, 0)
    return pl.pallas_call(
        functools.partial(_compress_kernel, n_pages=n_pages),
        grid=(bsz,),
        in_specs=[
            pl.BlockSpec((1, 1, KV_DIM, t), lambda b: (b, proj, 0, 0)),
            pl.BlockSpec(pm.shape, fixed),
            pl.BlockSpec(w1big.shape, fixed),
            pl.BlockSpec(w2big.shape, fixed),
            pl.BlockSpec(pe2.shape, fixed),
        ],
        out_specs=pl.BlockSpec((1, n, KV_DIM), lambda b: (b, 0, 0)),
        out_shape=jax.ShapeDtypeStruct((bsz, n, KV_DIM), F32),
        scratch_shapes=[pltpu.VMEM((n + SUBLANES, CMP_STRIDE * KV_DIM), F32),
                        pltpu.VMEM((n + SUBLANES, N_KV * CMP_HIDDEN), F32)],
        compiler_params=_params(("parallel",)),
        name="compress",
    )(kvtb4, pm, w1big, w2big, pe2)


def _page_copy(cache_ref, buf, sem, page, slot, p):
    return pltpu.make_async_copy(cache_ref.at[page], buf.at[slot, pl.ds(p * PAGE, PAGE)], sem.at[slot])


def _gather_pages(pt_ref, streams, n_pages):
    b = pl.program_id(0)
    slot = b % 2

    def issue(seq, into):
        def start(p, carry):
            page = pt_ref[seq, p]
            for cache_ref, buf, sem in streams:
                _page_copy(cache_ref, buf, sem, page, into, p).start()
            return carry
        lax.fori_loop(0, n_pages, start, 0)

    @pl.when(b == 0)
    def _():
        issue(b, slot)

    @pl.when(b + 1 < pl.num_programs(0))
    def _():
        issue(b + 1, 1 - slot)

    def wait(p, carry):
        for cache_ref, buf, sem in streams:
            _page_copy(cache_ref, buf, sem, 0, slot, p).wait()
        return carry

    lax.fori_loop(0, n_pages, wait, 0)
    return slot


def _compress_paged_kernel(pt_ref, cache_ref, tail_ref, pm_ref, w1_ref, w2_ref, pe_ref, o_ref,
                           buf, x_buf, sec_buf, sem, *, n_pages):
    slot = _gather_pages(pt_ref, [(cache_ref, buf, sem)], n_pages)
    rows = _page_group(n_pages) * PAGE
    get_pages = lambda i: buf[slot, pl.ds(pl.multiple_of(i * rows, rows), rows), :].astype(BF16)
    o_ref[0] = _compress_pages(get_pages, n_pages, tail_ref[0], pm_ref, w1_ref, w2_ref, pe_ref, x_buf, sec_buf)


def _compress_paged(page_table, cache, tail, pm, w1big, w2big, pe2):
    dbz, n_pages = page_table.shape
    n = n_pages * (PAGE // CMP_STRIDE)
    fixed = lambda b, pt: (0, 0)
    grid_spec = pltpu.PrefetchScalarGridSpec(
        num_scalar_prefetch=1,
        grid=(dbz,),
        in_specs=[
            pl.BlockSpec(memory_space=pl.ANY),
            pl.BlockSpec((1,) + tail.shape[1:], lambda b, pt: (b, 0, 0)),
            pl.BlockSpec(pm.shape, fixed),
            pl.BlockSpec(w1big.shape, fixed),
            pl.BlockSpec(w2big.shape, fixed),
            pl.BlockSpec(pe2.shape, fixed),
        ],
        out_specs=pl.BlockSpec((1, n, KV_DIM), lambda b, pt: (b, 0, 0)),
        scratch_shapes=[
            pltpu.VMEM((2, n_pages * PAGE, PAGE), F32),
            pltpu.VMEM((n + SUBLANES, CMP_STRIDE * KV_DIM), F32),
            pltpu.VMEM((n + SUBLANES, N_KV * CMP_HIDDEN), F32),
            pltpu.SemaphoreType.DMA((2,)),
        ],
    )
    return pl.pallas_call(
        functools.partial(_compress_paged_kernel, n_pages=n_pages),
        grid_spec=grid_spec,
        out_shape=jax.ShapeDtypeStruct((dbz, n, KV_DIM), F32),
        compiler_params=_params(("arbitrary",)),
        name="compress_paged",
    )(page_table, cache, tail, pm, w1big, w2big, pe2)


RANK_UNROLL = 4


def _select_blocks(imp_t, tpos, n_blocks, score_buf, n_live=None):
    j = _iota(imp_t.shape, 0)
    valid = (j * SEL_BLOCK <= tpos) & (j < n_blocks)
    cur = tpos // SEL_BLOCK
    forced = (j == 0) | (j == cur) | (j == cur - 1)
    score = jnp.where(valid, jnp.where(forced, FORCED_SCORE, imp_t), INVALID_SCORE)
    score_buf[...] = score

    def body(i, cnt):
        row = score_buf[pl.ds(i, 1), :]
        above = jnp.where(row > score, 1.0, 0.0)
        tie = jnp.where(row == score, jnp.where(j > i, 1.0, 0.0), 0.0)
        return cnt + above + tie

    zero = jnp.zeros(imp_t.shape, F32)
    if n_live is None:
        cnt = lax.fori_loop(0, n_blocks, body, zero, unroll=RANK_UNROLL)
    else:
        def group(gi, cnt):
            for u in range(RANK_UNROLL):
                cnt = body(gi * RANK_UNROLL + u, cnt)
            return cnt
        assert imp_t.shape[0] % RANK_UNROLL == 0
        cnt = lax.fori_loop(0, (n_live + RANK_UNROLL - 1) // RANK_UNROLL, group, zero)
    return jnp.where(valid & (cnt < float(min(SEL_TOP_N, n_blocks))), 1.0, 0.0)


def _nsa_prompt_kernel(q_ref, kc_ref, vc_ref, ks_ref, vs_ref, kw_ref, vw_ref, ovl_ref, e_ref, misc_ref,
                       o_ref, score_buf, *, tq, tk, tw, n_blocks):
    g = pl.program_id(1)
    t0 = pl.program_id(2) * tq
    rows = REP * tq
    q = q_ref[0].reshape(rows, HD)

    kc = kc_ref[0, 0]
    ncp = kc.shape[0]
    qpos_c = t0 + _iota((tq, ncp), 0)
    maskc = ((_iota((tq, ncp), 1) * CMP_STRIDE + (CMP_BLOCK - 1)) <= qpos_c)[None]
    s = _dot_nt(q, kc).reshape(REP, tq, ncp)
    sm = jnp.where(maskc, s, NEG)
    e = jnp.exp(sm - jnp.max(sm, axis=-1, keepdims=True))
    p = e / jnp.sum(e, axis=-1, keepdims=True) * jnp.where(maskc, 1.0, 0.0)
    o_c = _dot(p.reshape(rows, ncp).astype(BF16), vc_ref[0, 0])
    psum = jnp.sum(p, axis=0)

    imp_t = _dot_nt(ovl_ref[...], psum, HI)
    tpos = t0 + _iota(imp_t.shape, 1)
    n_live = jnp.minimum((t0 + tq - 1) // SEL_BLOCK + 1, n_blocks)
    selt = _select_blocks(imp_t, tpos, n_blocks, score_buf, n_live).astype(BF16)

    nsp = selt.shape[0]
    eye = jnp.where(_iota((nsp, nsp), 0) == _iota((nsp, nsp), 1), 1.0, 0.0).astype(BF16)
    bias = ((_dot_tn(selt, eye) - 1.0) * (-NEG)).astype(BF16)
    q_aug = jnp.concatenate([q, jnp.concatenate([bias] * REP, axis=0)], axis=1)
    ones_k = jnp.ones((HD, tk), BF16)
    qpos_k = t0 + _iota((tq, tk), 0)
    lane_k = _iota((tq, tk), 1)
    n_full = t0 // tk

    def step(kt, carry, diagonal):
        m, acc = carry
        k0 = pl.multiple_of(kt * tk, tk)
        k_aug = jnp.concatenate([ks_ref[0, 0, :, pl.ds(k0, tk)], e_ref[kt]], axis=0)
        v_aug = jnp.concatenate([vs_ref[0, 0, :, pl.ds(k0, tk)], ones_k], axis=0)
        s = _dot(q_aug, k_aug).reshape(REP, tq, tk)
        if diagonal:
            s = jnp.where((k0 + lane_k <= qpos_k)[None], s, NEG)
        m_new = jnp.maximum(m, jnp.max(s, axis=-1, keepdims=True))
        e = jnp.exp(s - m_new)
        pv = _dot_nt(e.reshape(rows, tk).astype(BF16), v_aug).reshape(REP, tq, 2 * HD)
        return m_new, jnp.exp(m - m_new) * acc + pv

    init = (jnp.full((REP, tq, 1), NEG, F32), jnp.zeros((REP, tq, 2 * HD), F32))
    carry = lax.fori_loop(0, n_full, functools.partial(step, diagonal=False), init)
    _, acc_s = step(n_full, carry, True)
    o_s = acc_s[..., :HD] / acc_s[..., HD:]

    wk = WINDOW + tw
    q3 = q.reshape(REP, tq, HD)
    ones_w = jnp.ones((HD, wk), BF16)
    o_w = []
    for w in range(tq // tw):
        t0w = t0 + w * tw
        start = pl.multiple_of(jnp.maximum(t0w - WINDOW, 0), tw)
        kw = kw_ref[0, 0, :, pl.ds(start, wk)]
        vw_aug = jnp.concatenate([vw_ref[0, 0, :, pl.ds(start, wk)], ones_w], axis=0)
        kpos = start + _iota((tw, wk), 1)
        qpos_w = t0w + _iota((tw, wk), 0)
        ok_w = (jnp.where(kpos <= qpos_w, jnp.where(kpos > qpos_w - WINDOW, 1.0, 0.0), 0.0) > 0.5)[None]
        qw = q3[:, w * tw:(w + 1) * tw].reshape(REP * tw, HD)
        sm = jnp.where(ok_w, _dot(qw, kw).reshape(REP, tw, wk), NEG)
        e = jnp.exp(sm - jnp.max(sm, axis=-1, keepdims=True))
        acc_w = _dot_nt(e.reshape(REP * tw, wk).astype(BF16), vw_aug).reshape(REP, tw, 2 * HD)
        o_w.append(acc_w[..., :HD] / acc_w[..., HD:])
    o_w = o_w[0] if len(o_w) == 1 else jnp.concatenate(o_w, axis=1)

    gates = jax.nn.sigmoid(misc_ref[0])
    gate_lane = _iota(gates.shape, 1) - (N_SSM_HEADS + 3 * REP * g)
    o_c = o_c.reshape(REP, tq, HD)
    for r in range(REP):
        gh = [jnp.sum(jnp.where(gate_lane == 3 * r + br, gates, 0.0), axis=-1, keepdims=True)
              for br in range(3)]
        o = gh[0] * o_c[r] + gh[1] * o_s[r] + gh[2] * o_w[r]
        o_ref[0, :, r * HD:(r + 1) * HD] = o.astype(o_ref.dtype)


def _nsa_prompt(qh, kch, vch, kvtb, ovl_t, e3, misc, n_blocks, tq, tk):
    bsz, nh, t, _ = qh.shape
    ncp = kch.shape[2]
    nsp = ovl_t.shape[0]
    kv_spec = lambda proj: pl.BlockSpec((1, 1, HD, t), lambda b, g, i: (b, N_KV * proj + g, 0, 0))
    cmp_spec = pl.BlockSpec((1, 1, ncp, HD), lambda b, g, i: (b, g, 0, 0))
    return pl.pallas_call(
        functools.partial(_nsa_prompt_kernel, tq=tq, tk=tk, tw=min(tq, 256), n_blocks=n_blocks),
        grid=(bsz, N_KV, t // tq),
        in_specs=[
            pl.BlockSpec((1, REP, tq, HD), lambda b, g, i: (b, g, i, 0)),
            cmp_spec, cmp_spec,
            kv_spec(2), kv_spec(3), kv_spec(4), kv_spec(5),
            pl.BlockSpec(ovl_t.shape, lambda b, g, i: (0, 0)),
            pl.BlockSpec(e3.shape, lambda b, g, i: (0, 0, 0)),
            pl.BlockSpec((1, tq, LANES), lambda b, g, i: (b, i, 0)),
        ],
        out_specs=pl.BlockSpec((1, tq, REP * HD), lambda b, g, i: (b, i, g)),
        out_shape=jax.ShapeDtypeStruct((bsz, t, nh * HD), BF16),
        scratch_shapes=[pltpu.VMEM((nsp, tq), F32)],
        compiler_params=_params(("parallel", "parallel", "arbitrary")),
        name="nsa_prompt",
    )(qh, kch, vch, kvtb, kvtb, kvtb, kvtb, ovl_t, e3, misc)


def _out_proj_kernel(y_ref, o_ref, h_ref, w_ref, g_ref, b_ref, out_ref, *, alpha):
    mixed = _dot(y_ref[...].astype(BF16), w_ref[0:D_SSM, :]) + _dot(o_ref[...], w_ref[D_SSM:, :])
    out_ref[...] = _layer_norm(alpha * h_ref[...] + mixed, g_ref[...], b_ref[...])


def _out_proj_ln(y, o, h, w, g, b, alpha):
    m, d = h.shape
    tm = min(512, m)
    row = lambda i: (i, 0)
    fixed = lambda i: (0, 0)
    return pl.pallas_call(
        functools.partial(_out_proj_kernel, alpha=alpha),
        grid=(m // tm,),
        in_specs=[
            pl.BlockSpec((tm, D_SSM), row),
            pl.BlockSpec((tm, D_NSA), row),
            pl.BlockSpec((tm, d), row),
            pl.BlockSpec(w.shape, fixed),
            pl.BlockSpec((1, d), fixed),
            pl.BlockSpec((1, d), fixed),
        ],
        out_specs=pl.BlockSpec((tm, d), row),
        out_shape=jax.ShapeDtypeStruct((m, d), F32),
        compiler_params=_params(("parallel",)),
        name="out_proj_ln",
    )(y, o, h, w, g, b)


def _row_group(shape):
    return _iota(shape, 0) // (shape[0] // N_KV)


def _cmp_attn_sample_kernel(q_ref, kc_ref, vc_ref, ssum_ref, oc_ref, psum_ref, *, past, s_len):
    q = q_ref[0]
    rows = q.shape[0]
    kc = kc_ref[0].astype(BF16)
    vc = vc_ref[0].astype(BF16)
    nc = kc.shape[0]
    t_row = (_iota((rows, nc), 0) // REP) % s_len
    cidx = _iota((rows, nc), 1)
    maskc = (cidx * CMP_STRIDE + (CMP_BLOCK - 1)) <= past + t_row
    s = _dot_nt(q, kc)
    sm = jnp.where(maskc, s, NEG)
    e = jnp.exp(sm - jnp.max(sm, axis=-1, keepdims=True))
    p = e / jnp.sum(e, axis=-1, keepdims=True) * jnp.where(maskc, 1.0, 0.0)
    o = _dot(p.astype(BF16), vc)
    own = (_iota(o.shape, 1) // HD) == _row_group(o.shape)
    oc_ref[0] = jnp.where(own, o, 0.0)
    psum_ref[0] = _dot(ssum_ref[...], p, HI)


def _cmp_attn_sample(qbd, kc, vc, ssum, past, s_len):
    dbz, rows, _ = qbd.shape
    nc = kc.shape[1]
    ng = ssum.shape[0]
    per_b = lambda b: (b, 0, 0)
    return pl.pallas_call(
        functools.partial(_cmp_attn_sample_kernel, past=past, s_len=s_len),
        grid=(dbz,),
        in_specs=[
            pl.BlockSpec((1, rows, KV_DIM), per_b),
            pl.BlockSpec((1, nc, KV_DIM), per_b),
            pl.BlockSpec((1, nc, KV_DIM), per_b),
            pl.BlockSpec(ssum.shape, lambda b: (0, 0)),
        ],
        out_specs=(pl.BlockSpec((1, rows, KV_DIM), per_b), pl.BlockSpec((1, ng, nc), per_b)),
        out_shape=(jax.ShapeDtypeStruct((dbz, rows, KV_DIM), F32),
                   jax.ShapeDtypeStruct((dbz, ng, nc), F32)),
        compiler_params=_params(("parallel",)),
        name="cmp_attn_sample",
    )(qbd, kc, vc, ssum)


def _select_sample_kernel(psum_ref, ovl_ref, selt_ref, score_buf, *, past, s_len, n_blocks):
    imp_t = _dot_nt(ovl_ref[...], psum_ref[...], HI)
    tpos = past + _iota(imp_t.shape, 1) % s_len
    selt_ref[...] = _select_blocks(imp_t, tpos, n_blocks, score_buf)


def _select_sample(psum_all, ovl_t, past, s_len, n_blocks):
    nsp = ovl_t.shape[0]
    cols = psum_all.shape[0]
    return pl.pallas_call(
        functools.partial(_select_sample_kernel, past=past, s_len=s_len, n_blocks=n_blocks),
        out_shape=jax.ShapeDtypeStruct((nsp, cols), F32),
        scratch_shapes=[pltpu.VMEM((nsp, cols), F32)],
        compiler_params=pltpu.CompilerParams(vmem_limit_bytes=VMEM_LIMIT),
        name="select_sample",
    )(psum_all, ovl_t)


def _online_update(state, sm, v_t):
    m, l, acc = state
    m_new = jnp.maximum(m, jnp.max(sm, axis=-1, keepdims=True))
    alpha = jnp.exp(m - m_new)
    e = jnp.exp(sm - m_new)
    l = alpha * l + jnp.sum(e, axis=-1, keepdims=True)
    return m_new, l, alpha * acc + _dot_nt(e.astype(BF16), v_t)


def _sel_win_sample_kernel(pt_ref, kcache_ref, vcache_ref, q_ref, sel_ref, e_ref, kt_ref, vt_ref,
                           kw_ref, vw_ref, kwn_ref, vwn_ref, oc_ref, gate_ref, o_ref, kbuf, vbuf, ksem, vsem,
                           *, n_pages, s_len, n_chunks, chunk):
    slot = _gather_pages(pt_ref, [(kcache_ref, kbuf, ksem), (vcache_ref, vbuf, vsem)], n_pages)

    q = q_ref[0]
    rows = q.shape[0]
    sel = sel_ref[0]
    sel_main = sel[:, :LANES]
    past = n_pages * PAGE
    t_col = (_iota((rows, 1), 0) // REP) % s_len
    pages_pc = chunk // PAGE

    def chunk_t(buf, c):
        return jnp.concatenate([buf[slot, (c * pages_pc + j) * PAGE:(c * pages_pc + j + 1) * PAGE, :]
                                for j in range(pages_pc)], axis=1).astype(BF16)

    state = (jnp.full((rows, 1), NEG, F32), jnp.zeros((rows, 1), F32), jnp.zeros((rows, KV_DIM), F32))
    for c in range(n_chunks):
        blk = _dot(sel_main, e_ref[c])
        kpos = c * chunk + _iota((rows, chunk), 1)
        ok = jnp.where(kpos <= past + t_col, blk, 0.0) > 0.5
        state = _online_update(state, jnp.where(ok, _dot(q, chunk_t(kbuf, c)), NEG), chunk_t(vbuf, c))
    lane_t = _iota((rows, kt_ref.shape[2]), 1)
    sel_new = sel[:, LANES:LANES + 1].astype(F32)
    ok = jnp.where(lane_t <= t_col, jnp.where(lane_t < s_len, sel_new, 0.0), 0.0) > 0.5
    _, l, acc = _online_update(state, jnp.where(ok, _dot(q, kt_ref[0].astype(BF16)), NEG),
                               vt_ref[0].astype(BF16))
    o_s = acc / l

    wb = kw_ref.shape[2]
    iw = _iota((rows, wb), 1)
    ok_c = iw > t_col + (wb - WINDOW)
    state = (jnp.full((rows, 1), NEG, F32), jnp.zeros((rows, 1), F32), jnp.zeros((rows, KV_DIM), F32))
    state = _online_update(state, jnp.where(ok_c, _dot(q, kw_ref[0].astype(BF16)), NEG), vw_ref[0].astype(BF16))
    ok_n = jnp.where(lane_t <= t_col, jnp.where(lane_t < s_len, 1.0, 0.0), 0.0) > 0.5
    _, l, acc = _online_update(state, jnp.where(ok_n, _dot(q, kwn_ref[0].astype(BF16)), NEG),
                               vwn_ref[0].astype(BF16))
    o_w = acc / l

    gates = jax.nn.sigmoid(gate_ref[0])
    o = gates[:, 0:1] * oc_ref[0] + gates[:, 1:2] * o_s + gates[:, 2:3] * o_w
    own = (_iota(o.shape, 1) // HD) == _row_group(o.shape)
    o = jnp.where(own, o, 0.0)
    o_ref[0] = o[:, :HD] + o[:, HD:]


def _sel_win_sample(page_table, kcache, vcache, qbd, selx, e4, ktail, vtail, kwin, vwin, kwnew, vwnew,
                    oc, graw, s_len):
    dbz, n_pages = page_table.shape
    rows = qbd.shape[1]
    n_chunks, _, chunk = e4.shape
    per_b = lambda b, pt: (b, 0, 0)
    blk = lambda a: pl.BlockSpec((1,) + a.shape[1:], per_b)
    grid_spec = pltpu.PrefetchScalarGridSpec(
        num_scalar_prefetch=1,
        grid=(dbz,),
        in_specs=[
            pl.BlockSpec(memory_space=pl.ANY),
            pl.BlockSpec(memory_space=pl.ANY),
            blk(qbd), blk(selx),
            pl.BlockSpec(e4.shape, lambda b, pt: (0, 0, 0)),
            blk(ktail), blk(vtail), blk(kwin), blk(vwin), blk(kwnew), blk(vwnew), blk(oc), blk(graw),
        ],
        out_specs=pl.BlockSpec((1, rows, HD), per_b),
        scratch_shapes=[
            pltpu.VMEM((2, n_pages * PAGE, PAGE), F32),
            pltpu.VMEM((2, n_pages * PAGE, PAGE), F32),
            pltpu.SemaphoreType.DMA((2,)),
            pltpu.SemaphoreType.DMA((2,)),
        ],
    )
    return pl.pallas_call(
        functools.partial(_sel_win_sample_kernel, n_pages=n_pages, s_len=s_len,
                          n_chunks=n_chunks, chunk=chunk),
        grid_spec=grid_spec,
        out_shape=jax.ShapeDtypeStruct((dbz, rows, HD), F32),
        compiler_params=_params(("arbitrary",)),
        name="sel_win_sample",
    )(page_table, kcache, vcache, qbd, selx, e4, ktail, vtail, kwin, vwin, kwnew, vwnew, oc, graw)


def _rope_tables(pos):
    half = HD // 2
    inv = ROPE_THETA ** (-jnp.arange(half, dtype=F32) / half)
    ang = pos.astype(F32)[:, None] * inv[None, :]
    cos = jnp.cos(ang)
    sin = jnp.sin(ang)
    reps = LANES // HD
    cos_l = jnp.tile(jnp.concatenate([cos, cos], axis=-1), (1, reps))
    sin_l = jnp.tile(jnp.concatenate([-sin, sin], axis=-1), (1, reps))
    return cos_l, sin_l, cos.T, sin.T


def _overlap_t(nc, ncp, ns, nsp):
    c_start = np.arange(ncp) * CMP_STRIDE
    s_start = np.arange(nsp) * SEL_BLOCK
    ovl = ((c_start[None, :] + CMP_BLOCK > s_start[:, None]) & (c_start[None, :] < s_start[:, None] + SEL_BLOCK))
    ovl = ovl & (np.arange(ncp)[None, :] < nc) & (np.arange(nsp)[:, None] < ns)
    return jnp.asarray(ovl.astype(np.float32))


def _expander(n_rows, n_chunks, chunk, dtype=BF16):
    key_block = (np.arange(n_chunks)[:, None] * chunk + np.arange(chunk)[None, :]) // SEL_BLOCK
    e = key_block[:, None, :] == np.arange(n_rows)[None, :, None]
    return jnp.asarray(e.astype(np.float32)).astype(dtype)


def _page_permutation():
    rows_pp = PAGE // CMP_STRIDE
    r = np.arange(PAGE)
    src = (r % rows_pp) * CMP_STRIDE + r // rows_pp
    return jnp.asarray((src[:, None] == np.arange(PAGE)[None, :]).astype(np.float32)).astype(BF16)


def _round_up(x, m):
    return -(-x // m) * m


def _compress_weights(w1, w2, pe):
    eye = jnp.eye(N_KV, dtype=F32)
    halves = []
    for half in range(2):
        wh = w1[half * CMP_STRIDE:(half + 1) * CMP_STRIDE]
        big = jnp.einsum("sdh,gk->sgdkh", wh, eye)
        halves.append(big.reshape(CMP_STRIDE * KV_DIM, N_KV * CMP_HIDDEN))
    w1big = jnp.concatenate(halves, axis=1).astype(BF16)
    w2big = jnp.einsum("hd,gk->ghkd", w2, eye).reshape(N_KV * CMP_HIDDEN, KV_DIM).astype(BF16)
    pe_rows = []
    for half in range(2):
        ph = pe[half * CMP_STRIDE:(half + 1) * CMP_STRIDE]
        row = jnp.broadcast_to(ph[:, None, :], (CMP_STRIDE, N_KV, HD)).reshape(1, CMP_STRIDE * KV_DIM)
        pe_rows.append(jnp.broadcast_to(row, (SUBLANES, CMP_STRIDE * KV_DIM)))
    pe2 = jnp.concatenate(pe_rows, axis=0).astype(BF16)
    return _page_permutation(), w1big, w2big, pe2


def _layer_weights(w, l):
    sizes = [D_SSM, CONV_DIM, N_SSM_HEADS, D_NSA, N_KV_PROJ * KV_DIM, N_HEADS * 3]
    offs = np.cumsum([0] + sizes)
    w_in = w["w_in"][l]
    seg = lambda i: w_in[:, offs[i]:offs[i + 1]]
    pad = LANES - sizes[2] - sizes[5]
    w_in_r = jnp.concatenate([seg(0), seg(1), seg(3), seg(2), seg(5),
                              jnp.zeros((w_in.shape[0], pad), F32)], axis=1).astype(BF16)
    bias_misc = jnp.concatenate([jnp.zeros((sizes[2],), F32), w["b_gate"][l], jnp.zeros((pad,), F32)])[None, :]
    lane_pad = lambda v: jnp.concatenate([v, jnp.zeros((LANES - v.shape[0],), F32)])[None, :]
    lw = {
        "w_in": w_in_r,
        "w_kv_t": seg(4).T.astype(BF16),
        "bias_misc": bias_misc,
        "conv_w": w["conv_w"][l],
        "conv_b_row": w["conv_b"][l][None, :],
        "dtb_row": lane_pad(w["dt_bias"][l]),
        "alog_row": lane_pad(w["a_log"][l]),
        "dtb_col": w["dt_bias"][l][:, None],
        "alog_col": w["a_log"][l][:, None],
        "dskip_row": jnp.repeat(w["d_skip"][l], HD)[None, :],
        "normw_row": w["ssm_norm_w"][l][None, :],
        "w_out": w["w_out"][l].astype(BF16),
    }
    lw["cmp_k"] = _compress_weights(w["cmp_k_w1"][l], w["cmp_k_w2"][l], w["cmp_k_pe"][l])
    lw["cmp_v"] = _compress_weights(w["cmp_v_w1"][l], w["cmp_v_w2"][l], w["cmp_v_pe"][l])
    for i in (1, 2, 3):
        lw[f"ln{i}"] = (w[f"ln{i}_g"][l][None, :], w[f"ln{i}_b"][l][None, :])
    for i in (1, 2):
        lw[f"ffn{i}"] = (w[f"ffn{i}_w_gate"][l].astype(BF16), w[f"ffn{i}_w_up"][l].astype(BF16),
                         w[f"ffn{i}_w_down"][l].astype(BF16))
    return lw


def _heads_major(x, bsz, t, n):
    return x.reshape(bsz, t, n, HD).transpose(0, 2, 1, 3)


def _mix_prompt(h, lw, bsz, t):
    m = bsz * t
    z, xbc, misc, qh, kvt, kvtb = _in_proj(h, lw["w_in"], lw["w_kv_t"], lw["bias_misc"],
                                           _rope_tables(jnp.arange(t)), bsz, t)
    kv5 = kvt.reshape(bsz, N_KV_PROJ, N_KV, HD, t)
    kv_rows = [kv5[:, i].transpose(0, 3, 1, 2) for i in range(N_KV_PROJ)]

    chunk = min(128, t)
    misc3 = misc.reshape(bsz, t, LANES)
    xbc3 = xbc.reshape(bsz, t, CONV_DIM)
    y_ssd, h_ssm = _ssd(z.reshape(bsz, t, D_SSM), xbc3, misc3, misc3.transpose(0, 2, 1), lw, chunk)
    conv_state = xbc3[:, t - (CONV_TAPS - 1):]

    n_str = t // CMP_STRIDE
    nc = n_str - 1
    ns = -(-t // SEL_BLOCK)
    nsp = _round_up(ns, SUBLANES)
    kvtb4 = kvtb.reshape(bsz, N_KV_PROJ, KV_DIM, t)
    kc = _compress(kvtb4, 0, *lw["cmp_k"])
    vc = _compress(kvtb4, 1, *lw["cmp_v"])
    kch = _heads_major(kc, bsz, n_str, N_KV).astype(BF16)
    vch = _heads_major(vc, bsz, n_str, N_KV).astype(BF16)
    tq = min(512, t)
    tk = min(512, t)
    e3 = _expander(nsp, t // tk, tk)
    o = _nsa_prompt(qh, kch, vch, kvtb.reshape(bsz, N_KV_PROJ * N_KV, HD, t), _overlap_t(nc, n_str, ns, nsp),
                    e3, misc3, ns, tq, tk)

    wb = min(WINDOW, t)
    state = tuple(kv_rows[:4]) + (kv_rows[4][:, t - wb:], kv_rows[5][:, t - wb:], h_ssm, conv_state)
    return y_ssd.reshape(m, D_SSM), o.reshape(m, D_NSA), state


def _mix_sample(h, lw, l, dbz, s_len, caches, state_ssm, state_conv, page_table):
    cache_k_cmp, cache_v_cmp, cache_k_slc, cache_v_slc, cache_k_win, cache_v_win = caches
    m = dbz * s_len
    n_pages = page_table.shape[1]
    past = n_pages * PAGE
    pos = past + jnp.arange(s_len)
    z, xbc, misc, qh, kvt, _ = _in_proj(h, lw["w_in"], lw["w_kv_t"], lw["bias_misc"],
                                        _rope_tables(jnp.tile(pos, dbz)), 1, m)
    kvs = kvt.reshape(N_KV_PROJ, KV_DIM, dbz, s_len)
    new_rows = [kvs[i].transpose(1, 2, 0) for i in range(N_KV_PROJ)]
    new_t = [kvs[i].transpose(1, 0, 2) for i in range(N_KV_PROJ)]
    k_c, v_c, k_s, v_s, k_w, v_w = [r.reshape(dbz, s_len, N_KV, HD) for r in new_rows]

    misc3 = misc.reshape(dbz, s_len, LANES)
    xbc3 = xbc.reshape(dbz, s_len, CONV_DIM)
    xh = jnp.concatenate([state_conv[l], xbc3], axis=1)
    y_t, h_new = _ssd_sample(xh.transpose(1, 2, 0), z.reshape(dbz, s_len, D_SSM).transpose(1, 2, 0),
                             misc3[:, :, :N_SSM_HEADS].transpose(1, 2, 0), state_ssm[l].transpose(1, 2, 3, 0),
                             lw, dbz, s_len)
    y_ssd = y_t.transpose(2, 0, 1)
    h_ssm = h_new.transpose(3, 0, 1, 2)
    conv_state = xh[:, -(CONV_TAPS - 1):]

    width = CMP_STRIDE * KV_DIM
    n_pool = cache_k_cmp.shape[1]
    pages_t = lambda cache: cache[l].transpose(0, 2, 3, 1).reshape(n_pool, KV_DIM, PAGE)

    def tail_rows(new):
        flat = new.reshape(dbz, 1, s_len * KV_DIM)
        return jnp.pad(flat, ((0, 0), (0, SUBLANES - 1), (0, width - s_len * KV_DIM)))

    kc = _compress_paged(page_table, pages_t(cache_k_cmp), tail_rows(new_rows[0]), *lw["cmp_k"])
    vc = _compress_paged(page_table, pages_t(cache_v_cmp), tail_rows(new_rows[1]), *lw["cmp_v"])
    nc = kc.shape[1]
    total = past + s_len
    ns = -(-total // SEL_BLOCK)
    nsp = _round_up(ns, SUBLANES)

    rows = N_KV * s_len * REP
    qg = qh.reshape(N_KV, REP, dbz, s_len, HD).transpose(2, 0, 3, 1, 4)
    qbd = jnp.einsum("bgtrd,gk->bgtrkd", qg, jnp.eye(N_KV, dtype=BF16)).reshape(dbz, rows, KV_DIM)
    ng = N_KV * s_len
    ssum = jnp.asarray((np.arange(ng)[:, None] == np.arange(rows)[None, :] // REP).astype(np.float32))
    oc, psum = _cmp_attn_sample(qbd, kc, vc, ssum, past, s_len)
    selt = _select_sample(psum.reshape(dbz * ng, nc), _overlap_t(nc, nc, ns, nsp), past, s_len, ns)
    sel = selt.T.reshape(dbz, ng, 1, nsp)
    sel = jnp.broadcast_to(sel, (dbz, ng, REP, nsp)).reshape(dbz, rows, nsp)
    n_cached = past // SEL_BLOCK
    selx = jnp.concatenate([sel[:, :, :n_cached], jnp.zeros((dbz, rows, LANES - n_cached), F32),
                            sel[:, :, n_cached:n_cached + 1], jnp.zeros((dbz, rows, LANES - 1), F32)],
                           axis=-1).astype(BF16)
    chunk_k = min(2048, past)
    e4 = _expander(LANES, past // chunk_k, chunk_k)
    pad_lanes = lambda a: jnp.pad(a, ((0, 0), (0, 0), (0, LANES - s_len)))
    wb = cache_k_win.shape[2]
    win_t = lambda cache: cache[l].transpose(0, 2, 3, 1).reshape(dbz, KV_DIM, wb)
    gate = misc3[:, :, N_SSM_HEADS:N_SSM_HEADS + 3 * N_HEADS].reshape(dbz, s_len, N_KV, REP, 3)
    graw = jnp.pad(gate.transpose(0, 2, 1, 3, 4).reshape(dbz, rows, 3), ((0, 0), (0, 0), (0, LANES - 3)))
    o = _sel_win_sample(page_table, pages_t(cache_k_slc), pages_t(cache_v_slc), qbd, selx, e4,
                        pad_lanes(new_t[2]), pad_lanes(new_t[3]), win_t(cache_k_win), win_t(cache_v_win),
                        pad_lanes(new_t[4]), pad_lanes(new_t[5]), oc, graw, s_len)
    o_nsa = o.reshape(dbz, N_KV, s_len, REP, HD).transpose(0, 2, 1, 3, 4).reshape(m, D_NSA).astype(BF16)

    kw_full = jnp.concatenate([cache_k_win[l], k_w], axis=1)
    vw_full = jnp.concatenate([cache_v_win[l], v_w], axis=1)
    state = (k_c, v_c, k_s, v_s, kw_full[:, -wb:], vw_full[:, -wb:], h_ssm, conv_state)
    return y_ssd.reshape(m, D_SSM), o_nsa, state


def _layer(x, lw, alpha, mix_fn):
    h1 = _ffn_ln(x, *lw["ffn1"], *lw["ln1"], alpha)
    y_ssd, o_nsa, state = mix_fn(h1)
    h2 = _out_proj_ln(y_ssd, o_nsa, h1, lw["w_out"], *lw["ln2"], alpha)
    return _ffn_ln(h2, *lw["ffn2"], *lw["ln3"], alpha), state


def kernel(x_prompt, x_sample, cache_k_cmp, cache_v_cmp, cache_k_slc, cache_v_slc, cache_k_win, cache_v_win, state_ssm, state_conv, page_table, w_in, b_gate, conv_w, conv_b, dt_bias, a_log, d_skip, ssm_norm_w, cmp_k_w1, cmp_k_w2, cmp_k_pe, cmp_v_w1, cmp_v_w2, cmp_v_pe, w_out, ln1_g, ln1_b, ln2_g, ln2_b, ln3_g, ln3_b, ffn1_w_gate, ffn1_w_up, ffn1_w_down, ffn2_w_gate, ffn2_w_up, ffn2_w_down):
    weights = dict(w_in=w_in, b_gate=b_gate, conv_w=conv_w, conv_b=conv_b, dt_bias=dt_bias, a_log=a_log,
                   d_skip=d_skip, ssm_norm_w=ssm_norm_w, cmp_k_w1=cmp_k_w1, cmp_k_w2=cmp_k_w2,
                   cmp_k_pe=cmp_k_pe, cmp_v_w1=cmp_v_w1, cmp_v_w2=cmp_v_w2, cmp_v_pe=cmp_v_pe, w_out=w_out,
                   ln1_g=ln1_g, ln1_b=ln1_b, ln2_g=ln2_g, ln2_b=ln2_b, ln3_g=ln3_g, ln3_b=ln3_b,
                   ffn1_w_gate=ffn1_w_gate, ffn1_w_up=ffn1_w_up, ffn1_w_down=ffn1_w_down,
                   ffn2_w_gate=ffn2_w_gate, ffn2_w_up=ffn2_w_up, ffn2_w_down=ffn2_w_down)
    depth = w_in.shape[0]
    bsz, t, d = x_prompt.shape
    dbz, s_len, _ = x_sample.shape
    alpha = (2.0 * depth) ** 0.25
    caches = (cache_k_cmp, cache_v_cmp, cache_k_slc, cache_v_slc, cache_k_win, cache_v_win)
    y_p = x_prompt.reshape(bsz * t, d)
    y_s = x_sample.reshape(dbz * s_len, d)
    p_states, s_states = [], []
    for l in range(depth):
        lw = _layer_weights(weights, l)
        y_p, st_p = _layer(y_p, lw, alpha, lambda h: _mix_prompt(h, lw, bsz, t))
        y_s, st_s = _layer(y_s, lw, alpha, lambda h: _mix_sample(h, lw, l, dbz, s_len, caches, state_ssm,
                                                                 state_conv, page_table))
        p_states.append(st_p)
        s_states.append(st_s)
    p_st = [jnp.stack(a) for a in zip(*p_states)]
    s_st = [jnp.stack(a) for a in zip(*s_states)]
    outs = [y_p.reshape(bsz, t, d), y_s.reshape(dbz, s_len, d)]
    for p, s in zip(p_st[:6], s_st[:6]):
        outs += [p, s]
    outs += [p_st[6], s_st[6], p_st[7], s_st[7]]
    return tuple(outs)
```

```python
import functools
import math

import numpy as np
import jax
import jax.numpy as jnp
from jax import lax
from jax.experimental import pallas as pl
from jax.experimental.pallas import tpu as pltpu

F32 = jnp.float32
BF16 = jnp.bfloat16
HI = lax.Precision.HIGHEST

HD = 64
N_SSM_HEADS = 8
N_SSM_GROUPS = 2
D_SSM = 512
CONV_DIM = 768
CONV_TAPS = 4
N_HEADS = 8
N_KV = 2
REP = N_HEADS // N_KV
D_NSA = 512
KV_DIM = N_KV * HD
N_KV_PROJ = 6
CMP_STRIDE = 16
CMP_BLOCK = 32
CMP_HIDDEN = 128
SEL_BLOCK = 64
SEL_TOP_N = 16
WINDOW = 512
PAGE = 128
ROPE_THETA = 10000.0
LN_EPS = 1e-5
RMS_EPS = 1e-5
NEG = -1e30
FORCED_SCORE = 1e30
INVALID_SCORE = -1.0

LANES = 128
SUBLANES = 8
VMEM_LIMIT = 56 * 1024 * 1024

NT_DIMS = (((1,), (1,)), ((), ()))
TN_DIMS = (((0,), (0,)), ((), ()))


def _params(sem):
    return pltpu.CompilerParams(dimension_semantics=sem, vmem_limit_bytes=VMEM_LIMIT)


def _dot(a, b, precision=None):
    return jnp.dot(a, b, preferred_element_type=F32, precision=precision)


def _dot_nt(a, b, precision=None):
    return lax.dot_general(a, b, NT_DIMS, preferred_element_type=F32, precision=precision)


def _dot_tn(a, b, precision=None):
    return lax.dot_general(a, b, TN_DIMS, preferred_element_type=F32, precision=precision)


def _iota(shape, dim):
    return lax.broadcasted_iota(jnp.int32, shape, dim)


def _layer_norm(y, g, b):
    mu = jnp.mean(y, axis=-1, keepdims=True)
    yc = y - mu
    var = jnp.mean(yc * yc, axis=-1, keepdims=True)
    return yc * lax.rsqrt(var + LN_EPS) * g + b


def _ffn_ln_kernel(x_ref, wg_ref, wu_ref, wd_ref, g_ref, b_ref, o_ref, xb_ref, acc_ref, *, alpha):
    f = pl.program_id(1)

    @pl.when(f == 0)
    def _():
        xb_ref[...] = x_ref[...].astype(BF16)
        acc_ref[...] = jnp.zeros_like(acc_ref)

    xb = xb_ref[...]
    gate = _dot(xb, wg_ref[...])
    up = _dot(xb, wu_ref[...])
    act = (jax.nn.silu(gate) * up).astype(BF16)
    acc_ref[...] += _dot(act, wd_ref[...])

    @pl.when(f == pl.num_programs(1) - 1)
    def _():
        y = alpha * x_ref[...] + 0.5 * acc_ref[...]
        o_ref[...] = _layer_norm(y, g_ref[...], b_ref[...])


def _ffn_ln(x, wg, wu, wd, g, b, alpha):
    m, d = x.shape
    dff = wg.shape[1]
    tm = min(512, m)
    tf = dff // 2 if (dff // 2) % LANES == 0 else dff
    grid = (m // tm, dff // tf)
    return pl.pallas_call(
        functools.partial(_ffn_ln_kernel, alpha=alpha),
        grid=grid,
        in_specs=[
            pl.BlockSpec((tm, d), lambda i, f: (i, 0)),
            pl.BlockSpec((d, tf), lambda i, f: (0, f)),
            pl.BlockSpec((d, tf), lambda i, f: (0, f)),
            pl.BlockSpec((tf, d), lambda i, f: (f, 0)),
            pl.BlockSpec((1, d), lambda i, f: (0, 0)),
            pl.BlockSpec((1, d), lambda i, f: (0, 0)),
        ],
        out_specs=pl.BlockSpec((tm, d), lambda i, f: (i, 0)),
        out_shape=jax.ShapeDtypeStruct((m, d), F32),
        scratch_shapes=[pltpu.VMEM((tm, d), BF16), pltpu.VMEM((tm, d), F32)],
        compiler_params=_params(("parallel", "arbitrary")),
        name="ffn_ln",
    )(x, wg, wu, wd, g, b)


OFF_Z, OFF_XBC, OFF_Q, OFF_MISC, W_IN_COLS = 0, 512, 1280, 1792, 1920


def _in_proj_kernel(h_ref, w_ref, wkv_ref, bias_ref, cos_ref, sin_ref, cost_ref, sint_ref,
                    z_ref, xbc_ref, misc_ref, qh_ref, kvtb_ref, *kvt_refs, q_scale):
    hb = h_ref[...].astype(BF16)
    z_ref[...] = _dot(hb, w_ref[:, OFF_Z:OFF_XBC])
    xbc_ref[...] = _dot(hb, w_ref[:, OFF_XBC:OFF_Q])
    misc_ref[...] = _dot(hb, w_ref[:, OFF_MISC:W_IN_COLS]) + bias_ref[...]
    cos = cos_ref[...]
    sin = sin_ref[...]
    first_half = (_iota(cos.shape, 1) & (HD - 1)) < (HD // 2)
    for c in range(D_NSA // LANES):
        lo = OFF_Q + c * LANES
        x = _dot(hb, w_ref[:, lo:lo + LANES])
        rot = jnp.where(first_half, pltpu.roll(x, LANES - HD // 2, 1), pltpu.roll(x, HD // 2, 1))
        q = ((x * cos + rot * sin) * q_scale).astype(BF16)
        for j in range(LANES // HD):
            qh_ref[0, c * (LANES // HD) + j] = q[:, j * HD:(j + 1) * HD]
    kvt = _dot_nt(wkv_ref[...], hb)
    cost = cost_ref[...]
    sint = sint_ref[...]
    half = HD // 2
    for i in range(N_KV_PROJ):
        blk = kvt[i * KV_DIM:(i + 1) * KV_DIM, :]
        if i % 2 == 0:
            parts = []
            for g in range(N_KV):
                x1 = blk[g * HD:g * HD + half, :]
                x2 = blk[g * HD + half:(g + 1) * HD, :]
                parts += [x1 * cost - x2 * sint, x2 * cost + x1 * sint]
            blk = jnp.concatenate(parts, axis=0)
        kvt_refs[i][0] = blk
        kvtb_ref[0, i * KV_DIM:(i + 1) * KV_DIM, :] = blk.astype(BF16)


def _in_proj(h, w, wkv, bias, tables, n_seq, t_seq):
    cos, sin, cost, sint = tables
    m, d = h.shape
    tm = min(512, t_seq)
    n_pos = t_seq // tm
    row = lambda i: (i, 0)
    fixed = lambda i: (0, 0)
    out_shapes = (
        jax.ShapeDtypeStruct((m, D_SSM), F32),
        jax.ShapeDtypeStruct((m, CONV_DIM), F32),
        jax.ShapeDtypeStruct((m, LANES), F32),
        jax.ShapeDtypeStruct((n_seq, N_HEADS, t_seq, HD), BF16),
        jax.ShapeDtypeStruct((n_seq, N_KV_PROJ * KV_DIM, t_seq), BF16),
    ) + (jax.ShapeDtypeStruct((n_seq, KV_DIM, t_seq), F32),) * N_KV_PROJ
    seq_t = lambda i: (i // n_pos, 0, i % n_pos)
    return pl.pallas_call(
        functools.partial(_in_proj_kernel, q_scale=HD ** -0.5),
        grid=(m // tm,),
        in_specs=[
            pl.BlockSpec((tm, d), row),
            pl.BlockSpec(w.shape, fixed),
            pl.BlockSpec(wkv.shape, fixed),
            pl.BlockSpec((1, LANES), fixed),
            pl.BlockSpec((tm, LANES), lambda i: (i % n_pos, 0)),
            pl.BlockSpec((tm, LANES), lambda i: (i % n_pos, 0)),
            pl.BlockSpec((HD // 2, tm), lambda i: (0, i % n_pos)),
            pl.BlockSpec((HD // 2, tm), lambda i: (0, i % n_pos)),
        ],
        out_specs=(
            pl.BlockSpec((tm, D_SSM), row),
            pl.BlockSpec((tm, CONV_DIM), row),
            pl.BlockSpec((tm, LANES), row),
            pl.BlockSpec((1, N_HEADS, tm, HD), lambda i: (i // n_pos, 0, i % n_pos, 0)),
            pl.BlockSpec((1, N_KV_PROJ * KV_DIM, tm), seq_t),
        ) + (pl.BlockSpec((1, KV_DIM, tm), seq_t),) * N_KV_PROJ,
        out_shape=out_shapes,
        compiler_params=_params(("parallel",)),
        name="in_proj",
    )(h, w, wkv, bias, cos, sin, cost, sint)


def _ssd_kernel(z_ref, xbc_ref, misc_ref, misct_ref, convw_ref, convb_ref,
                dtb_row_ref, alog_row_ref, dtb_col_ref, alog_col_ref, dskip_ref, normw_ref,
                y_ref, hout_ref, xbuf, hstate, ybuf, *, chunk):
    c = pl.program_id(1)
    L = chunk
    H = SUBLANES

    @pl.when(c == 0)
    def _():
        xbuf[0:H, :] = jnp.zeros((H, CONV_DIM), F32)
        hstate[...] = jnp.zeros_like(hstate)

    x = xbc_ref[0]
    xbuf[H:H + L, :] = x
    conv = convb_ref[...] + convw_ref[CONV_TAPS - 1:CONV_TAPS, :] * x
    for k in range(1, CONV_TAPS):
        conv = conv + convw_ref[CONV_TAPS - 1 - k:CONV_TAPS - k, :] * xbuf[H - k:H - k + L, :]
    xbuf[0:H, :] = x[L - H:L, :]
    xact = jax.nn.silu(conv)
    bm = xact[:, D_SSM:D_SSM + N_SSM_GROUPS * HD]
    cm = xact[:, D_SSM + N_SSM_GROUPS * HD:]

    dt = jax.nn.softplus(misc_ref[0] + dtb_row_ref[...])
    dtt = jax.nn.softplus(misct_ref[0, 0:SUBLANES, :] + dtb_col_ref[...])
    a_row = -jnp.exp(alog_row_ref[...])
    a_col = -jnp.exp(alog_col_ref[...])
    lane_ok = _iota(dt.shape, 1) < N_SSM_HEADS
    da = jnp.where(lane_ok, dt * a_row, 0.0)
    dat = dtt * a_col
    ri = _iota((L, L), 0)
    ci = _iota((L, L), 1)
    causal = ri >= ci
    tri = jnp.where(causal, 1.0, 0.0).astype(F32)
    cum = _dot(tri, da, HI)
    cumt = _dot_nt(dat, tri, HI)
    cum_last = cum[L - 1:L, :]

    rep = N_SSM_HEADS // N_SSM_GROUPS
    for g in range(N_SSM_GROUPS):
        cm_g = cm[:, g * HD:(g + 1) * HD].astype(BF16)
        bm_g = bm[:, g * HD:(g + 1) * HD].astype(BF16)
        cb = _dot_nt(cm_g, bm_g)
        for r in range(rep):
            h = g * rep + r
            ch = cum[:, h:h + 1]
            seg = ch - cumt[h:h + 1, :]
            decay = jnp.where(causal, jnp.exp(jnp.where(causal, seg, 0.0)), 0.0)
            xs_h = xact[:, h * HD:(h + 1) * HD]
            xdt = xs_h * dt[:, h:h + 1]
            hprev = hstate[h]
            y_h = (_dot((cb * decay).astype(BF16), xdt.astype(BF16))
                   + _dot_nt(cm_g, hprev.astype(BF16)) * jnp.exp(ch))
            cl = cum_last[:, h:h + 1]
            tail = jnp.exp(cl - ch)
            hstate[h] = hprev * jnp.exp(cl) + _dot_tn((xdt * tail).astype(BF16), bm_g)
            ybuf[:, h * HD:(h + 1) * HD] = y_h

    xs = xact[:, :D_SSM]
    y = (ybuf[...] + dskip_ref[...] * xs) * jax.nn.silu(z_ref[0])
    y = y * lax.rsqrt(jnp.mean(y * y, axis=-1, keepdims=True) + RMS_EPS) * normw_ref[...]
    y_ref[0] = y

    @pl.when(c == pl.num_programs(1) - 1)
    def _():
        hout_ref[0] = hstate[...]


def _ssd(z, xbc, misc, misct, lw, chunk):
    bsz, t, _ = z.shape
    nc = t // chunk
    seq = lambda b, c: (b, c, 0)
    per_b4 = lambda b, c: (b, 0, 0, 0)
    fixed = lambda b, c: (0, 0)
    return pl.pallas_call(
        functools.partial(_ssd_kernel, chunk=chunk),
        grid=(bsz, nc),
        in_specs=[
            pl.BlockSpec((1, chunk, D_SSM), seq),
            pl.BlockSpec((1, chunk, CONV_DIM), seq),
            pl.BlockSpec((1, chunk, LANES), seq),
            pl.BlockSpec((1, LANES, chunk), lambda b, c: (b, 0, c)),
            pl.BlockSpec((CONV_TAPS, CONV_DIM), fixed),
            pl.BlockSpec((1, CONV_DIM), fixed),
            pl.BlockSpec((1, LANES), fixed),
            pl.BlockSpec((1, LANES), fixed),
            pl.BlockSpec((SUBLANES, 1), fixed),
            pl.BlockSpec((SUBLANES, 1), fixed),
            pl.BlockSpec((1, D_SSM), fixed),
            pl.BlockSpec((1, D_SSM), fixed),
        ],
        out_specs=(
            pl.BlockSpec((1, chunk, D_SSM), seq),
            pl.BlockSpec((1, N_SSM_HEADS, HD, HD), per_b4),
        ),
        out_shape=(
            jax.ShapeDtypeStruct((bsz, t, D_SSM), F32),
            jax.ShapeDtypeStruct((bsz, N_SSM_HEADS, HD, HD), F32),
        ),
        scratch_shapes=[
            pltpu.VMEM((SUBLANES + chunk, CONV_DIM), F32),
            pltpu.VMEM((N_SSM_HEADS, HD, HD), F32),
            pltpu.VMEM((chunk, D_SSM), F32),
        ],
        compiler_params=_params(("parallel", "arbitrary")),
        name="ssd",
    )(z, xbc, misc, misct, lw["conv_w"], lw["conv_b_row"], lw["dtb_row"], lw["alog_row"],
      lw["dtb_col"], lw["alog_col"], lw["dskip_row"], lw["normw_row"])


def _ssd_sample_kernel(xh_ref, z_ref, dtraw_ref, state_ref, convw_ref, convb_ref, dtb_ref, alog_ref,
                       dskip_ref, normw_ref, y_ref, hout_ref, xact, ypre, *, s_len):
    h = pl.program_id(0)

    @pl.when(h == 0)
    def _():
        for t in range(s_len):
            conv = convb_ref[...]
            for k in range(CONV_TAPS):
                conv = conv + convw_ref[k] * xh_ref[t + k]
            xact[t] = jax.nn.silu(conv)

    a = -jnp.exp(alog_ref[pl.ds(h, 1), :])
    dts = [jax.nn.softplus(dtraw_ref[t, pl.ds(h, 1), :] + dtb_ref[pl.ds(h, 1), :]) for t in range(s_len)]
    das = [jnp.exp(dt * a) for dt in dts]
    g = h // (N_SSM_HEADS // N_SSM_GROUPS)
    b_lo = pl.multiple_of(D_SSM + g * HD, HD)
    c_lo = pl.multiple_of(D_SSM + N_SSM_GROUPS * HD + g * HD, HD)
    bs = [xact[t, pl.ds(b_lo, HD), :] for t in range(s_len)]
    cs = [xact[t, pl.ds(c_lo, HD), :] for t in range(s_len)]

    def body(p, carry):
        hs = state_ref[0, p]
        row = h * HD + p
        for t in range(s_len):
            x = xact[t, pl.ds(row, 1), :]
            hs = hs * das[t] + (x * dts[t]) * bs[t]
            y = jnp.sum(cs[t] * hs, axis=0, keepdims=True)
            ypre[t, pl.ds(row, 1), :] = y + dskip_ref[pl.ds(row, 1), :] * x
        hout_ref[0, p] = hs
        return carry

    lax.fori_loop(0, HD, body, 0)

    @pl.when(h == pl.num_programs(0) - 1)
    def _():
        for t in range(s_len):
            y = ypre[t] * jax.nn.silu(z_ref[t])
            y = y * lax.rsqrt(jnp.mean(y * y, axis=0, keepdims=True) + RMS_EPS) * normw_ref[...]
            y_ref[t] = y


def _ssd_sample(xh, zt, dtraw, state, lw, dbz, s_len):
    lane_b = lambda v: jnp.broadcast_to(v[..., None], v.shape + (dbz,))
    full = lambda a: pl.BlockSpec(a.shape, lambda h: (0,) * a.ndim)
    args = (xh, zt, dtraw, state, lane_b(lw["conv_w"]), lane_b(lw["conv_b_row"][0]), lane_b(lw["dtb_col"][:, 0]),
            lane_b(lw["alog_col"][:, 0]), lane_b(lw["dskip_row"][0]), lane_b(lw["normw_row"][0]))
    state_spec = pl.BlockSpec((1, HD, HD, dbz), lambda h: (h, 0, 0, 0))
    in_specs = [full(a) for a in args]
    in_specs[3] = state_spec
    return pl.pallas_call(
        functools.partial(_ssd_sample_kernel, s_len=s_len),
        grid=(N_SSM_HEADS,),
        in_specs=in_specs,
        out_specs=(pl.BlockSpec((s_len, D_SSM, dbz), lambda h: (0, 0, 0)), state_spec),
        out_shape=(jax.ShapeDtypeStruct((s_len, D_SSM, dbz), F32),
                   jax.ShapeDtypeStruct(state.shape, F32)),
        scratch_shapes=[pltpu.VMEM((s_len, CONV_DIM, dbz), F32), pltpu.VMEM((s_len, D_SSM, dbz), F32)],
        compiler_params=_params(("arbitrary",)),
        name="ssd_sample",
    )(*args)


def _page_group(n_pages):
    return math.gcd(n_pages, 8)


def _compress_pages(get_pages, n_pages, tail_rows, pm_ref, w1_ref, w2_ref, pe_ref, x_buf, sec_buf):
    rows_pp = PAGE // CMP_STRIDE
    n = n_pages * rows_pp
    half = N_KV * CMP_HIDDEN
    group = _page_group(n_pages)

    for i in range(n_pages // group):
        perm = _dot_nt(pm_ref[...], get_pages(i))
        for j in range(group):
            r0 = (i * group + j) * rows_pp
            for s in range(CMP_STRIDE):
                x_buf[r0:r0 + rows_pp, s * KV_DIM:(s + 1) * KV_DIM] = (
                    perm[s * rows_pp:(s + 1) * rows_pp, j * KV_DIM:(j + 1) * KV_DIM])
    x_buf[n:n + SUBLANES, :] = tail_rows
    out = _dot(x_buf[...].astype(BF16), w1_ref[...])
    sec_buf[...] = out[:, half:]
    pe_out = _dot(pe_ref[...], w1_ref[...])
    pe_term = pe_out[0:1, :half] + pe_out[SUBLANES:SUBLANES + 1, half:]
    pre = out[0:n, :half] + sec_buf[1:n + 1, :] + pe_term
    return _dot(jax.nn.gelu(pre).astype(BF16), w2_ref[...])


def _compress_kernel(x_ref, pm_ref, w1_ref, w2_ref, pe_ref, o_ref, x_buf, sec_buf, *, n_pages):
    group = _page_group(n_pages)

    def get_pages(i):
        wide = x_ref[0, 0, :, i * group * PAGE:(i + 1) * group * PAGE]
        return jnp.concatenate([wide[:, j * PAGE:(j + 1) * PAGE] for j in range(group)], axis=0)

    tail_rows = jnp.zeros((SUBLANES, CMP_STRIDE * KV_DIM), F32)
    o_ref[0] = _compress_pages(get_pages, n_pages, tail_rows, pm_ref, w1_ref, w2_ref, pe_ref, x_buf, sec_buf)


def _compress(kvtb4, proj, pm, w1big, w2big, pe2):
    bsz, _, _, t = kvtb4.shape
    n_pages = t // PAGE
    n = t // CMP_STRIDE
    fixed = lambda b: (0, 0)
    return pl.pallas_call(
        functools.partial(_compress_kernel, n_pages=n_pages),
        grid=(bsz,),
        in_specs=[
            pl.BlockSpec((1, 1, KV_DIM, t), lambda b: (b, proj, 0, 0)),
            pl.BlockSpec(pm.shape, fixed),
            pl.BlockSpec(w1big.shape, fixed),
            pl.BlockSpec(w2big.shape, fixed),
            pl.BlockSpec(pe2.shape, fixed),
        ],
        out_specs=pl.BlockSpec((1, n, KV_DIM), lambda b: (b, 0, 0)),
        out_shape=jax.ShapeDtypeStruct((bsz, n, KV_DIM), F32),
        scratch_shapes=[pltpu.VMEM((n + SUBLANES, CMP_STRIDE * KV_DIM), F32),
                        pltpu.VMEM((n + SUBLANES, N_KV * CMP_HIDDEN), F32)],
        compiler_params=_params(("parallel",)),
        name="compress",
    )(kvtb4, pm, w1big, w2big, pe2)


def _page_copy(cache_ref, buf, sem, page, slot, p):
    return pltpu.make_async_copy(cache_ref.at[page], buf.at[slot, pl.ds(p * PAGE, PAGE)], sem.at[slot])


def _gather_pages(pt_ref, streams, n_pages):
    b = pl.program_id(0)
    slot = b % 2

    def issue(seq, into):
        def start(p, carry):
            page = pt_ref[seq, p]
            for cache_ref, buf, sem in streams:
                _page_copy(cache_ref, buf, sem, page, into, p).start()
            return carry
        lax.fori_loop(0, n_pages, start, 0, unroll=_page_group(n_pages))

    @pl.when(b == 0)
    def _():
        issue(b, slot)

    @pl.when(b + 1 < pl.num_programs(0))
    def _():
        issue(b + 1, 1 - slot)

    def wait(p, carry):
        for cache_ref, buf, sem in streams:
            _page_copy(cache_ref, buf, sem, 0, slot, p).wait()
        return carry

    lax.fori_loop(0, n_pages, wait, 0, unroll=_page_group(n_pages))
    return slot


def _compress_paged_kernel(pt_ref, cache_ref, tail_ref, pm_ref, w1_ref, w2_ref, pe_ref, o_ref,
                           buf, x_buf, sec_buf, sem, *, n_pages):
    slot = _gather_pages(pt_ref, [(cache_ref, buf, sem)], n_pages)
    rows = _page_group(n_pages) * PAGE
    get_pages = lambda i: buf[slot, i * rows:(i + 1) * rows, :].astype(BF16)
    o_ref[0] = _compress_pages(get_pages, n_pages, tail_ref[0], pm_ref, w1_ref, w2_ref, pe_ref, x_buf, sec_buf)


def _compress_paged(page_table, cache, tail, pm, w1big, w2big, pe2):
    dbz, n_pages = page_table.shape
    n = n_pages * (PAGE // CMP_STRIDE)
    fixed = lambda b, pt: (0, 0)
    grid_spec = pltpu.PrefetchScalarGridSpec(
        num_scalar_prefetch=1,
        grid=(dbz,),
        in_specs=[
            pl.BlockSpec(memory_space=pl.ANY),
            pl.BlockSpec((1,) + tail.shape[1:], lambda b, pt: (b, 0, 0)),
            pl.BlockSpec(pm.shape, fixed),
            pl.BlockSpec(w1big.shape, fixed),
            pl.BlockSpec(w2big.shape, fixed),
            pl.BlockSpec(pe2.shape, fixed),
        ],
        out_specs=pl.BlockSpec((1, n, KV_DIM), lambda b, pt: (b, 0, 0)),
        scratch_shapes=[
            pltpu.VMEM((2, n_pages * PAGE, PAGE), F32),
            pltpu.VMEM((n + SUBLANES, CMP_STRIDE * KV_DIM), F32),
            pltpu.VMEM((n + SUBLANES, N_KV * CMP_HIDDEN), F32),
            pltpu.SemaphoreType.DMA((2,)),
        ],
    )
    return pl.pallas_call(
        functools.partial(_compress_paged_kernel, n_pages=n_pages),
        grid_spec=grid_spec,
        out_shape=jax.ShapeDtypeStruct((dbz, n, KV_DIM), F32),
        compiler_params=_params(("arbitrary",)),
        name="compress_paged",
    )(page_table, cache, tail, pm, w1big, w2big, pe2)


RANK_UNROLL = 4


def _select_blocks(imp_t, tpos, n_blocks, score_buf, n_live=None):
    j = _iota(imp_t.shape, 0)
    valid = (j * SEL_BLOCK <= tpos) & (j < n_blocks)
    cur = tpos // SEL_BLOCK
    forced = (j == 0) | (j == cur) | (j == cur - 1)
    score = jnp.where(valid, jnp.where(forced, FORCED_SCORE, imp_t), INVALID_SCORE)
    score_buf[...] = score

    def body(i, cnt):
        row = score_buf[pl.ds(i, 1), :]
        above = jnp.where(row > score, 1.0, 0.0)
        tie = jnp.where(row == score, jnp.where(j > i, 1.0, 0.0), 0.0)
        return cnt + above + tie

    zero = jnp.zeros(imp_t.shape, F32)
    if n_live is None:
        cnt = lax.fori_loop(0, n_blocks, body, zero, unroll=RANK_UNROLL)
    else:
        def group(gi, cnt):
            for u in range(RANK_UNROLL):
                cnt = body(gi * RANK_UNROLL + u, cnt)
            return cnt
        assert imp_t.shape[0] % RANK_UNROLL == 0
        cnt = lax.fori_loop(0, (n_live + RANK_UNROLL - 1) // RANK_UNROLL, group, zero)
    return jnp.where(valid & (cnt < float(min(SEL_TOP_N, n_blocks))), 1.0, 0.0)


def _nsa_prompt_kernel(q_ref, kc_ref, vc_ref, ks_ref, vs_ref, kw_ref, vw_ref, ovl_ref, e_ref, wbias_ref, misc_ref,
                       o_ref, score_buf, *, tq, tk, tw, n_blocks):
    g = pl.program_id(1)
    t0 = pl.program_id(2) * tq
    rows = REP * tq
    q = q_ref[0].reshape(rows, HD)

    kc = kc_ref[0, 0]
    ncp = kc.shape[0]
    qpos_c = t0 + _iota((tq, ncp), 0)
    maskc = ((_iota((tq, ncp), 1) * CMP_STRIDE + (CMP_BLOCK - 1)) <= qpos_c)[None]
    s = _dot_nt(q, kc).reshape(REP, tq, ncp)
    sm = jnp.where(maskc, s, NEG)
    e = jnp.exp(sm - jnp.max(sm, axis=-1, keepdims=True))
    p = e / jnp.sum(e, axis=-1, keepdims=True) * jnp.where(maskc, 1.0, 0.0)
    o_c = _dot(p.reshape(rows, ncp).astype(BF16), vc_ref[0, 0])
    psum = jnp.sum(p, axis=0)

    imp_t = _dot_nt(ovl_ref[...], psum, HI)
    tpos = t0 + _iota(imp_t.shape, 1)
    n_live = jnp.minimum((t0 + tq - 1) // SEL_BLOCK + 1, n_blocks)
    selt = _select_blocks(imp_t, tpos, n_blocks, score_buf, n_live).astype(BF16)

    nsp = selt.shape[0]
    eye = jnp.where(_iota((nsp, nsp), 0) == _iota((nsp, nsp), 1), 1.0, 0.0).astype(BF16)
    bias = ((_dot_tn(selt, eye) - 1.0) * (-NEG)).astype(BF16)
    q_aug = jnp.concatenate([q, jnp.concatenate([bias] * REP, axis=0)], axis=1)
    ones_k = jnp.ones((HD, tk), BF16)
    qpos_k = t0 + _iota((tq, tk), 0)
    lane_k = _iota((tq, tk), 1)
    n_full = t0 // tk

    def step(kt, carry, diagonal):
        m, acc = carry
        k0 = pl.multiple_of(kt * tk, tk)
        k_aug = jnp.concatenate([ks_ref[0, 0, :, pl.ds(k0, tk)], e_ref[kt]], axis=0)
        v_aug = jnp.concatenate([vs_ref[0, 0, :, pl.ds(k0, tk)], ones_k], axis=0)
        s = _dot(q_aug, k_aug).reshape(REP, tq, tk)
        if diagonal:
            s = jnp.where((k0 + lane_k <= qpos_k)[None], s, NEG)
        m_new = jnp.maximum(m, jnp.max(s, axis=-1, keepdims=True))
        e = jnp.exp(s - m_new)
        pv = _dot_nt(e.reshape(rows, tk).astype(BF16), v_aug).reshape(REP, tq, 2 * HD)
        return m_new, jnp.exp(m - m_new) * acc + pv

    init = (jnp.full((REP, tq, 1), NEG, F32), jnp.zeros((REP, tq, 2 * HD), F32))
    carry = lax.fori_loop(0, n_full, functools.partial(step, diagonal=False), init)
    _, acc_s = step(n_full, carry, True)
    o_s = acc_s[..., :HD] / acc_s[..., HD:]

    wk = WINDOW + tw
    q3 = q.reshape(REP, tq, HD)
    ones_w = jnp.ones((HD, wk), BF16)
    o_w = []
    for w in range(tq // tw):
        t0w = t0 + w * tw
        start = pl.multiple_of(jnp.maximum(t0w - WINDOW, 0), tw)
        kw = kw_ref[0, 0, :, pl.ds(start, wk)]
        vw_aug = jnp.concatenate([vw_ref[0, 0, :, pl.ds(start, wk)], ones_w], axis=0)
        case = jnp.minimum(t0w // tw, WINDOW // tw)
        qw = q3[:, w * tw:(w + 1) * tw].reshape(REP * tw, HD)
        sm = _dot(qw, kw).reshape(REP, tw, wk) + wbias_ref[case][None]
        e = jnp.exp(sm - jnp.max(sm, axis=-1, keepdims=True))
        acc_w = _dot_nt(e.reshape(REP * tw, wk).astype(BF16), vw_aug).reshape(REP, tw, 2 * HD)
        o_w.append(acc_w[..., :HD] / acc_w[..., HD:])
    o_w = o_w[0] if len(o_w) == 1 else jnp.concatenate(o_w, axis=1)

    gates = jax.nn.sigmoid(misc_ref[0])
    gate_lane = _iota(gates.shape, 1) - (N_SSM_HEADS + 3 * REP * g)
    o_c = o_c.reshape(REP, tq, HD)
    for r in range(REP):
        gh = [jnp.sum(jnp.where(gate_lane == 3 * r + br, gates, 0.0), axis=-1, keepdims=True)
              for br in range(3)]
        o = gh[0] * o_c[r] + gh[1] * o_s[r] + gh[2] * o_w[r]
        o_ref[0, :, r * HD:(r + 1) * HD] = o.astype(o_ref.dtype)


NSA_WINDOW_TILE = 128


def _window_bias(tw):
    c = np.arange(WINDOW // tw + 1)[:, None, None]
    d = np.arange(WINDOW + tw)[None, None, :] - c * tw - np.arange(tw)[None, :, None]
    return jnp.asarray(np.where((d <= 0) & (d > -WINDOW), 0.0, NEG).astype(np.float32))


def _nsa_prompt(qh, kch, vch, kvtb, ovl_t, e3, misc, n_blocks, tq, tk):
    bsz, nh, t, _ = qh.shape
    ncp = kch.shape[2]
    nsp = ovl_t.shape[0]
    kv_spec = lambda proj: pl.BlockSpec((1, 1, HD, t), lambda b, g, i: (b, N_KV * proj + g, 0, 0))
    cmp_spec = pl.BlockSpec((1, 1, ncp, HD), lambda b, g, i: (b, g, 0, 0))
    tw = min(tq, NSA_WINDOW_TILE)
    wbias = _window_bias(tw)
    return pl.pallas_call(
        functools.partial(_nsa_prompt_kernel, tq=tq, tk=tk, tw=tw, n_blocks=n_blocks),
        grid=(bsz, N_KV, t // tq),
        in_specs=[
            pl.BlockSpec((1, REP, tq, HD), lambda b, g, i: (b, g, i, 0)),
            cmp_spec, cmp_spec,
            kv_spec(2), kv_spec(3), kv_spec(4), kv_spec(5),
            pl.BlockSpec(ovl_t.shape, lambda b, g, i: (0, 0)),
            pl.BlockSpec(e3.shape, lambda b, g, i: (0, 0, 0)),
            pl.BlockSpec(wbias.shape, lambda b, g, i: (0, 0, 0)),
            pl.BlockSpec((1, tq, LANES), lambda b, g, i: (b, i, 0)),
        ],
        out_specs=pl.BlockSpec((1, tq, REP * HD), lambda b, g, i: (b, i, g)),
        out_shape=jax.ShapeDtypeStruct((bsz, t, nh * HD), BF16),
        scratch_shapes=[pltpu.VMEM((nsp, tq), F32)],
        compiler_params=_params(("parallel", "parallel", "arbitrary")),
        name="nsa_prompt",
    )(qh, kch, vch, kvtb, kvtb, kvtb, kvtb, ovl_t, e3, wbias, misc)


def _out_proj_kernel(y_ref, o_ref, h_ref, w_ref, g_ref, b_ref, out_ref, *, alpha):
    mixed = _dot(y_ref[...].astype(BF16), w_ref[0:D_SSM, :]) + _dot(o_ref[...], w_ref[D_SSM:, :])
    out_ref[...] = _layer_norm(alpha * h_ref[...] + mixed, g_ref[...], b_ref[...])


def _out_proj_ln(y, o, h, w, g, b, alpha):
    m, d = h.shape
    tm = min(512, m)
    row = lambda i: (i, 0)
    fixed = lambda i: (0, 0)
    return pl.pallas_call(
        functools.partial(_out_proj_kernel, alpha=alpha),
        grid=(m // tm,),
        in_specs=[
            pl.BlockSpec((tm, D_SSM), row),
            pl.BlockSpec((tm, D_NSA), row),
            pl.BlockSpec((tm, d), row),
            pl.BlockSpec(w.shape, fixed),
            pl.BlockSpec((1, d), fixed),
            pl.BlockSpec((1, d), fixed),
        ],
        out_specs=pl.BlockSpec((tm, d), row),
        out_shape=jax.ShapeDtypeStruct((m, d), F32),
        compiler_params=_params(("parallel",)),
        name="out_proj_ln",
    )(y, o, h, w, g, b)


def _row_group(shape):
    return _iota(shape, 0) // (shape[0] // N_KV)


def _cmp_attn_sample_kernel(q_ref, kc_ref, vc_ref, ssum_ref, oc_ref, psum_ref, *, past, s_len):
    q = q_ref[0]
    rows = q.shape[0]
    kc = kc_ref[0].astype(BF16)
    vc = vc_ref[0].astype(BF16)
    nc = kc.shape[0]
    t_row = (_iota((rows, nc), 0) // REP) % s_len
    cidx = _iota((rows, nc), 1)
    maskc = (cidx * CMP_STRIDE + (CMP_BLOCK - 1)) <= past + t_row
    s = _dot_nt(q, kc)
    sm = jnp.where(maskc, s, NEG)
    e = jnp.exp(sm - jnp.max(sm, axis=-1, keepdims=True))
    p = e / jnp.sum(e, axis=-1, keepdims=True) * jnp.where(maskc, 1.0, 0.0)
    o = _dot(p.astype(BF16), vc)
    own = (_iota(o.shape, 1) // HD) == _row_group(o.shape)
    oc_ref[0] = jnp.where(own, o, 0.0)
    psum_ref[0] = _dot(ssum_ref[...], p, HI)


def _cmp_attn_sample(qbd, kc, vc, ssum, past, s_len):
    dbz, rows, _ = qbd.shape
    nc = kc.shape[1]
    ng = ssum.shape[0]
    per_b = lambda b: (b, 0, 0)
    return pl.pallas_call(
        functools.partial(_cmp_attn_sample_kernel, past=past, s_len=s_len),
        grid=(dbz,),
        in_specs=[
            pl.BlockSpec((1, rows, KV_DIM), per_b),
            pl.BlockSpec((1, nc, KV_DIM), per_b),
            pl.BlockSpec((1, nc, KV_DIM), per_b),
            pl.BlockSpec(ssum.shape, lambda b: (0, 0)),
        ],
        out_specs=(pl.BlockSpec((1, rows, KV_DIM), per_b), pl.BlockSpec((1, ng, nc), per_b)),
        out_shape=(jax.ShapeDtypeStruct((dbz, rows, KV_DIM), F32),
                   jax.ShapeDtypeStruct((dbz, ng, nc), F32)),
        compiler_params=_params(("parallel",)),
        name="cmp_attn_sample",
    )(qbd, kc, vc, ssum)


def _select_sample_kernel(psum_ref, ovl_ref, selt_ref, score_buf, *, past, s_len, n_blocks):
    imp_t = _dot_nt(ovl_ref[...], psum_ref[...], HI)
    tpos = past + _iota(imp_t.shape, 1) % s_len
    selt_ref[...] = _select_blocks(imp_t, tpos, n_blocks, score_buf)


def _select_sample(psum_all, ovl_t, past, s_len, n_blocks):
    nsp = ovl_t.shape[0]
    cols = psum_all.shape[0]
    return pl.pallas_call(
        functools.partial(_select_sample_kernel, past=past, s_len=s_len, n_blocks=n_blocks),
        out_shape=jax.ShapeDtypeStruct((nsp, cols), F32),
        scratch_shapes=[pltpu.VMEM((nsp, cols), F32)],
        compiler_params=pltpu.CompilerParams(vmem_limit_bytes=VMEM_LIMIT),
        name="select_sample",
    )(psum_all, ovl_t)


def _online_update(state, sm, v_t):
    m, l, acc = state
    m_new = jnp.maximum(m, jnp.max(sm, axis=-1, keepdims=True))
    alpha = jnp.exp(m - m_new)
    e = jnp.exp(sm - m_new)
    l = alpha * l + jnp.sum(e, axis=-1, keepdims=True)
    return m_new, l, alpha * acc + _dot_nt(e.astype(BF16), v_t)


def _sel_win_sample_kernel(pt_ref, kcache_ref, vcache_ref, q_ref, sel_ref, e_ref, kt_ref, vt_ref,
                           kw_ref, vw_ref, kwn_ref, vwn_ref, oc_ref, gate_ref, o_ref, kbuf, vbuf, ksem, vsem,
                           *, n_pages, s_len, n_chunks, chunk):
    slot = _gather_pages(pt_ref, [(kcache_ref, kbuf, ksem), (vcache_ref, vbuf, vsem)], n_pages)

    q = q_ref[0]
    rows = q.shape[0]
    sel = sel_ref[0]
    sel_main = sel[:, :LANES]
    past = n_pages * PAGE
    t_col = (_iota((rows, 1), 0) // REP) % s_len
    pages_pc = chunk // PAGE

    def chunk_t(buf, c):
        return jnp.concatenate([buf[slot, (c * pages_pc + j) * PAGE:(c * pages_pc + j + 1) * PAGE, :]
                                for j in range(pages_pc)], axis=1).astype(BF16)

    state = (jnp.full((rows, 1), NEG, F32), jnp.zeros((rows, 1), F32), jnp.zeros((rows, KV_DIM), F32))
    for c in range(n_chunks):
        blk = _dot(sel_main, e_ref[c])
        kpos = c * chunk + _iota((rows, chunk), 1)
        ok = jnp.where(kpos <= past + t_col, blk, 0.0) > 0.5
        state = _online_update(state, jnp.where(ok, _dot(q, chunk_t(kbuf, c)), NEG), chunk_t(vbuf, c))
    lane_t = _iota((rows, kt_ref.shape[2]), 1)
    sel_new = sel[:, LANES:LANES + 1].astype(F32)
    ok = jnp.where(lane_t <= t_col, jnp.where(lane_t < s_len, sel_new, 0.0), 0.0) > 0.5
    _, l, acc = _online_update(state, jnp.where(ok, _dot(q, kt_ref[0].astype(BF16)), NEG),
                               vt_ref[0].astype(BF16))
    o_s = acc / l

    wb = kw_ref.shape[2]
    iw = _iota((rows, wb), 1)
    ok_c = iw > t_col + (wb - WINDOW)
    state = (jnp.full((rows, 1), NEG, F32), jnp.zeros((rows, 1), F32), jnp.zeros((rows, KV_DIM), F32))
    state = _online_update(state, jnp.where(ok_c, _dot(q, kw_ref[0].astype(BF16)), NEG), vw_ref[0].astype(BF16))
    ok_n = jnp.where(lane_t <= t_col, jnp.where(lane_t < s_len, 1.0, 0.0), 0.0) > 0.5
    _, l, acc = _online_update(state, jnp.where(ok_n, _dot(q, kwn_ref[0].astype(BF16)), NEG),
                               vwn_ref[0].astype(BF16))
    o_w = acc / l

    gates = jax.nn.sigmoid(gate_ref[0])
    o = gates[:, 0:1] * oc_ref[0] + gates[:, 1:2] * o_s + gates[:, 2:3] * o_w
    own = (_iota(o.shape, 1) // HD) == _row_group(o.shape)
    o = jnp.where(own, o, 0.0)
    o_ref[0] = o[:, :HD] + o[:, HD:]


def _sel_win_sample(page_table, kcache, vcache, qbd, selx, e4, ktail, vtail, kwin, vwin, kwnew, vwnew,
                    oc, graw, s_len):
    dbz, n_pages = page_table.shape
    rows = qbd.shape[1]
    n_chunks, _, chunk = e4.shape
    per_b = lambda b, pt: (b, 0, 0)
    blk = lambda a: pl.BlockSpec((1,) + a.shape[1:], per_b)
    grid_spec = pltpu.PrefetchScalarGridSpec(
        num_scalar_prefetch=1,
        grid=(dbz,),
        in_specs=[
            pl.BlockSpec(memory_space=pl.ANY),
            pl.BlockSpec(memory_space=pl.ANY),
            blk(qbd), blk(selx),
            pl.BlockSpec(e4.shape, lambda b, pt: (0, 0, 0)),
            blk(ktail), blk(vtail), blk(kwin), blk(vwin), blk(kwnew), blk(vwnew), blk(oc), blk(graw),
        ],
        out_specs=pl.BlockSpec((1, rows, HD), per_b),
        scratch_shapes=[
            pltpu.VMEM((2, n_pages * PAGE, PAGE), F32),
            pltpu.VMEM((2, n_pages * PAGE, PAGE), F32),
            pltpu.SemaphoreType.DMA((2,)),
            pltpu.SemaphoreType.DMA((2,)),
        ],
    )
    return pl.pallas_call(
        functools.partial(_sel_win_sample_kernel, n_pages=n_pages, s_len=s_len,
                          n_chunks=n_chunks, chunk=chunk),
        grid_spec=grid_spec,
        out_shape=jax.ShapeDtypeStruct((dbz, rows, HD), F32),
        compiler_params=_params(("arbitrary",)),
        name="sel_win_sample",
    )(page_table, kcache, vcache, qbd, selx, e4, ktail, vtail, kwin, vwin, kwnew, vwnew, oc, graw)


def _rope_tables(pos):
    half = HD // 2
    inv = ROPE_THETA ** (-jnp.arange(half, dtype=F32) / half)
    ang = pos.astype(F32)[:, None] * inv[None, :]
    cos = jnp.cos(ang)
    sin = jnp.sin(ang)
    reps = LANES // HD
    cos_l = jnp.tile(jnp.concatenate([cos, cos], axis=-1), (1, reps))
    sin_l = jnp.tile(jnp.concatenate([-sin, sin], axis=-1), (1, reps))
    return cos_l, sin_l, cos.T, sin.T


def _overlap_t(nc, ncp, ns, nsp):
    c_start = np.arange(ncp) * CMP_STRIDE
    s_start = np.arange(nsp) * SEL_BLOCK
    ovl = ((c_start[None, :] + CMP_BLOCK > s_start[:, None]) & (c_start[None, :] < s_start[:, None] + SEL_BLOCK))
    ovl = ovl & (np.arange(ncp)[None, :] < nc) & (np.arange(nsp)[:, None] < ns)
    return jnp.asarray(ovl.astype(np.float32))


def _expander(n_rows, n_chunks, chunk, dtype=BF16):
    key_block = (np.arange(n_chunks)[:, None] * chunk + np.arange(chunk)[None, :]) // SEL_BLOCK
    e = key_block[:, None, :] == np.arange(n_rows)[None, :, None]
    return jnp.asarray(e.astype(np.float32)).astype(dtype)


def _page_permutation():
    rows_pp = PAGE // CMP_STRIDE
    r = np.arange(PAGE)
    src = (r % rows_pp) * CMP_STRIDE + r // rows_pp
    return jnp.asarray((src[:, None] == np.arange(PAGE)[None, :]).astype(np.float32)).astype(BF16)


def _round_up(x, m):
    return -(-x // m) * m


def _compress_weights(w1, w2, pe):
    eye = jnp.eye(N_KV, dtype=F32)
    halves = []
    for half in range(2):
        wh = w1[half * CMP_STRIDE:(half + 1) * CMP_STRIDE]
        big = jnp.einsum("sdh,gk->sgdkh", wh, eye)
        halves.append(big.reshape(CMP_STRIDE * KV_DIM, N_KV * CMP_HIDDEN))
    w1big = jnp.concatenate(halves, axis=1).astype(BF16)
    w2big = jnp.einsum("hd,gk->ghkd", w2, eye).reshape(N_KV * CMP_HIDDEN, KV_DIM).astype(BF16)
    pe_rows = []
    for half in range(2):
        ph = pe[half * CMP_STRIDE:(half + 1) * CMP_STRIDE]
        row = jnp.broadcast_to(ph[:, None, :], (CMP_STRIDE, N_KV, HD)).reshape(1, CMP_STRIDE * KV_DIM)
        pe_rows.append(jnp.broadcast_to(row, (SUBLANES, CMP_STRIDE * KV_DIM)))
    pe2 = jnp.concatenate(pe_rows, axis=0).astype(BF16)
    return _page_permutation(), w1big, w2big, pe2


def _layer_weights(w, l):
    sizes = [D_SSM, CONV_DIM, N_SSM_HEADS, D_NSA, N_KV_PROJ * KV_DIM, N_HEADS * 3]
    offs = np.cumsum([0] + sizes)
    w_in = w["w_in"][l]
    seg = lambda i: w_in[:, offs[i]:offs[i + 1]]
    pad = LANES - sizes[2] - sizes[5]
    w_in_r = jnp.concatenate([seg(0), seg(1), seg(3), seg(2), seg(5),
                              jnp.zeros((w_in.shape[0], pad), F32)], axis=1).astype(BF16)
    bias_misc = jnp.concatenate([jnp.zeros((sizes[2],), F32), w["b_gate"][l], jnp.zeros((pad,), F32)])[None, :]
    lane_pad = lambda v: jnp.concatenate([v, jnp.zeros((LANES - v.shape[0],), F32)])[None, :]
    lw = {
        "w_in": w_in_r,
        "w_kv_t": seg(4).T.astype(BF16),
        "bias_misc": bias_misc,
        "conv_w": w["conv_w"][l],
        "conv_b_row": w["conv_b"][l][None, :],
        "dtb_row": lane_pad(w["dt_bias"][l]),
        "alog_row": lane_pad(w["a_log"][l]),
        "dtb_col": w["dt_bias"][l][:, None],
        "alog_col": w["a_log"][l][:, None],
        "dskip_row": jnp.repeat(w["d_skip"][l], HD)[None, :],
        "normw_row": w["ssm_norm_w"][l][None, :],
        "w_out": w["w_out"][l].astype(BF16),
    }
    lw["cmp_k"] = _compress_weights(w["cmp_k_w1"][l], w["cmp_k_w2"][l], w["cmp_k_pe"][l])
    lw["cmp_v"] = _compress_weights(w["cmp_v_w1"][l], w["cmp_v_w2"][l], w["cmp_v_pe"][l])
    for i in (1, 2, 3):
        lw[f"ln{i}"] = (w[f"ln{i}_g"][l][None, :], w[f"ln{i}_b"][l][None, :])
    for i in (1, 2):
        lw[f"ffn{i}"] = (w[f"ffn{i}_w_gate"][l].astype(BF16), w[f"ffn{i}_w_up"][l].astype(BF16),
                         w[f"ffn{i}_w_down"][l].astype(BF16))
    return lw


def _heads_major(x, bsz, t, n):
    return x.reshape(bsz, t, n, HD).transpose(0, 2, 1, 3)


def _mix_prompt(h, lw, bsz, t):
    m = bsz * t
    z, xbc, misc, qh, kvtb, *kvt = _in_proj(h, lw["w_in"], lw["w_kv_t"], lw["bias_misc"],
                                            _rope_tables(jnp.arange(t)), bsz, t)
    kv_rows = [a.reshape(bsz, N_KV, HD, t).transpose(0, 3, 1, 2) for a in kvt]

    chunk = min(128, t)
    misc3 = misc.reshape(bsz, t, LANES)
    xbc3 = xbc.reshape(bsz, t, CONV_DIM)
    y_ssd, h_ssm = _ssd(z.reshape(bsz, t, D_SSM), xbc3, misc3, misc3.transpose(0, 2, 1), lw, chunk)
    conv_state = xbc3[:, t - (CONV_TAPS - 1):]

    n_str = t // CMP_STRIDE
    nc = n_str - 1
    ns = -(-t // SEL_BLOCK)
    nsp = _round_up(ns, SUBLANES)
    kvtb4 = kvtb.reshape(bsz, N_KV_PROJ, KV_DIM, t)
    kc = _compress(kvtb4, 0, *lw["cmp_k"])
    vc = _compress(kvtb4, 1, *lw["cmp_v"])
    kch = _heads_major(kc, bsz, n_str, N_KV).astype(BF16)
    vch = _heads_major(vc, bsz, n_str, N_KV).astype(BF16)
    tq = min(512, t)
    tk = min(512, t)
    e3 = _expander(nsp, t // tk, tk)
    o = _nsa_prompt(qh, kch, vch, kvtb.reshape(bsz, N_KV_PROJ * N_KV, HD, t), _overlap_t(nc, n_str, ns, nsp),
                    e3, misc3, ns, tq, tk)

    wb = min(WINDOW, t)
    state = tuple(kv_rows[:4]) + (kv_rows[4][:, t - wb:], kv_rows[5][:, t - wb:], h_ssm, conv_state)
    return y_ssd.reshape(m, D_SSM), o.reshape(m, D_NSA), state


def _mix_sample(h, lw, l, dbz, s_len, caches, state_ssm, state_conv, page_table):
    cache_k_cmp, cache_v_cmp, cache_k_slc, cache_v_slc, cache_k_win, cache_v_win = caches
    m = dbz * s_len
    n_pages = page_table.shape[1]
    past = n_pages * PAGE
    pos = past + jnp.arange(s_len)
    z, xbc, misc, qh, _, *kvt = _in_proj(h, lw["w_in"], lw["w_kv_t"], lw["bias_misc"],
                                         _rope_tables(jnp.tile(pos, dbz)), 1, m)
    kvs = [a.reshape(KV_DIM, dbz, s_len) for a in kvt]
    new_rows = [kvs[i].transpose(1, 2, 0) for i in range(N_KV_PROJ)]
    new_t = [kvs[i].transpose(1, 0, 2) for i in range(N_KV_PROJ)]
    k_c, v_c, k_s, v_s, k_w, v_w = [r.reshape(dbz, s_len, N_KV, HD) for r in new_rows]

    misc3 = misc.reshape(dbz, s_len, LANES)
    xbc3 = xbc.reshape(dbz, s_len, CONV_DIM)
    xh = jnp.concatenate([state_conv[l], xbc3], axis=1)
    y_t, h_new = _ssd_sample(xh.transpose(1, 2, 0), z.reshape(dbz, s_len, D_SSM).transpose(1, 2, 0),
                             misc3[:, :, :N_SSM_HEADS].transpose(1, 2, 0), state_ssm[l].transpose(1, 2, 3, 0),
                             lw, dbz, s_len)
    y_ssd = y_t.transpose(2, 0, 1)
    h_ssm = h_new.transpose(3, 0, 1, 2)
    conv_state = xh[:, -(CONV_TAPS - 1):]

    width = CMP_STRIDE * KV_DIM
    n_pool = cache_k_cmp.shape[1]
    pages_t = lambda cache: cache[l].transpose(0, 2, 3, 1).reshape(n_pool, KV_DIM, PAGE)

    def tail_rows(new):
        flat = new.reshape(dbz, 1, s_len * KV_DIM)
        return jnp.pad(flat, ((0, 0), (0, SUBLANES - 1), (0, width - s_len * KV_DIM)))

    kc = _compress_paged(page_table, pages_t(cache_k_cmp), tail_rows(new_rows[0]), *lw["cmp_k"])
    vc = _compress_paged(page_table, pages_t(cache_v_cmp), tail_rows(new_rows[1]), *lw["cmp_v"])
    nc = kc.shape[1]
    total = past + s_len
    ns = -(-total // SEL_BLOCK)
    nsp = _round_up(ns, SUBLANES)

    rows = N_KV * s_len * REP
    qg = qh.reshape(N_KV, REP, dbz, s_len, HD).transpose(2, 0, 3, 1, 4)
    qbd = jnp.einsum("bgtrd,gk->bgtrkd", qg, jnp.eye(N_KV, dtype=BF16)).reshape(dbz, rows, KV_DIM)
    ng = N_KV * s_len
    ssum = jnp.asarray((np.arange(ng)[:, None] == np.arange(rows)[None, :] // REP).astype(np.float32))
    oc, psum = _cmp_attn_sample(qbd, kc, vc, ssum, past, s_len)
    selt = _select_sample(psum.reshape(dbz * ng, nc), _overlap_t(nc, nc, ns, nsp), past, s_len, ns)
    sel = selt.T.reshape(dbz, ng, 1, nsp)
    sel = jnp.broadcast_to(sel, (dbz, ng, REP, nsp)).reshape(dbz, rows, nsp)
    n_cached = past // SEL_BLOCK
    selx = jnp.concatenate([sel[:, :, :n_cached], jnp.zeros((dbz, rows, LANES - n_cached), F32),
                            sel[:, :, n_cached:n_cached + 1], jnp.zeros((dbz, rows, LANES - 1), F32)],
                           axis=-1).astype(BF16)
    chunk_k = min(2048, past)
    e4 = _expander(LANES, past // chunk_k, chunk_k)
    pad_lanes = lambda a: jnp.pad(a, ((0, 0), (0, 0), (0, LANES - s_len)))
    wb = cache_k_win.shape[2]
    win_t = lambda cache: cache[l].transpose(0, 2, 3, 1).reshape(dbz, KV_DIM, wb)
    gate = misc3[:, :, N_SSM_HEADS:N_SSM_HEADS + 3 * N_HEADS].reshape(dbz, s_len, N_KV, REP, 3)
    graw = jnp.pad(gate.transpose(0, 2, 1, 3, 4).reshape(dbz, rows, 3), ((0, 0), (0, 0), (0, LANES - 3)))
    o = _sel_win_sample(page_table, pages_t(cache_k_slc), pages_t(cache_v_slc), qbd, selx, e4,
                        pad_lanes(new_t[2]), pad_lanes(new_t[3]), win_t(cache_k_win), win_t(cache_v_win),
                        pad_lanes(new_t[4]), pad_lanes(new_t[5]), oc, graw, s_len)
    o_nsa = o.reshape(dbz, N_KV, s_len, REP, HD).transpose(0, 2, 1, 3, 4).reshape(m, D_NSA).astype(BF16)

    kw_full = jnp.concatenate([cache_k_win[l], k_w], axis=1)
    vw_full = jnp.concatenate([cache_v_win[l], v_w], axis=1)
    state = (k_c, v_c, k_s, v_s, kw_full[:, -wb:], vw_full[:, -wb:], h_ssm, conv_state)
    return y_ssd.reshape(m, D_SSM), o_nsa, state


def _layer(x, lw, alpha, mix_fn):
    h1 = _ffn_ln(x, *lw["ffn1"], *lw["ln1"], alpha)
    y_ssd, o_nsa, state = mix_fn(h1)
    h2 = _out_proj_ln(y_ssd, o_nsa, h1, lw["w_out"], *lw["ln2"], alpha)
    return _ffn_ln(h2, *lw["ffn2"], *lw["ln3"], alpha), state


def kernel(x_prompt, x_sample, cache_k_cmp, cache_v_cmp, cache_k_slc, cache_v_slc, cache_k_win, cache_v_win, state_ssm, state_conv, page_table, w_in, b_gate, conv_w, conv_b, dt_bias, a_log, d_skip, ssm_norm_w, cmp_k_w1, cmp_k_w2, cmp_k_pe, cmp_v_w1, cmp_v_w2, cmp_v_pe, w_out, ln1_g, ln1_b, ln2_g, ln2_b, ln3_g, ln3_b, ffn1_w_gate, ffn1_w_up, ffn1_w_down, ffn2_w_gate, ffn2_w_up, ffn2_w_down):
    weights = dict(w_in=w_in, b_gate=b_gate, conv_w=conv_w, conv_b=conv_b, dt_bias=dt_bias, a_log=a_log,
                   d_skip=d_skip, ssm_norm_w=ssm_norm_w, cmp_k_w1=cmp_k_w1, cmp_k_w2=cmp_k_w2,
                   cmp_k_pe=cmp_k_pe, cmp_v_w1=cmp_v_w1, cmp_v_w2=cmp_v_w2, cmp_v_pe=cmp_v_pe, w_out=w_out,
                   ln1_g=ln1_g, ln1_b=ln1_b, ln2_g=ln2_g, ln2_b=ln2_b, ln3_g=ln3_g, ln3_b=ln3_b,
                   ffn1_w_gate=ffn1_w_gate, ffn1_w_up=ffn1_w_up, ffn1_w_down=ffn1_w_down,
                   ffn2_w_gate=ffn2_w_gate, ffn2_w_up=ffn2_w_up, ffn2_w_down=ffn2_w_down)
    depth = w_in.shape[0]
    bsz, t, d = x_prompt.shape
    dbz, s_len, _ = x_sample.shape
    alpha = (2.0 * depth) ** 0.25
    caches = (cache_k_cmp, cache_v_cmp, cache_k_slc, cache_v_slc, cache_k_win, cache_v_win)
    y_p = x_prompt.reshape(bsz * t, d)
    y_s = x_sample.reshape(dbz * s_len, d)
    p_states, s_states = [], []
    for l in range(depth):
        lw = _layer_weights(weights, l)
        y_p, st_p = _layer(y_p, lw, alpha, lambda h: _mix_prompt(h, lw, bsz, t))
        y_s, st_s = _layer(y_s, lw, alpha, lambda h: _mix_sample(h, lw, l, dbz, s_len, caches, state_ssm,
                                                                 state_conv, page_table))
        p_states.append(st_p)
        s_states.append(st_s)
    p_st = [jnp.stack(a) for a in zip(*p_states)]
    s_st = [jnp.stack(a) for a in zip(*s_states)]
    outs = [y_p.reshape(bsz, t, d), y_s.reshape(dbz, s_len, d)]
    for p, s in zip(p_st[:6], s_st[:6]):
        outs += [p, s]
    outs += [p_st[6], s_st[6], p_st[7], s_st[7]]
    return tuple(outs)
```

```python
import functools
import math

import numpy as np
import jax
import jax.numpy as jnp
from jax import lax
from jax.experimental import pallas as pl
from jax.experimental.pallas import tpu as pltpu

F32 = jnp.float32
BF16 = jnp.bfloat16
HI = lax.Precision.HIGHEST

HD = 64
N_SSM_HEADS = 8
N_SSM_GROUPS = 2
D_SSM = 512
CONV_DIM = 768
CONV_TAPS = 4
N_HEADS = 8
N_KV = 2
REP = N_HEADS // N_KV
D_NSA = 512
KV_DIM = N_KV * HD
N_KV_PROJ = 6
CMP_STRIDE = 16
CMP_BLOCK = 32
CMP_HIDDEN = 128
SEL_BLOCK = 64
SEL_TOP_N = 16
WINDOW = 512
PAGE = 128
ROPE_THETA = 10000.0
LN_EPS = 1e-5
RMS_EPS = 1e-5
NEG = -1e30
FORCED_SCORE = 1e30
INVALID_SCORE = -1.0

LANES = 128
SUBLANES = 8
VMEM_LIMIT = 56 * 1024 * 1024

NT_DIMS = (((1,), (1,)), ((), ()))
TN_DIMS = (((0,), (0,)), ((), ()))


def _params(sem):
    return pltpu.CompilerParams(dimension_semantics=sem, vmem_limit_bytes=VMEM_LIMIT)


def _dot(a, b, precision=None):
    return jnp.dot(a, b, preferred_element_type=F32, precision=precision)


def _dot_nt(a, b, precision=None):
    return lax.dot_general(a, b, NT_DIMS, preferred_element_type=F32, precision=precision)


def _dot_tn(a, b, precision=None):
    return lax.dot_general(a, b, TN_DIMS, preferred_element_type=F32, precision=precision)


def _iota(shape, dim):
    return lax.broadcasted_iota(jnp.int32, shape, dim)


def _layer_norm(y, g, b):
    mu = jnp.mean(y, axis=-1, keepdims=True)
    yc = y - mu
    var = jnp.mean(yc * yc, axis=-1, keepdims=True)
    return yc * lax.rsqrt(var + LN_EPS) * g + b


def _ffn_ln_kernel(x_ref, wg_ref, wu_ref, wd_ref, g_ref, b_ref, o_ref, y_ref, *, alpha, n_split, skip_flush_matmuls):
    step = pl.program_id(0)

    @pl.when(step == 0)
    def _():
        y_ref[...] = jnp.zeros_like(y_ref)

    o_ref[...] = _layer_norm(y_ref[...], g_ref[...], b_ref[...])

    def matmuls():
        x = x_ref[...]
        xb = x.astype(BF16)
        tf = wg_ref.shape[1] // n_split
        acc = None
        for s in range(n_split):
            gate = _dot(xb, wg_ref[:, s * tf:(s + 1) * tf])
            up = _dot(xb, wu_ref[:, s * tf:(s + 1) * tf])
            act = (jax.nn.silu(gate) * up).astype(BF16)
            part = _dot(act, wd_ref[s * tf:(s + 1) * tf, :])
            acc = part if acc is None else acc + part
        y_ref[...] = alpha * x + 0.5 * acc

    if skip_flush_matmuls:
        pl.when(step < pl.num_programs(0) - 1)(matmuls)
    else:
        matmuls()


def _ffn_ln(x, wg, wu, wd, g, b, alpha):
    m, d = x.shape
    dff = wg.shape[1]
    tm = min(512, m)
    n_i = m // tm
    n_split = 2 if (dff // 2) % LANES == 0 else 1
    resident = lambda a: pl.BlockSpec(a.shape, lambda i: (0, 0), pipeline_mode=pl.Buffered(1))
    return pl.pallas_call(
        functools.partial(_ffn_ln_kernel, alpha=alpha, n_split=n_split, skip_flush_matmuls=n_i < 8),
        grid=(n_i + 1,),
        in_specs=[
            pl.BlockSpec((tm, d), lambda i: (jnp.minimum(i, n_i - 1), 0)),
            resident(wg), resident(wu), resident(wd),
            pl.BlockSpec((1, d), lambda i: (0, 0)),
            pl.BlockSpec((1, d), lambda i: (0, 0)),
        ],
        out_specs=pl.BlockSpec((tm, d), lambda i: (jnp.maximum(i - 1, 0), 0)),
        out_shape=jax.ShapeDtypeStruct((m, d), F32),
        scratch_shapes=[pltpu.VMEM((tm, d), F32)],
        compiler_params=_params(("arbitrary",)),
        name="ffn_ln",
    )(x, wg, wu, wd, g, b)


OFF_Z, OFF_XBC, OFF_Q, OFF_MISC, W_IN_COLS = 0, 512, 1280, 1792, 1920


def _in_proj_kernel(h_ref, w_ref, wkv_ref, bias_ref, cos_ref, sin_ref, cost_ref, sint_ref,
                    z_ref, xbc_ref, misc_ref, qh_ref, kvtb_ref, *kvt_refs, q_scale):
    hb = h_ref[...].astype(BF16)
    z_ref[...] = _dot(hb, w_ref[:, OFF_Z:OFF_XBC])
    xbc_ref[...] = _dot(hb, w_ref[:, OFF_XBC:OFF_Q])
    qm = _dot(hb, w_ref[:, OFF_Q:W_IN_COLS])
    misc_ref[...] = qm[:, OFF_MISC - OFF_Q:] + bias_ref[...]
    cos = cos_ref[...]
    sin = sin_ref[...]
    first_half = (_iota(cos.shape, 1) & (HD - 1)) < (HD // 2)
    for c in range(D_NSA // LANES):
        x = qm[:, c * LANES:(c + 1) * LANES]
        rot = jnp.where(first_half, pltpu.roll(x, LANES - HD // 2, 1), pltpu.roll(x, HD // 2, 1))
        q = ((x * cos + rot * sin) * q_scale).astype(BF16)
        for j in range(LANES // HD):
            qh_ref[0, c * (LANES // HD) + j] = q[:, j * HD:(j + 1) * HD]
    kvt = _dot_nt(wkv_ref[...], hb)
    cost = cost_ref[...]
    sint = sint_ref[...]
    half = HD // 2
    for i in range(N_KV_PROJ):
        blk = kvt[i * KV_DIM:(i + 1) * KV_DIM, :]
        if i % 2 == 0:
            parts = []
            for g in range(N_KV):
                x1 = blk[g * HD:g * HD + half, :]
                x2 = blk[g * HD + half:(g + 1) * HD, :]
                parts += [x1 * cost - x2 * sint, x2 * cost + x1 * sint]
            blk = jnp.concatenate(parts, axis=0)
        kvt_refs[i][0] = blk
        kvtb_ref[0, i * KV_DIM:(i + 1) * KV_DIM, :] = blk.astype(BF16)


def _in_proj(h, w, wkv, bias, tables, n_seq, t_seq):
    cos, sin, cost, sint = tables
    m, d = h.shape
    tm = min(512, t_seq)
    n_pos = t_seq // tm
    row = lambda i: (i, 0)
    fixed = lambda i: (0, 0)
    out_shapes = (
        jax.ShapeDtypeStruct((m, D_SSM), F32),
        jax.ShapeDtypeStruct((m, CONV_DIM), F32),
        jax.ShapeDtypeStruct((m, LANES), F32),
        jax.ShapeDtypeStruct((n_seq, N_HEADS, t_seq, HD), BF16),
        jax.ShapeDtypeStruct((n_seq, N_KV_PROJ * KV_DIM, t_seq), BF16),
    ) + (jax.ShapeDtypeStruct((n_seq, KV_DIM, t_seq), F32),) * N_KV_PROJ
    seq_t = lambda i: (i // n_pos, 0, i % n_pos)
    return pl.pallas_call(
        functools.partial(_in_proj_kernel, q_scale=HD ** -0.5),
        grid=(m // tm,),
        in_specs=[
            pl.BlockSpec((tm, d), row),
            pl.BlockSpec(w.shape, fixed),
            pl.BlockSpec(wkv.shape, fixed),
            pl.BlockSpec((1, LANES), fixed),
            pl.BlockSpec((tm, LANES), lambda i: (i % n_pos, 0)),
            pl.BlockSpec((tm, LANES), lambda i: (i % n_pos, 0)),
            pl.BlockSpec((HD // 2, tm), lambda i: (0, i % n_pos)),
            pl.BlockSpec((HD // 2, tm), lambda i: (0, i % n_pos)),
        ],
        out_specs=(
            pl.BlockSpec((tm, D_SSM), row),
            pl.BlockSpec((tm, CONV_DIM), row),
            pl.BlockSpec((tm, LANES), row),
            pl.BlockSpec((1, N_HEADS, tm, HD), lambda i: (i // n_pos, 0, i % n_pos, 0)),
            pl.BlockSpec((1, N_KV_PROJ * KV_DIM, tm), seq_t),
        ) + (pl.BlockSpec((1, KV_DIM, tm), seq_t),) * N_KV_PROJ,
        out_shape=out_shapes,
        compiler_params=_params(("parallel",)),
        name="in_proj",
    )(h, w, wkv, bias, cos, sin, cost, sint)


def _ssd_kernel(z_ref, xbc_ref, misc_ref, misct_ref, convw_ref, convb_ref,
                dtb_row_ref, alog_row_ref, dtb_col_ref, alog_col_ref, dskip_ref, normw_ref,
                y_ref, hout_ref, xbuf, hstate, ybuf, *, chunk, nseq):
    c = pl.program_id(1)

    @pl.when(c == 0)
    def _():
        xbuf[:, 0:SUBLANES, :] = jnp.zeros((nseq, SUBLANES, CONV_DIM), F32)
        hstate[...] = jnp.zeros_like(hstate)

    for i in range(nseq):
        _ssd_chunk(z_ref.at[i], xbc_ref.at[i], misc_ref.at[i], misct_ref.at[i], convw_ref, convb_ref,
                   dtb_row_ref, alog_row_ref, dtb_col_ref, alog_col_ref, dskip_ref, normw_ref,
                   y_ref.at[i], xbuf.at[i], hstate.at[i], ybuf.at[i], chunk)

    @pl.when(c == pl.num_programs(1) - 1)
    def _():
        hout_ref[...] = hstate[...]


def _ssd_chunk(z_ref, xbc_ref, misc_ref, misct_ref, convw_ref, convb_ref,
               dtb_row_ref, alog_row_ref, dtb_col_ref, alog_col_ref, dskip_ref, normw_ref,
               y_ref, xbuf, hstate, ybuf, chunk):
    L = chunk
    H = SUBLANES
    x = xbc_ref[...]
    xbuf[H:H + L, :] = x
    conv = convb_ref[...] + convw_ref[CONV_TAPS - 1:CONV_TAPS, :] * x
    for k in range(1, CONV_TAPS):
        conv = conv + convw_ref[CONV_TAPS - 1 - k:CONV_TAPS - k, :] * xbuf[H - k:H - k + L, :]
    xbuf[0:H, :] = x[L - H:L, :]
    xact = jax.nn.silu(conv)
    bm = xact[:, D_SSM:D_SSM + N_SSM_GROUPS * HD]
    cm = xact[:, D_SSM + N_SSM_GROUPS * HD:]

    dt = jax.nn.softplus(misc_ref[...] + dtb_row_ref[...])
    dtt = jax.nn.softplus(misct_ref[0:SUBLANES, :] + dtb_col_ref[...])
    a_row = -jnp.exp(alog_row_ref[...])
    a_col = -jnp.exp(alog_col_ref[...])
    lane_ok = _iota(dt.shape, 1) < N_SSM_HEADS
    da = jnp.where(lane_ok, dt * a_row, 0.0)
    dat = dtt * a_col
    ri = _iota((L, L), 0)
    ci = _iota((L, L), 1)
    causal = ri >= ci
    tri = jnp.where(causal, 1.0, 0.0).astype(F32)
    cum = _dot(tri, da, HI)
    cumt = _dot_nt(dat, tri, HI)
    cum_last = cum[L - 1:L, :]

    rep = N_SSM_HEADS // N_SSM_GROUPS
    for g in range(N_SSM_GROUPS):
        cm_g = cm[:, g * HD:(g + 1) * HD].astype(BF16)
        bm_g = bm[:, g * HD:(g + 1) * HD].astype(BF16)
        cb = _dot_nt(cm_g, bm_g)
        for r in range(rep):
            h = g * rep + r
            ch = cum[:, h:h + 1]
            seg = ch - cumt[h:h + 1, :]
            decay = jnp.where(causal, jnp.exp(jnp.where(causal, seg, 0.0)), 0.0)
            xs_h = xact[:, h * HD:(h + 1) * HD]
            xdt = xs_h * dt[:, h:h + 1]
            hprev = hstate[h]
            y_h = (_dot((cb * decay).astype(BF16), xdt.astype(BF16))
                   + _dot_nt(cm_g, hprev.astype(BF16)) * jnp.exp(ch))
            cl = cum_last[:, h:h + 1]
            tail = jnp.exp(cl - ch)
            hstate[h] = hprev * jnp.exp(cl) + _dot_tn((xdt * tail).astype(BF16), bm_g)
            ybuf[:, h * HD:(h + 1) * HD] = y_h

    xs = xact[:, :D_SSM]
    y = (ybuf[...] + dskip_ref[...] * xs) * jax.nn.silu(z_ref[...])
    y = y * lax.rsqrt(jnp.mean(y * y, axis=-1, keepdims=True) + RMS_EPS) * normw_ref[...]
    y_ref[...] = y


def _ssd(z, xbc, misc, misct, lw, chunk):
    bsz, t, _ = z.shape
    nc = t // chunk
    nseq = 1
    seq = lambda b, c: (b, c, 0)
    per_b4 = lambda b, c: (b, 0, 0, 0)
    fixed = lambda b, c: (0, 0)
    return pl.pallas_call(
        functools.partial(_ssd_kernel, chunk=chunk, nseq=nseq),
        grid=(bsz // nseq, nc),
        in_specs=[
            pl.BlockSpec((nseq, chunk, D_SSM), seq),
            pl.BlockSpec((nseq, chunk, CONV_DIM), seq),
            pl.BlockSpec((nseq, chunk, LANES), seq),
            pl.BlockSpec((nseq, LANES, chunk), lambda b, c: (b, 0, c)),
            pl.BlockSpec((CONV_TAPS, CONV_DIM), fixed),
            pl.BlockSpec((1, CONV_DIM), fixed),
            pl.BlockSpec((1, LANES), fixed),
            pl.BlockSpec((1, LANES), fixed),
            pl.BlockSpec((SUBLANES, 1), fixed),
            pl.BlockSpec((SUBLANES, 1), fixed),
            pl.BlockSpec((1, D_SSM), fixed),
            pl.BlockSpec((1, D_SSM), fixed),
        ],
        out_specs=(
            pl.BlockSpec((nseq, chunk, D_SSM), seq),
            pl.BlockSpec((nseq, N_SSM_HEADS, HD, HD), per_b4),
        ),
        out_shape=(
            jax.ShapeDtypeStruct((bsz, t, D_SSM), F32),
            jax.ShapeDtypeStruct((bsz, N_SSM_HEADS, HD, HD), F32),
        ),
        scratch_shapes=[
            pltpu.VMEM((nseq, SUBLANES + chunk, CONV_DIM), F32),
            pltpu.VMEM((nseq, N_SSM_HEADS, HD, HD), F32),
            pltpu.VMEM((nseq, chunk, D_SSM), F32),
        ],
        compiler_params=_params(("parallel", "arbitrary")),
        name="ssd",
    )(z, xbc, misc, misct, lw["conv_w"], lw["conv_b_row"], lw["dtb_row"], lw["alog_row"],
      lw["dtb_col"], lw["alog_col"], lw["dskip_row"], lw["normw_row"])


def _ssd_sample_kernel(xh_ref, z_ref, dtraw_ref, state_ref, convw_ref, convb_ref, dtb_ref, alog_ref,
                       dskip_ref, normw_ref, y_ref, hout_ref, xact, ypre, *, s_len):
    h = pl.program_id(0)

    @pl.when(h == 0)
    def _():
        for t in range(s_len):
            conv = convb_ref[...]
            for k in range(CONV_TAPS):
                conv = conv + convw_ref[k] * xh_ref[t + k]
            xact[t] = jax.nn.silu(conv)

    a = -jnp.exp(alog_ref[pl.ds(h, 1), :])
    dts = [jax.nn.softplus(dtraw_ref[t, pl.ds(h, 1), :] + dtb_ref[pl.ds(h, 1), :]) for t in range(s_len)]
    das = [jnp.exp(dt * a) for dt in dts]
    g = h // (N_SSM_HEADS // N_SSM_GROUPS)
    b_lo = pl.multiple_of(D_SSM + g * HD, HD)
    c_lo = pl.multiple_of(D_SSM + N_SSM_GROUPS * HD + g * HD, HD)
    bs = [xact[t, pl.ds(b_lo, HD), :] for t in range(s_len)]
    cs = [xact[t, pl.ds(c_lo, HD), :] for t in range(s_len)]

    def body(p, carry):
        hs = state_ref[0, p]
        row = h * HD + p
        for t in range(s_len):
            x = xact[t, pl.ds(row, 1), :]
            hs = hs * das[t] + (x * dts[t]) * bs[t]
            y = jnp.sum(cs[t] * hs, axis=0, keepdims=True)
            ypre[t, pl.ds(row, 1), :] = y + dskip_ref[pl.ds(row, 1), :] * x
        hout_ref[0, p] = hs
        return carry

    lax.fori_loop(0, HD, body, 0)

    @pl.when(h == pl.num_programs(0) - 1)
    def _():
        for t in range(s_len):
            y = ypre[t] * jax.nn.silu(z_ref[t])
            y = y * lax.rsqrt(jnp.mean(y * y, axis=0, keepdims=True) + RMS_EPS) * normw_ref[...]
            y_ref[t] = y


def _ssd_sample(xh, zt, dtraw, state, lw, dbz, s_len):
    lane_b = lambda v: jnp.broadcast_to(v[..., None], v.shape + (dbz,))
    full = lambda a: pl.BlockSpec(a.shape, lambda h: (0,) * a.ndim)
    args = (xh, zt, dtraw, state, lane_b(lw["conv_w"]), lane_b(lw["conv_b_row"][0]), lane_b(lw["dtb_col"][:, 0]),
            lane_b(lw["alog_col"][:, 0]), lane_b(lw["dskip_row"][0]), lane_b(lw["normw_row"][0]))
    state_spec = pl.BlockSpec((1, HD, HD, dbz), lambda h: (h, 0, 0, 0))
    in_specs = [full(a) for a in args]
    in_specs[3] = state_spec
    return pl.pallas_call(
        functools.partial(_ssd_sample_kernel, s_len=s_len),
        grid=(N_SSM_HEADS,),
        in_specs=in_specs,
        out_specs=(pl.BlockSpec((s_len, D_SSM, dbz), lambda h: (0, 0, 0)), state_spec),
        out_shape=(jax.ShapeDtypeStruct((s_len, D_SSM, dbz), F32),
                   jax.ShapeDtypeStruct(state.shape, F32)),
        scratch_shapes=[pltpu.VMEM((s_len, CONV_DIM, dbz), F32), pltpu.VMEM((s_len, D_SSM, dbz), F32)],
        compiler_params=_params(("arbitrary",)),
        name="ssd_sample",
    )(*args)


def _page_group(n_pages):
    return math.gcd(n_pages, 8)


def _compress_pages(get_pages, n_pages, tail_rows, pm_ref, w1_ref, w2_ref, pe_ref, x_buf, sec_buf):
    rows_pp = PAGE // CMP_STRIDE
    n = n_pages * rows_pp
    half = N_KV * CMP_HIDDEN
    group = _page_group(n_pages)

    for i in range(n_pages // group):
        perm = _dot_nt(pm_ref[...], get_pages(i))
        for j in range(group):
            r0 = (i * group + j) * rows_pp
            for s in range(CMP_STRIDE):
                x_buf[r0:r0 + rows_pp, s * KV_DIM:(s + 1) * KV_DIM] = (
                    perm[s * rows_pp:(s + 1) * rows_pp, j * KV_DIM:(j + 1) * KV_DIM])
    x_buf[n:n + SUBLANES, :] = tail_rows
    out = _dot(x_buf[...].astype(BF16), w1_ref[...])
    sec_buf[...] = out[:, half:]
    pe_out = _dot(pe_ref[...], w1_ref[...])
    pe_term = pe_out[0:1, :half] + pe_out[SUBLANES:SUBLANES + 1, half:]
    pre = out[0:n, :half] + sec_buf[1:n + 1, :] + pe_term
    return _dot(jax.nn.gelu(pre).astype(BF16), w2_ref[...])


def _compress_kernel(x_ref, pm_ref, w1_ref, w2_ref, pe_ref, o_ref, x_buf, sec_buf, *, n_pages):
    group = _page_group(n_pages)

    def get_pages(i):
        wide = x_ref[0, 0, :, i * group * PAGE:(i + 1) * group * PAGE]
        return jnp.concatenate([wide[:, j * PAGE:(j + 1) * PAGE] for j in range(group)], axis=0)

    tail_rows = jnp.zeros((SUBLANES, CMP_STRIDE * KV_DIM), F32)
    o_ref[0] = _compress_pages(get_pages, n_pages, tail_rows, pm_ref, w1_ref, w2_ref, pe_ref, x_buf, sec_buf)


def _compress(kvtb4, proj, pm, w1big, w2big, pe2):
    bsz, _, _, t = kvtb4.shape
    n_pages = t // PAGE
    n = t // CMP_STRIDE
    fixed = lambda b: (0, 0)
    return pl.pallas_call(
        functools.partial(_compress_kernel, n_pages=n_pages),
        grid=(bsz,),
        in_specs=[
            pl.BlockSpec((1, 1, KV_DIM, t), lambda b: (b, proj, 0, 0)),
            pl.BlockSpec(pm.shape, fixed),
            pl.BlockSpec(w1big.shape, fixed),
            pl.BlockSpec(w2big.shape, fixed),
            pl.BlockSpec(pe2.shape, fixed),
        ],
        out_specs=pl.BlockSpec((1, n, KV_DIM), lambda b: (b, 0, 0)),
        out_shape=jax.ShapeDtypeStruct((bsz, n, KV_DIM), F32),
        scratch_shapes=[pltpu.VMEM((n + SUBLANES, CMP_STRIDE * KV_DIM), F32),
                        pltpu.VMEM((n + SUBLANES, N_KV * CMP_HIDDEN), F32)],
        compiler_params=_params(("parallel",)),
        name="compress",
    )(kvtb4, pm, w1big, w2big, pe2)


def _page_copy(cache_ref, buf, sem, page, slot, p):
    return pltpu.make_async_copy(cache_ref.at[page], buf.at[slot, pl.ds(p * PAGE, PAGE)], sem.at[slot])


def _gather_pages(pt_ref, streams, n_pages):
    b = pl.program_id(0)
    slot = b % 2

    def issue(seq, into):
        def start(p, carry):
            page = pt_ref[seq, p]
            for cache_ref, buf, sem in streams:
                _page_copy(cache_ref, buf, sem, page, into, p).start()
            return carry
        lax.fori_loop(0, n_pages, start, 0, unroll=_page_group(n_pages))

    @pl.when(b == 0)
    def _():
        issue(b, slot)

    @pl.when(b + 1 < pl.num_programs(0))
    def _():
        issue(b + 1, 1 - slot)

    def wait(p, carry):
        for cache_ref, buf, sem in streams:
            _page_copy(cache_ref, buf, sem, 0, slot, p).wait()
        return carry

    lax.fori_loop(0, n_pages, wait, 0, unroll=_page_group(n_pages))
    return slot


def _compress_paged_kernel(pt_ref, cache_ref, tail_ref, pm_ref, w1_ref, w2_ref, pe_ref, o_ref,
                           buf, x_buf, sec_buf, sem, *, n_pages):
    slot = _gather_pages(pt_ref, [(cache_ref, buf, sem)], n_pages)
    rows = _page_group(n_pages) * PAGE
    get_pages = lambda i: buf[slot, i * rows:(i + 1) * rows, :].astype(BF16)
    o_ref[0] = _compress_pages(get_pages, n_pages, tail_ref[0], pm_ref, w1_ref, w2_ref, pe_ref, x_buf, sec_buf)


def _compress_paged(page_table, cache, tail, pm, w1big, w2big, pe2):
    dbz, n_pages = page_table.shape
    n = n_pages * (PAGE // CMP_STRIDE)
    fixed = lambda b, pt: (0, 0)
    grid_spec = pltpu.PrefetchScalarGridSpec(
        num_scalar_prefetch=1,
        grid=(dbz,),
        in_specs=[
            pl.BlockSpec(memory_space=pl.ANY),
            pl.BlockSpec((1,) + tail.shape[1:], lambda b, pt: (b, 0, 0)),
            pl.BlockSpec(pm.shape, fixed),
            pl.BlockSpec(w1big.shape, fixed),
            pl.BlockSpec(w2big.shape, fixed),
            pl.BlockSpec(pe2.shape, fixed),
        ],
        out_specs=pl.BlockSpec((1, n, KV_DIM), lambda b, pt: (b, 0, 0)),
        scratch_shapes=[
            pltpu.VMEM((2, n_pages * PAGE, PAGE), F32),
            pltpu.VMEM((n + SUBLANES, CMP_STRIDE * KV_DIM), F32),
            pltpu.VMEM((n + SUBLANES, N_KV * CMP_HIDDEN), F32),
            pltpu.SemaphoreType.DMA((2,)),
        ],
    )
    return pl.pallas_call(
        functools.partial(_compress_paged_kernel, n_pages=n_pages),
        grid_spec=grid_spec,
        out_shape=jax.ShapeDtypeStruct((dbz, n, KV_DIM), F32),
        compiler_params=_params(("arbitrary",)),
        name="compress_paged",
    )(page_table, cache, tail, pm, w1big, w2big, pe2)


RANK_UNROLL = 4


def _select_blocks(imp_t, tpos, n_blocks, score_buf, n_live=None):
    j = _iota(imp_t.shape, 0)
    valid = (j * SEL_BLOCK <= tpos) & (j < n_blocks)
    cur = tpos // SEL_BLOCK
    forced = (j == 0) | (j == cur) | (j == cur - 1)
    score = jnp.where(valid, jnp.where(forced, FORCED_SCORE, imp_t), INVALID_SCORE)
    score_buf[...] = score

    def body(i, cnt):
        row = score_buf[pl.ds(i, 1), :]
        above = jnp.where(row > score, 1.0, 0.0)
        tie = jnp.where(row == score, jnp.where(j > i, 1.0, 0.0), 0.0)
        return cnt + above + tie

    zero = jnp.zeros(imp_t.shape, F32)
    if n_live is None:
        cnt = lax.fori_loop(0, n_blocks, body, zero, unroll=RANK_UNROLL)
    else:
        def group(gi, cnt):
            for u in range(RANK_UNROLL):
                cnt = body(gi * RANK_UNROLL + u, cnt)
            return cnt
        assert imp_t.shape[0] % RANK_UNROLL == 0
        cnt = lax.fori_loop(0, (n_live + RANK_UNROLL - 1) // RANK_UNROLL, group, zero)
    return jnp.where(valid & (cnt < float(min(SEL_TOP_N, n_blocks))), 1.0, 0.0)


def _nsa_prompt_kernel(q_ref, kc_ref, vc_ref, ks_ref, vs_ref, kw_ref, vw_ref, ovl_ref, e_ref, wbias_ref, misc_ref,
                       o_ref, score_buf, *, tq, tk, tw, n_blocks):
    g = pl.program_id(1)
    t0 = pl.program_id(2) * tq
    rows = REP * tq
    q = q_ref[0].reshape(rows, HD)

    kc = kc_ref[0, 0]
    ncp = kc.shape[0]
    qpos_c = t0 + _iota((tq, ncp), 0)
    maskc = ((_iota((tq, ncp), 1) * CMP_STRIDE + (CMP_BLOCK - 1)) <= qpos_c)[None]
    s = _dot_nt(q, kc).reshape(REP, tq, ncp)
    sm = jnp.where(maskc, s, NEG)
    e = jnp.exp(sm - jnp.max(sm, axis=-1, keepdims=True))
    p = e / jnp.sum(e, axis=-1, keepdims=True) * jnp.where(maskc, 1.0, 0.0)
    o_c = _dot(p.reshape(rows, ncp).astype(BF16), vc_ref[0, 0])
    psum = jnp.sum(p, axis=0)

    imp_t = _dot_nt(ovl_ref[...], psum, HI)
    tpos = t0 + _iota(imp_t.shape, 1)
    n_live = jnp.minimum((t0 + tq - 1) // SEL_BLOCK + 1, n_blocks)
    selt = _select_blocks(imp_t, tpos, n_blocks, score_buf, n_live).astype(BF16)

    nsp = selt.shape[0]
    eye = jnp.where(_iota((nsp, nsp), 0) == _iota((nsp, nsp), 1), 1.0, 0.0).astype(BF16)
    bias = ((_dot_tn(selt, eye) - 1.0) * (-NEG)).astype(BF16)
    q_aug = jnp.concatenate([q, jnp.concatenate([bias] * REP, axis=0)], axis=1)
    ones_k = jnp.ones((HD, tk), BF16)
    qpos_k = t0 + _iota((tq, tk), 0)
    lane_k = _iota((tq, tk), 1)
    n_full = t0 // tk

    def step(kt, carry, diagonal):
        m, acc = carry
        k0 = pl.multiple_of(kt * tk, tk)
        k_aug = jnp.concatenate([ks_ref[0, 0, :, pl.ds(k0, tk)], e_ref[kt]], axis=0)
        v_aug = jnp.concatenate([vs_ref[0, 0, :, pl.ds(k0, tk)], ones_k], axis=0)
        s = _dot(q_aug, k_aug).reshape(REP, tq, tk)
        if diagonal:
            s = jnp.where((k0 + lane_k <= qpos_k)[None], s, NEG)
        m_new = jnp.maximum(m, jnp.max(s, axis=-1, keepdims=True))
        e = jnp.exp(s - m_new)
        pv = _dot_nt(e.reshape(rows, tk).astype(BF16), v_aug).reshape(REP, tq, 2 * HD)
        return m_new, jnp.exp(m - m_new) * acc + pv

    init = (jnp.full((REP, tq, 1), NEG, F32), jnp.zeros((REP, tq, 2 * HD), F32))
    carry = lax.fori_loop(0, n_full, functools.partial(step, diagonal=False), init)
    _, acc_s = step(n_full, carry, True)
    o_s = acc_s[..., :HD] / acc_s[..., HD:]

    wk = WINDOW + tw
    q3 = q.reshape(REP, tq, HD)
    ones_w = jnp.ones((HD, wk), BF16)
    o_w = []
    for w in range(tq // tw):
        t0w = t0 + w * tw
        start = pl.multiple_of(jnp.maximum(t0w - WINDOW, 0), tw)
        kw = kw_ref[0, 0, :, pl.ds(start, wk)]
        vw_aug = jnp.concatenate([vw_ref[0, 0, :, pl.ds(start, wk)], ones_w], axis=0)
        case = jnp.minimum(t0w // tw, WINDOW // tw)
        qw = q3[:, w * tw:(w + 1) * tw].reshape(REP * tw, HD)
        sm = _dot(qw, kw).reshape(REP, tw, wk) + wbias_ref[case][None]
        e = jnp.exp(sm - jnp.max(sm, axis=-1, keepdims=True))
        acc_w = _dot_nt(e.reshape(REP * tw, wk).astype(BF16), vw_aug).reshape(REP, tw, 2 * HD)
        o_w.append(acc_w[..., :HD] / acc_w[..., HD:])
    o_w = o_w[0] if len(o_w) == 1 else jnp.concatenate(o_w, axis=1)

    gates = jax.nn.sigmoid(misc_ref[0])
    gate_lane = _iota(gates.shape, 1) - (N_SSM_HEADS + 3 * REP * g)
    o_c = o_c.reshape(REP, tq, HD)
    for r in range(REP):
        gh = [jnp.sum(jnp.where(gate_lane == 3 * r + br, gates, 0.0), axis=-1, keepdims=True)
              for br in range(3)]
        o = gh[0] * o_c[r] + gh[1] * o_s[r] + gh[2] * o_w[r]
        o_ref[0, :, r * HD:(r + 1) * HD] = o.astype(o_ref.dtype)


NSA_WINDOW_TILE = 128


def _window_bias(tw):
    c = np.arange(WINDOW // tw + 1)[:, None, None]
    d = np.arange(WINDOW + tw)[None, None, :] - c * tw - np.arange(tw)[None, :, None]
    return jnp.asarray(np.where((d <= 0) & (d > -WINDOW), 0.0, NEG).astype(np.float32))


def _nsa_prompt(qh, kch, vch, kvtb, ovl_t, e3, misc, n_blocks, tq, tk):
    bsz, nh, t, _ = qh.shape
    ncp = kch.shape[2]
    nsp = ovl_t.shape[0]
    kv_spec = lambda proj: pl.BlockSpec((1, 1, HD, t), lambda b, g, i: (b, N_KV * proj + g, 0, 0))
    cmp_spec = pl.BlockSpec((1, 1, ncp, HD), lambda b, g, i: (b, g, 0, 0))
    tw = min(tq, NSA_WINDOW_TILE)
    wbias = _window_bias(tw)
    return pl.pallas_call(
        functools.partial(_nsa_prompt_kernel, tq=tq, tk=tk, tw=tw, n_blocks=n_blocks),
        grid=(bsz, N_KV, t // tq),
        in_specs=[
            pl.BlockSpec((1, REP, tq, HD), lambda b, g, i: (b, g, i, 0)),
            cmp_spec, cmp_spec,
            kv_spec(2), kv_spec(3), kv_spec(4), kv_spec(5),
            pl.BlockSpec(ovl_t.shape, lambda b, g, i: (0, 0)),
            pl.BlockSpec(e3.shape, lambda b, g, i: (0, 0, 0)),
            pl.BlockSpec(wbias.shape, lambda b, g, i: (0, 0, 0)),
            pl.BlockSpec((1, tq, LANES), lambda b, g, i: (b, i, 0)),
        ],
        out_specs=pl.BlockSpec((1, tq, REP * HD), lambda b, g, i: (b, i, g)),
        out_shape=jax.ShapeDtypeStruct((bsz, t, nh * HD), BF16),
        scratch_shapes=[pltpu.VMEM((nsp, tq), F32)],
        compiler_params=_params(("parallel", "parallel", "arbitrary")),
        name="nsa_prompt",
    )(qh, kch, vch, kvtb, kvtb, kvtb, kvtb, ovl_t, e3, wbias, misc)


def _out_proj_kernel(y_ref, o_ref, h_ref, w_ref, g_ref, b_ref, out_ref, *, alpha):
    mixed = _dot(y_ref[...].astype(BF16), w_ref[0:D_SSM, :]) + _dot(o_ref[...], w_ref[D_SSM:, :])
    out_ref[...] = _layer_norm(alpha * h_ref[...] + mixed, g_ref[...], b_ref[...])


def _out_proj_ln(y, o, h, w, g, b, alpha):
    m, d = h.shape
    tm = min(512, m)
    row = lambda i: (i, 0)
    fixed = lambda i: (0, 0)
    return pl.pallas_call(
        functools.partial(_out_proj_kernel, alpha=alpha),
        grid=(m // tm,),
        in_specs=[
            pl.BlockSpec((tm, D_SSM), row),
            pl.BlockSpec((tm, D_NSA), row),
            pl.BlockSpec((tm, d), row),
            pl.BlockSpec(w.shape, fixed),
            pl.BlockSpec((1, d), fixed),
            pl.BlockSpec((1, d), fixed),
        ],
        out_specs=pl.BlockSpec((tm, d), row),
        out_shape=jax.ShapeDtypeStruct((m, d), F32),
        compiler_params=_params(("parallel",)),
        name="out_proj_ln",
    )(y, o, h, w, g, b)


def _row_group(shape):
    return _iota(shape, 0) // (shape[0] // N_KV)


def _cmp_attn_sample_kernel(q_ref, kc_ref, vc_ref, ssum_ref, oc_ref, psum_ref, *, past, s_len):
    nb, rows, _ = q_ref.shape
    nc = kc_ref.shape[1]
    t_row = (_iota((rows, nc), 0) // REP) % s_len
    cidx = _iota((rows, nc), 1)
    maskc = (cidx * CMP_STRIDE + (CMP_BLOCK - 1)) <= past + t_row
    maskf = jnp.where(maskc, 1.0, 0.0)
    own = (_iota((rows, KV_DIM), 1) // HD) == _row_group((rows, KV_DIM))
    for i in range(nb):
        s = _dot_nt(q_ref[i], kc_ref[i].astype(BF16))
        sm = jnp.where(maskc, s, NEG)
        e = jnp.exp(sm - jnp.max(sm, axis=-1, keepdims=True))
        p = e / jnp.sum(e, axis=-1, keepdims=True) * maskf
        o = _dot(p.astype(BF16), vc_ref[i].astype(BF16))
        oc_ref[i] = jnp.where(own, o, 0.0)
        psum_ref[i] = _dot(ssum_ref[...], p, HI)


def _cmp_attn_sample(qbd, kc, vc, ssum, past, s_len):
    dbz, rows, _ = qbd.shape
    nc = kc.shape[1]
    ng = ssum.shape[0]
    nb = math.gcd(dbz, 4)
    per_b = lambda b: (b, 0, 0)
    return pl.pallas_call(
        functools.partial(_cmp_attn_sample_kernel, past=past, s_len=s_len),
        grid=(dbz // nb,),
        in_specs=[
            pl.BlockSpec((nb, rows, KV_DIM), per_b),
            pl.BlockSpec((nb, nc, KV_DIM), per_b),
            pl.BlockSpec((nb, nc, KV_DIM), per_b),
            pl.BlockSpec(ssum.shape, lambda b: (0, 0)),
        ],
        out_specs=(pl.BlockSpec((nb, rows, KV_DIM), per_b), pl.BlockSpec((nb, ng, nc), per_b)),
        out_shape=(jax.ShapeDtypeStruct((dbz, rows, KV_DIM), F32),
                   jax.ShapeDtypeStruct((dbz, ng, nc), F32)),
        compiler_params=_params(("parallel",)),
        name="cmp_attn_sample",
    )(qbd, kc, vc, ssum)


def _select_sample_kernel(psum_ref, ovl_ref, selt_ref, score_buf, *, past, s_len, n_blocks):
    imp_t = _dot_nt(ovl_ref[...], psum_ref[...], HI)
    tpos = past + _iota(imp_t.shape, 1) % s_len
    selt_ref[...] = _select_blocks(imp_t, tpos, n_blocks, score_buf)


def _select_sample(psum_all, ovl_t, past, s_len, n_blocks):
    nsp = ovl_t.shape[0]
    cols = psum_all.shape[0]
    return pl.pallas_call(
        functools.partial(_select_sample_kernel, past=past, s_len=s_len, n_blocks=n_blocks),
        out_shape=jax.ShapeDtypeStruct((nsp, cols), F32),
        scratch_shapes=[pltpu.VMEM((nsp, cols), F32)],
        compiler_params=pltpu.CompilerParams(vmem_limit_bytes=VMEM_LIMIT),
        name="select_sample",
    )(psum_all, ovl_t)


def _online_update(state, sm, v_t):
    m, l, acc = state
    m_new = jnp.maximum(m, jnp.max(sm, axis=-1, keepdims=True))
    alpha = jnp.exp(m - m_new)
    e = jnp.exp(sm - m_new)
    l = alpha * l + jnp.sum(e, axis=-1, keepdims=True)
    return m_new, l, alpha * acc + _dot_nt(e.astype(BF16), v_t)


def _sel_win_sample_kernel(pt_ref, kcache_ref, vcache_ref, q_ref, sel_ref, e_ref, kt_ref, vt_ref,
                           kw_ref, vw_ref, kwn_ref, vwn_ref, oc_ref, gate_ref, o_ref, kbuf, vbuf, ksem, vsem,
                           *, n_pages, s_len, n_chunks, chunk):
    slot = _gather_pages(pt_ref, [(kcache_ref, kbuf, ksem), (vcache_ref, vbuf, vsem)], n_pages)

    q = q_ref[0]
    rows = q.shape[0]
    sel = sel_ref[0]
    sel_main = sel[:, :LANES]
    past = n_pages * PAGE
    t_col = (_iota((rows, 1), 0) // REP) % s_len
    pages_pc = chunk // PAGE

    def chunk_t(buf, c):
        return jnp.concatenate([buf[slot, (c * pages_pc + j) * PAGE:(c * pages_pc + j + 1) * PAGE, :]
                                for j in range(pages_pc)], axis=1).astype(BF16)

    state = (jnp.full((rows, 1), NEG, F32), jnp.zeros((rows, 1), F32), jnp.zeros((rows, KV_DIM), F32))
    for c in range(n_chunks):
        blk = _dot(sel_main, e_ref[c])
        kpos = c * chunk + _iota((rows, chunk), 1)
        ok = jnp.where(kpos <= past + t_col, blk, 0.0) > 0.5
        state = _online_update(state, jnp.where(ok, _dot(q, chunk_t(kbuf, c)), NEG), chunk_t(vbuf, c))
    lane_t = _iota((rows, kt_ref.shape[2]), 1)
    sel_new = sel[:, LANES:LANES + 1].astype(F32)
    ok = jnp.where(lane_t <= t_col, jnp.where(lane_t < s_len, sel_new, 0.0), 0.0) > 0.5
    _, l, acc = _online_update(state, jnp.where(ok, _dot(q, kt_ref[0].astype(BF16)), NEG),
                               vt_ref[0].astype(BF16))
    o_s = acc / l

    wb = kw_ref.shape[2]
    iw = _iota((rows, wb), 1)
    ok_c = iw > t_col + (wb - WINDOW)
    state = (jnp.full((rows, 1), NEG, F32), jnp.zeros((rows, 1), F32), jnp.zeros((rows, KV_DIM), F32))
    state = _online_update(state, jnp.where(ok_c, _dot(q, kw_ref[0].astype(BF16)), NEG), vw_ref[0].astype(BF16))
    ok_n = jnp.where(lane_t <= t_col, jnp.where(lane_t < s_len, 1.0, 0.0), 0.0) > 0.5
    _, l, acc = _online_update(state, jnp.where(ok_n, _dot(q, kwn_ref[0].astype(BF16)), NEG),
                               vwn_ref[0].astype(BF16))
    o_w = acc / l

    gates = jax.nn.sigmoid(gate_ref[0])
    o = gates[:, 0:1] * oc_ref[0] + gates[:, 1:2] * o_s + gates[:, 2:3] * o_w
    own = (_iota(o.shape, 1) // HD) == _row_group(o.shape)
    o = jnp.where(own, o, 0.0)
    o_ref[0] = o[:, :HD] + o[:, HD:]


def _sel_win_sample(page_table, kcache, vcache, qbd, selx, e4, ktail, vtail, kwin, vwin, kwnew, vwnew,
                    oc, graw, s_len):
    dbz, n_pages = page_table.shape
    rows = qbd.shape[1]
    n_chunks, _, chunk = e4.shape
    per_b = lambda b, pt: (b, 0, 0)
    blk = lambda a: pl.BlockSpec((1,) + a.shape[1:], per_b)
    grid_spec = pltpu.PrefetchScalarGridSpec(
        num_scalar_prefetch=1,
        grid=(dbz,),
        in_specs=[
            pl.BlockSpec(memory_space=pl.ANY),
            pl.BlockSpec(memory_space=pl.ANY),
            blk(qbd), blk(selx),
            pl.BlockSpec(e4.shape, lambda b, pt: (0, 0, 0)),
            blk(ktail), blk(vtail), blk(kwin), blk(vwin), blk(kwnew), blk(vwnew), blk(oc), blk(graw),
        ],
        out_specs=pl.BlockSpec((1, rows, HD), per_b),
        scratch_shapes=[
            pltpu.VMEM((2, n_pages * PAGE, PAGE), F32),
            pltpu.VMEM((2, n_pages * PAGE, PAGE), F32),
            pltpu.SemaphoreType.DMA((2,)),
            pltpu.SemaphoreType.DMA((2,)),
        ],
    )
    return pl.pallas_call(
        functools.partial(_sel_win_sample_kernel, n_pages=n_pages, s_len=s_len,
                          n_chunks=n_chunks, chunk=chunk),
        grid_spec=grid_spec,
        out_shape=jax.ShapeDtypeStruct((dbz, rows, HD), F32),
        compiler_params=_params(("arbitrary",)),
        name="sel_win_sample",
    )(page_table, kcache, vcache, qbd, selx, e4, ktail, vtail, kwin, vwin, kwnew, vwnew, oc, graw)


def _rope_tables(pos):
    half = HD // 2
    inv = ROPE_THETA ** (-jnp.arange(half, dtype=F32) / half)
    ang = pos.astype(F32)[:, None] * inv[None, :]
    cos = jnp.cos(ang)
    sin = jnp.sin(ang)
    reps = LANES // HD
    cos_l = jnp.tile(jnp.concatenate([cos, cos], axis=-1), (1, reps))
    sin_l = jnp.tile(jnp.concatenate([-sin, sin], axis=-1), (1, reps))
    return cos_l, sin_l, cos.T, sin.T


def _overlap_t(nc, ncp, ns, nsp):
    c_start = np.arange(ncp) * CMP_STRIDE
    s_start = np.arange(nsp) * SEL_BLOCK
    ovl = ((c_start[None, :] + CMP_BLOCK > s_start[:, None]) & (c_start[None, :] < s_start[:, None] + SEL_BLOCK))
    ovl = ovl & (np.arange(ncp)[None, :] < nc) & (np.arange(nsp)[:, None] < ns)
    return jnp.asarray(ovl.astype(np.float32))


def _expander(n_rows, n_chunks, chunk, dtype=BF16):
    key_block = (np.arange(n_chunks)[:, None] * chunk + np.arange(chunk)[None, :]) // SEL_BLOCK
    e = key_block[:, None, :] == np.arange(n_rows)[None, :, None]
    return jnp.asarray(e.astype(np.float32)).astype(dtype)


def _page_permutation():
    rows_pp = PAGE // CMP_STRIDE
    r = np.arange(PAGE)
    src = (r % rows_pp) * CMP_STRIDE + r // rows_pp
    return jnp.asarray((src[:, None] == np.arange(PAGE)[None, :]).astype(np.float32)).astype(BF16)


def _round_up(x, m):
    return -(-x // m) * m


def _compress_weights(w1, w2, pe):
    eye = jnp.eye(N_KV, dtype=F32)
    halves = []
    for half in range(2):
        wh = w1[half * CMP_STRIDE:(half + 1) * CMP_STRIDE]
        big = jnp.einsum("sdh,gk->sgdkh", wh, eye)
        halves.append(big.reshape(CMP_STRIDE * KV_DIM, N_KV * CMP_HIDDEN))
    w1big = jnp.concatenate(halves, axis=1).astype(BF16)
    w2big = jnp.einsum("hd,gk->ghkd", w2, eye).reshape(N_KV * CMP_HIDDEN, KV_DIM).astype(BF16)
    pe_rows = []
    for half in range(2):
        ph = pe[half * CMP_STRIDE:(half + 1) * CMP_STRIDE]
        row = jnp.broadcast_to(ph[:, None, :], (CMP_STRIDE, N_KV, HD)).reshape(1, CMP_STRIDE * KV_DIM)
        pe_rows.append(jnp.broadcast_to(row, (SUBLANES, CMP_STRIDE * KV_DIM)))
    pe2 = jnp.concatenate(pe_rows, axis=0).astype(BF16)
    return _page_permutation(), w1big, w2big, pe2


def _layer_weights(w, l):
    sizes = [D_SSM, CONV_DIM, N_SSM_HEADS, D_NSA, N_KV_PROJ * KV_DIM, N_HEADS * 3]
    offs = np.cumsum([0] + sizes)
    w_in = w["w_in"][l]
    seg = lambda i: w_in[:, offs[i]:offs[i + 1]]
    pad = LANES - sizes[2] - sizes[5]
    w_in_r = jnp.concatenate([seg(0), seg(1), seg(3), seg(2), seg(5),
                              jnp.zeros((w_in.shape[0], pad), F32)], axis=1).astype(BF16)
    bias_misc = jnp.concatenate([jnp.zeros((sizes[2],), F32), w["b_gate"][l], jnp.zeros((pad,), F32)])[None, :]
    lane_pad = lambda v: jnp.concatenate([v, jnp.zeros((LANES - v.shape[0],), F32)])[None, :]
    lw = {
        "w_in": w_in_r,
        "w_kv_t": seg(4).T.astype(BF16),
        "bias_misc": bias_misc,
        "conv_w": w["conv_w"][l],
        "conv_b_row": w["conv_b"][l][None, :],
        "dtb_row": lane_pad(w["dt_bias"][l]),
        "alog_row": lane_pad(w["a_log"][l]),
        "dtb_col": w["dt_bias"][l][:, None],
        "alog_col": w["a_log"][l][:, None],
        "dskip_row": jnp.repeat(w["d_skip"][l], HD)[None, :],
        "normw_row": w["ssm_norm_w"][l][None, :],
        "w_out": w["w_out"][l].astype(BF16),
    }
    lw["cmp_k"] = _compress_weights(w["cmp_k_w1"][l], w["cmp_k_w2"][l], w["cmp_k_pe"][l])
    lw["cmp_v"] = _compress_weights(w["cmp_v_w1"][l], w["cmp_v_w2"][l], w["cmp_v_pe"][l])
    for i in (1, 2, 3):
        lw[f"ln{i}"] = (w[f"ln{i}_g"][l][None, :], w[f"ln{i}_b"][l][None, :])
    for i in (1, 2):
        lw[f"ffn{i}"] = (w[f"ffn{i}_w_gate"][l].astype(BF16), w[f"ffn{i}_w_up"][l].astype(BF16),
                         w[f"ffn{i}_w_down"][l].astype(BF16))
    return lw


def _heads_major(x, bsz, t, n):
    return x.reshape(bsz, t, n, HD).transpose(0, 2, 1, 3)


def _mix_prompt(h, lw, bsz, t):
    m = bsz * t
    z, xbc, misc, qh, kvtb, *kvt = _in_proj(h, lw["w_in"], lw["w_kv_t"], lw["bias_misc"],
                                            _rope_tables(jnp.arange(t)), bsz, t)
    kv_rows = [a.reshape(bsz, N_KV, HD, t).transpose(0, 3, 1, 2) for a in kvt]

    chunk = min(128, t)
    misc3 = misc.reshape(bsz, t, LANES)
    xbc3 = xbc.reshape(bsz, t, CONV_DIM)
    y_ssd, h_ssm = _ssd(z.reshape(bsz, t, D_SSM), xbc3, misc3, misc3.transpose(0, 2, 1), lw, chunk)
    conv_state = xbc3[:, t - (CONV_TAPS - 1):]

    n_str = t // CMP_STRIDE
    nc = n_str - 1
    ns = -(-t // SEL_BLOCK)
    nsp = _round_up(ns, SUBLANES)
    kvtb4 = kvtb.reshape(bsz, N_KV_PROJ, KV_DIM, t)
    kc = _compress(kvtb4, 0, *lw["cmp_k"])
    vc = _compress(kvtb4, 1, *lw["cmp_v"])
    kch = _heads_major(kc, bsz, n_str, N_KV).astype(BF16)
    vch = _heads_major(vc, bsz, n_str, N_KV).astype(BF16)
    tq = min(512, t)
    tk = min(512, t)
    e3 = _expander(nsp, t // tk, tk)
    o = _nsa_prompt(qh, kch, vch, kvtb.reshape(bsz, N_KV_PROJ * N_KV, HD, t), _overlap_t(nc, n_str, ns, nsp),
                    e3, misc3, ns, tq, tk)

    wb = min(WINDOW, t)
    state = tuple(kv_rows[:4]) + (kv_rows[4][:, t - wb:], kv_rows[5][:, t - wb:], h_ssm, conv_state)
    return y_ssd.reshape(m, D_SSM), o.reshape(m, D_NSA), state


def _mix_sample(h, lw, l, dbz, s_len, caches, state_ssm, state_conv, page_table):
    cache_k_cmp, cache_v_cmp, cache_k_slc, cache_v_slc, cache_k_win, cache_v_win = caches
    m = dbz * s_len
    n_pages = page_table.shape[1]
    past = n_pages * PAGE
    pos = past + jnp.arange(s_len)
    z, xbc, misc, qh, _, *kvt = _in_proj(h, lw["w_in"], lw["w_kv_t"], lw["bias_misc"],
                                         _rope_tables(jnp.tile(pos, dbz)), 1, m)
    kvs = [a.reshape(KV_DIM, dbz, s_len) for a in kvt]
    new_rows = [kvs[i].transpose(1, 2, 0) for i in range(N_KV_PROJ)]
    new_t = [kvs[i].transpose(1, 0, 2) for i in range(N_KV_PROJ)]
    k_c, v_c, k_s, v_s, k_w, v_w = [r.reshape(dbz, s_len, N_KV, HD) for r in new_rows]

    misc3 = misc.reshape(dbz, s_len, LANES)
    xbc3 = xbc.reshape(dbz, s_len, CONV_DIM)
    xh = jnp.concatenate([state_conv[l], xbc3], axis=1)
    y_t, h_new = _ssd_sample(xh.transpose(1, 2, 0), z.reshape(dbz, s_len, D_SSM).transpose(1, 2, 0),
                             misc3[:, :, :N_SSM_HEADS].transpose(1, 2, 0), state_ssm[l].transpose(1, 2, 3, 0),
                             lw, dbz, s_len)
    y_ssd = y_t.transpose(2, 0, 1)
    h_ssm = h_new.transpose(3, 0, 1, 2)
    conv_state = xh[:, -(CONV_TAPS - 1):]

    width = CMP_STRIDE * KV_DIM
    n_pool = cache_k_cmp.shape[1]
    pages_t = lambda cache: cache[l].transpose(0, 2, 3, 1).reshape(n_pool, KV_DIM, PAGE)

    def tail_rows(new):
        flat = new.reshape(dbz, 1, s_len * KV_DIM)
        return jnp.pad(flat, ((0, 0), (0, SUBLANES - 1), (0, width - s_len * KV_DIM)))

    kc = _compress_paged(page_table, pages_t(cache_k_cmp), tail_rows(new_rows[0]), *lw["cmp_k"])
    vc = _compress_paged(page_table, pages_t(cache_v_cmp), tail_rows(new_rows[1]), *lw["cmp_v"])
    nc = kc.shape[1]
    total = past + s_len
    ns = -(-total // SEL_BLOCK)
    nsp = _round_up(ns, SUBLANES)

    rows = N_KV * s_len * REP
    qg = qh.reshape(N_KV, REP, dbz, s_len, HD).transpose(2, 0, 3, 1, 4)
    qbd = jnp.einsum("bgtrd,gk->bgtrkd", qg, jnp.eye(N_KV, dtype=BF16)).reshape(dbz, rows, KV_DIM)
    ng = N_KV * s_len
    ssum = jnp.asarray((np.arange(ng)[:, None] == np.arange(rows)[None, :] // REP).astype(np.float32))
    oc, psum = _cmp_attn_sample(qbd, kc, vc, ssum, past, s_len)
    selt = _select_sample(psum.reshape(dbz * ng, nc), _overlap_t(nc, nc, ns, nsp), past, s_len, ns)
    sel = selt.T.reshape(dbz, ng, 1, nsp)
    sel = jnp.broadcast_to(sel, (dbz, ng, REP, nsp)).reshape(dbz, rows, nsp)
    n_cached = past // SEL_BLOCK
    selx = jnp.concatenate([sel[:, :, :n_cached], jnp.zeros((dbz, rows, LANES - n_cached), F32),
                            sel[:, :, n_cached:n_cached + 1], jnp.zeros((dbz, rows, LANES - 1), F32)],
                           axis=-1).astype(BF16)
    chunk_k = min(2048, past)
    e4 = _expander(LANES, past // chunk_k, chunk_k)
    pad_lanes = lambda a: jnp.pad(a, ((0, 0), (0, 0), (0, LANES - s_len)))
    wb = cache_k_win.shape[2]
    win_t = lambda cache: cache[l].transpose(0, 2, 3, 1).reshape(dbz, KV_DIM, wb)
    gate = misc3[:, :, N_SSM_HEADS:N_SSM_HEADS + 3 * N_HEADS].reshape(dbz, s_len, N_KV, REP, 3)
    graw = jnp.pad(gate.transpose(0, 2, 1, 3, 4).reshape(dbz, rows, 3), ((0, 0), (0, 0), (0, LANES - 3)))
    o = _sel_win_sample(page_table, pages_t(cache_k_slc), pages_t(cache_v_slc), qbd, selx, e4,
                        pad_lanes(new_t[2]), pad_lanes(new_t[3]), win_t(cache_k_win), win_t(cache_v_win),
                        pad_lanes(new_t[4]), pad_lanes(new_t[5]), oc, graw, s_len)
    o_nsa = o.reshape(dbz, N_KV, s_len, REP, HD).transpose(0, 2, 1, 3, 4).reshape(m, D_NSA).astype(BF16)

    kw_full = jnp.concatenate([cache_k_win[l], k_w], axis=1)
    vw_full = jnp.concatenate([cache_v_win[l], v_w], axis=1)
    state = (k_c, v_c, k_s, v_s, kw_full[:, -wb:], vw_full[:, -wb:], h_ssm, conv_state)
    return y_ssd.reshape(m, D_SSM), o_nsa, state


def _layer(x, lw, alpha, mix_fn):
    h1 = _ffn_ln(x, *lw["ffn1"], *lw["ln1"], alpha)
    y_ssd, o_nsa, state = mix_fn(h1)
    h2 = _out_proj_ln(y_ssd, o_nsa, h1, lw["w_out"], *lw["ln2"], alpha)
    return _ffn_ln(h2, *lw["ffn2"], *lw["ln3"], alpha), state


def kernel(x_prompt, x_sample, cache_k_cmp, cache_v_cmp, cache_k_slc, cache_v_slc, cache_k_win, cache_v_win, state_ssm, state_conv, page_table, w_in, b_gate, conv_w, conv_b, dt_bias, a_log, d_skip, ssm_norm_w, cmp_k_w1, cmp_k_w2, cmp_k_pe, cmp_v_w1, cmp_v_w2, cmp_v_pe, w_out, ln1_g, ln1_b, ln2_g, ln2_b, ln3_g, ln3_b, ffn1_w_gate, ffn1_w_up, ffn1_w_down, ffn2_w_gate, ffn2_w_up, ffn2_w_down):
    weights = dict(w_in=w_in, b_gate=b_gate, conv_w=conv_w, conv_b=conv_b, dt_bias=dt_bias, a_log=a_log,
                   d_skip=d_skip, ssm_norm_w=ssm_norm_w, cmp_k_w1=cmp_k_w1, cmp_k_w2=cmp_k_w2,
                   cmp_k_pe=cmp_k_pe, cmp_v_w1=cmp_v_w1, cmp_v_w2=cmp_v_w2, cmp_v_pe=cmp_v_pe, w_out=w_out,
                   ln1_g=ln1_g, ln1_b=ln1_b, ln2_g=ln2_g, ln2_b=ln2_b, ln3_g=ln3_g, ln3_b=ln3_b,
                   ffn1_w_gate=ffn1_w_gate, ffn1_w_up=ffn1_w_up, ffn1_w_down=ffn1_w_down,
                   ffn2_w_gate=ffn2_w_gate, ffn2_w_up=ffn2_w_up, ffn2_w_down=ffn2_w_down)
    depth = w_in.shape[0]
    bsz, t, d = x_prompt.shape
    dbz, s_len, _ = x_sample.shape
    alpha = (2.0 * depth) ** 0.25
    caches = (cache_k_cmp, cache_v_cmp, cache_k_slc, cache_v_slc, cache_k_win, cache_v_win)
    y_p = x_prompt.reshape(bsz * t, d)
    y_s = x_sample.reshape(dbz * s_len, d)
    p_states, s_states = [], []
    for l in range(depth):
        lw = _layer_weights(weights, l)
        y_p, st_p = _layer(y_p, lw, alpha, lambda h: _mix_prompt(h, lw, bsz, t))
        y_s, st_s = _layer(y_s, lw, alpha, lambda h: _mix_sample(h, lw, l, dbz, s_len, caches, state_ssm,
                                                                 state_conv, page_table))
        p_states.append(st_p)
        s_states.append(st_s)
    p_st = [jnp.stack(a) for a in zip(*p_states)]
    s_st = [jnp.stack(a) for a in zip(*s_states)]
    outs = [y_p.reshape(bsz, t, d), y_s.reshape(dbz, s_len, d)]
    for p, s in zip(p_st[:6], s_st[:6]):
        outs += [p, s]
    outs += [p_st[6], s_st[6], p_st[7], s_st[7]]
    return tuple(outs)
```

```python
import functools
import math

import numpy as np
import jax
import jax.numpy as jnp
from jax import lax
from jax.experimental import pallas as pl
from jax.experimental.pallas import tpu as pltpu

F32 = jnp.float32
BF16 = jnp.bfloat16
HI = lax.Precision.HIGHEST

HD = 64
N_SSM_HEADS = 8
N_SSM_GROUPS = 2
D_SSM = 512
CONV_DIM = 768
CONV_TAPS = 4
N_HEADS = 8
N_KV = 2
REP = N_HEADS // N_KV
D_NSA = 512
KV_DIM = N_KV * HD
N_KV_PROJ = 6
CMP_STRIDE = 16
CMP_BLOCK = 32
CMP_HIDDEN = 128
SEL_BLOCK = 64
SEL_TOP_N = 16
WINDOW = 512
PAGE = 128
ROPE_THETA = 10000.0
LN_EPS = 1e-5
RMS_EPS = 1e-5
NEG = -1e30
FORCED_SCORE = 1e30
INVALID_SCORE = -1.0

LANES = 128
SUBLANES = 8
VMEM_LIMIT = 56 * 1024 * 1024

NT_DIMS = (((1,), (1,)), ((), ()))
TN_DIMS = (((0,), (0,)), ((), ()))


def _params(sem):
    return pltpu.CompilerParams(dimension_semantics=sem, vmem_limit_bytes=VMEM_LIMIT)


def _dot(a, b, precision=None):
    return jnp.dot(a, b, preferred_element_type=F32, precision=precision)


def _dot_nt(a, b, precision=None):
    return lax.dot_general(a, b, NT_DIMS, preferred_element_type=F32, precision=precision)


def _dot_tn(a, b, precision=None):
    return lax.dot_general(a, b, TN_DIMS, preferred_element_type=F32, precision=precision)


def _iota(shape, dim):
    return lax.broadcasted_iota(jnp.int32, shape, dim)


def _layer_norm(y, g, b):
    mu = jnp.mean(y, axis=-1, keepdims=True)
    yc = y - mu
    var = jnp.mean(yc * yc, axis=-1, keepdims=True)
    return yc * lax.rsqrt(var + LN_EPS) * g + b


def _ffn_ln_kernel(x_ref, wg_ref, wu_ref, wd_ref, g_ref, b_ref, o_ref, y_ref, *, alpha, n_split, skip_flush_matmuls):
    step = pl.program_id(0)

    @pl.when(step == 0)
    def _():
        y_ref[...] = jnp.zeros_like(y_ref)

    o_ref[...] = _layer_norm(y_ref[...], g_ref[...], b_ref[...])

    def matmuls():
        x = x_ref[...]
        xb = x.astype(BF16)
        tf = wg_ref.shape[1] // n_split
        acc = None
        for s in range(n_split):
            gate = _dot(xb, wg_ref[:, s * tf:(s + 1) * tf])
            up = _dot(xb, wu_ref[:, s * tf:(s + 1) * tf])
            act = (jax.nn.silu(gate) * up).astype(BF16)
            part = _dot(act, wd_ref[s * tf:(s + 1) * tf, :])
            acc = part if acc is None else acc + part
        y_ref[...] = alpha * x + 0.5 * acc

    if skip_flush_matmuls:
        pl.when(step < pl.num_programs(0) - 1)(matmuls)
    else:
        matmuls()


def _ffn_ln(x, wg, wu, wd, g, b, alpha):
    m, d = x.shape
    dff = wg.shape[1]
    tm = min(512, m)
    n_i = m // tm
    n_split = 2 if (dff // 2) % LANES == 0 else 1
    resident = lambda a: pl.BlockSpec(a.shape, lambda i: (0, 0), pipeline_mode=pl.Buffered(1))
    return pl.pallas_call(
        functools.partial(_ffn_ln_kernel, alpha=alpha, n_split=n_split, skip_flush_matmuls=n_i < 8),
        grid=(n_i + 1,),
        in_specs=[
            pl.BlockSpec((tm, d), lambda i: (jnp.minimum(i, n_i - 1), 0)),
            resident(wg), resident(wu), resident(wd),
            pl.BlockSpec((1, d), lambda i: (0, 0)),
            pl.BlockSpec((1, d), lambda i: (0, 0)),
        ],
        out_specs=pl.BlockSpec((tm, d), lambda i: (jnp.maximum(i - 1, 0), 0)),
        out_shape=jax.ShapeDtypeStruct((m, d), F32),
        scratch_shapes=[pltpu.VMEM((tm, d), F32)],
        compiler_params=_params(("arbitrary",)),
        name="ffn_ln",
    )(x, wg, wu, wd, g, b)


OFF_Z, OFF_XBC, OFF_Q, OFF_MISC, W_IN_COLS = 0, 512, 1280, 1792, 1920


def _in_proj_kernel(h_ref, w_ref, wkv_ref, bias_ref, cos_ref, sin_ref, cost_ref, sint_ref,
                    z_ref, xbc_ref, misc_ref, qh_ref, kvtb_ref, *kvt_refs, q_scale):
    hb = h_ref[...].astype(BF16)
    z_ref[...] = _dot(hb, w_ref[:, OFF_Z:OFF_XBC])
    xbc_ref[...] = _dot(hb, w_ref[:, OFF_XBC:OFF_Q])
    qm = _dot(hb, w_ref[:, OFF_Q:W_IN_COLS])
    misc_ref[...] = qm[:, OFF_MISC - OFF_Q:] + bias_ref[...]
    cos = cos_ref[...]
    sin = sin_ref[...]
    first_half = (_iota(cos.shape, 1) & (HD - 1)) < (HD // 2)
    for c in range(D_NSA // LANES):
        x = qm[:, c * LANES:(c + 1) * LANES]
        rot = jnp.where(first_half, pltpu.roll(x, LANES - HD // 2, 1), pltpu.roll(x, HD // 2, 1))
        q = ((x * cos + rot * sin) * q_scale).astype(BF16)
        for j in range(LANES // HD):
            qh_ref[0, c * (LANES // HD) + j] = q[:, j * HD:(j + 1) * HD]
    kvt = _dot_nt(wkv_ref[...], hb)
    cost = cost_ref[...]
    sint = sint_ref[...]
    half = HD // 2
    for i in range(N_KV_PROJ):
        blk = kvt[i * KV_DIM:(i + 1) * KV_DIM, :]
        if i % 2 == 0:
            parts = []
            for g in range(N_KV):
                x1 = blk[g * HD:g * HD + half, :]
                x2 = blk[g * HD + half:(g + 1) * HD, :]
                parts += [x1 * cost - x2 * sint, x2 * cost + x1 * sint]
            blk = jnp.concatenate(parts, axis=0)
        kvt_refs[i][0] = blk
        kvtb_ref[0, i * KV_DIM:(i + 1) * KV_DIM, :] = blk.astype(BF16)


def _in_proj(h, w, wkv, bias, tables, n_seq, t_seq):
    cos, sin, cost, sint = tables
    m, d = h.shape
    tm = min(512, t_seq)
    n_pos = t_seq // tm
    row = lambda i: (i, 0)
    fixed = lambda i: (0, 0)
    out_shapes = (
        jax.ShapeDtypeStruct((m, D_SSM), F32),
        jax.ShapeDtypeStruct((m, CONV_DIM), F32),
        jax.ShapeDtypeStruct((m, LANES), F32),
        jax.ShapeDtypeStruct((n_seq, N_HEADS, t_seq, HD), BF16),
        jax.ShapeDtypeStruct((n_seq, N_KV_PROJ * KV_DIM, t_seq), BF16),
    ) + (jax.ShapeDtypeStruct((n_seq, KV_DIM, t_seq), F32),) * N_KV_PROJ
    seq_t = lambda i: (i // n_pos, 0, i % n_pos)
    return pl.pallas_call(
        functools.partial(_in_proj_kernel, q_scale=HD ** -0.5),
        grid=(m // tm,),
        in_specs=[
            pl.BlockSpec((tm, d), row),
            pl.BlockSpec(w.shape, fixed),
            pl.BlockSpec(wkv.shape, fixed),
            pl.BlockSpec((1, LANES), fixed),
            pl.BlockSpec((tm, LANES), lambda i: (i % n_pos, 0)),
            pl.BlockSpec((tm, LANES), lambda i: (i % n_pos, 0)),
            pl.BlockSpec((HD // 2, tm), lambda i: (0, i % n_pos)),
            pl.BlockSpec((HD // 2, tm), lambda i: (0, i % n_pos)),
        ],
        out_specs=(
            pl.BlockSpec((tm, D_SSM), row),
            pl.BlockSpec((tm, CONV_DIM), row),
            pl.BlockSpec((tm, LANES), row),
            pl.BlockSpec((1, N_HEADS, tm, HD), lambda i: (i // n_pos, 0, i % n_pos, 0)),
            pl.BlockSpec((1, N_KV_PROJ * KV_DIM, tm), seq_t),
        ) + (pl.BlockSpec((1, KV_DIM, tm), seq_t),) * N_KV_PROJ,
        out_shape=out_shapes,
        compiler_params=_params(("parallel",)),
        name="in_proj",
    )(h, w, wkv, bias, cos, sin, cost, sint)


def _ssd_kernel(z_ref, xbc_ref, misc_ref, misct_ref, convw_ref, convb_ref,
                dtb_row_ref, alog_row_ref, dtb_col_ref, alog_col_ref, dskip_ref, normw_ref,
                y_ref, hout_ref, xbuf, hstate, ybuf, *, chunk, nseq):
    c = pl.program_id(1)

    @pl.when(c == 0)
    def _():
        xbuf[:, 0:SUBLANES, :] = jnp.zeros((nseq, SUBLANES, CONV_DIM), F32)
        hstate[...] = jnp.zeros_like(hstate)

    for i in range(nseq):
        _ssd_chunk(z_ref.at[i], xbc_ref.at[i], misc_ref.at[i], misct_ref.at[i], convw_ref, convb_ref,
                   dtb_row_ref, alog_row_ref, dtb_col_ref, alog_col_ref, dskip_ref, normw_ref,
                   y_ref.at[i], xbuf.at[i], hstate.at[i], ybuf.at[i], chunk)

    @pl.when(c == pl.num_programs(1) - 1)
    def _():
        hout_ref[...] = hstate[...]


def _ssd_chunk(z_ref, xbc_ref, misc_ref, misct_ref, convw_ref, convb_ref,
               dtb_row_ref, alog_row_ref, dtb_col_ref, alog_col_ref, dskip_ref, normw_ref,
               y_ref, xbuf, hstate, ybuf, chunk):
    L = chunk
    H = SUBLANES
    x = xbc_ref[...]
    xbuf[H:H + L, :] = x
    conv = convb_ref[...] + convw_ref[CONV_TAPS - 1:CONV_TAPS, :] * x
    for k in range(1, CONV_TAPS):
        conv = conv + convw_ref[CONV_TAPS - 1 - k:CONV_TAPS - k, :] * xbuf[H - k:H - k + L, :]
    xbuf[0:H, :] = x[L - H:L, :]
    xact = jax.nn.silu(conv)
    bm = xact[:, D_SSM:D_SSM + N_SSM_GROUPS * HD]
    cm = xact[:, D_SSM + N_SSM_GROUPS * HD:]

    dt = jax.nn.softplus(misc_ref[...] + dtb_row_ref[...])
    dtt = jax.nn.softplus(misct_ref[0:SUBLANES, :] + dtb_col_ref[...])
    a_row = -jnp.exp(alog_row_ref[...])
    a_col = -jnp.exp(alog_col_ref[...])
    lane_ok = _iota(dt.shape, 1) < N_SSM_HEADS
    da = jnp.where(lane_ok, dt * a_row, 0.0)
    dat = dtt * a_col
    ri = _iota((L, L), 0)
    ci = _iota((L, L), 1)
    causal = ri >= ci
    tri = jnp.where(causal, 1.0, 0.0).astype(F32)
    cum = _dot(tri, da, HI)
    cumt = _dot_nt(dat, tri, HI)
    cum_last = cum[L - 1:L, :]

    rep = N_SSM_HEADS // N_SSM_GROUPS
    for g in range(N_SSM_GROUPS):
        cm_g = cm[:, g * HD:(g + 1) * HD].astype(BF16)
        bm_g = bm[:, g * HD:(g + 1) * HD].astype(BF16)
        cb = _dot_nt(cm_g, bm_g)
        for r in range(rep):
            h = g * rep + r
            ch = cum[:, h:h + 1]
            seg = ch - cumt[h:h + 1, :]
            decay = jnp.where(causal, jnp.exp(jnp.where(causal, seg, 0.0)), 0.0)
            xs_h = xact[:, h * HD:(h + 1) * HD]
            xdt = xs_h * dt[:, h:h + 1]
            hprev = hstate[h]
            y_h = (_dot((cb * decay).astype(BF16), xdt.astype(BF16))
                   + _dot_nt(cm_g, hprev.astype(BF16)) * jnp.exp(ch))
            cl = cum_last[:, h:h + 1]
            tail = jnp.exp(cl - ch)
            hstate[h] = hprev * jnp.exp(cl) + _dot_tn((xdt * tail).astype(BF16), bm_g)
            ybuf[:, h * HD:(h + 1) * HD] = y_h

    xs = xact[:, :D_SSM]
    y = (ybuf[...] + dskip_ref[...] * xs) * jax.nn.silu(z_ref[...])
    y = y * lax.rsqrt(jnp.mean(y * y, axis=-1, keepdims=True) + RMS_EPS) * normw_ref[...]
    y_ref[...] = y


def _ssd(z, xbc, misc, misct, lw, chunk):
    bsz, t, _ = z.shape
    nc = t // chunk
    nseq = 1
    seq = lambda b, c: (b, c, 0)
    per_b4 = lambda b, c: (b, 0, 0, 0)
    fixed = lambda b, c: (0, 0)
    return pl.pallas_call(
        functools.partial(_ssd_kernel, chunk=chunk, nseq=nseq),
        grid=(bsz // nseq, nc),
        in_specs=[
            pl.BlockSpec((nseq, chunk, D_SSM), seq),
            pl.BlockSpec((nseq, chunk, CONV_DIM), seq),
            pl.BlockSpec((nseq, chunk, LANES), seq),
            pl.BlockSpec((nseq, LANES, chunk), lambda b, c: (b, 0, c)),
            pl.BlockSpec((CONV_TAPS, CONV_DIM), fixed),
            pl.BlockSpec((1, CONV_DIM), fixed),
            pl.BlockSpec((1, LANES), fixed),
            pl.BlockSpec((1, LANES), fixed),
            pl.BlockSpec((SUBLANES, 1), fixed),
            pl.BlockSpec((SUBLANES, 1), fixed),
            pl.BlockSpec((1, D_SSM), fixed),
            pl.BlockSpec((1, D_SSM), fixed),
        ],
        out_specs=(
            pl.BlockSpec((nseq, chunk, D_SSM), seq),
            pl.BlockSpec((nseq, N_SSM_HEADS, HD, HD), per_b4),
        ),
        out_shape=(
            jax.ShapeDtypeStruct((bsz, t, D_SSM), F32),
            jax.ShapeDtypeStruct((bsz, N_SSM_HEADS, HD, HD), F32),
        ),
        scratch_shapes=[
            pltpu.VMEM((nseq, SUBLANES + chunk, CONV_DIM), F32),
            pltpu.VMEM((nseq, N_SSM_HEADS, HD, HD), F32),
            pltpu.VMEM((nseq, chunk, D_SSM), F32),
        ],
        compiler_params=_params(("parallel", "arbitrary")),
        name="ssd",
    )(z, xbc, misc, misct, lw["conv_w"], lw["conv_b_row"], lw["dtb_row"], lw["alog_row"],
      lw["dtb_col"], lw["alog_col"], lw["dskip_row"], lw["normw_row"])


def _ssd_sample_kernel(xh_ref, z_ref, dtraw_ref, state_ref, convw_ref, convb_ref, dtb_ref, alog_ref,
                       dskip_ref, normw_ref, y_ref, hout_ref, xact, ypre, *, s_len):
    h = pl.program_id(0)

    @pl.when(h == 0)
    def _():
        for t in range(s_len):
            conv = convb_ref[...]
            for k in range(CONV_TAPS):
                conv = conv + convw_ref[k] * xh_ref[t + k]
            xact[t] = jax.nn.silu(conv)

    a = -jnp.exp(alog_ref[pl.ds(h, 1), :])
    dts = [jax.nn.softplus(dtraw_ref[t, pl.ds(h, 1), :] + dtb_ref[pl.ds(h, 1), :]) for t in range(s_len)]
    das = [jnp.exp(dt * a) for dt in dts]
    g = h // (N_SSM_HEADS // N_SSM_GROUPS)
    b_lo = pl.multiple_of(D_SSM + g * HD, HD)
    c_lo = pl.multiple_of(D_SSM + N_SSM_GROUPS * HD + g * HD, HD)
    bs = [xact[t, pl.ds(b_lo, HD), :] for t in range(s_len)]
    cs = [xact[t, pl.ds(c_lo, HD), :] for t in range(s_len)]

    def body(p, carry):
        hs = state_ref[0, p]
        row = h * HD + p
        for t in range(s_len):
            x = xact[t, pl.ds(row, 1), :]
            hs = hs * das[t] + (x * dts[t]) * bs[t]
            y = jnp.sum(cs[t] * hs, axis=0, keepdims=True)
            ypre[t, pl.ds(row, 1), :] = y + dskip_ref[pl.ds(row, 1), :] * x
        hout_ref[0, p] = hs
        return carry

    lax.fori_loop(0, HD, body, 0)

    @pl.when(h == pl.num_programs(0) - 1)
    def _():
        for t in range(s_len):
            y = ypre[t] * jax.nn.silu(z_ref[t])
            y = y * lax.rsqrt(jnp.mean(y * y, axis=0, keepdims=True) + RMS_EPS) * normw_ref[...]
            y_ref[t] = y


def _ssd_sample(xh, zt, dtraw, state, lw, dbz, s_len):
    lane_b = lambda v: jnp.broadcast_to(v[..., None], v.shape + (dbz,))
    full = lambda a: pl.BlockSpec(a.shape, lambda h: (0,) * a.ndim)
    args = (xh, zt, dtraw, state, lane_b(lw["conv_w"]), lane_b(lw["conv_b_row"][0]), lane_b(lw["dtb_col"][:, 0]),
            lane_b(lw["alog_col"][:, 0]), lane_b(lw["dskip_row"][0]), lane_b(lw["normw_row"][0]))
    state_spec = pl.BlockSpec((1, HD, HD, dbz), lambda h: (h, 0, 0, 0))
    in_specs = [full(a) for a in args]
    in_specs[3] = state_spec
    return pl.pallas_call(
        functools.partial(_ssd_sample_kernel, s_len=s_len),
        grid=(N_SSM_HEADS,),
        in_specs=in_specs,
        out_specs=(pl.BlockSpec((s_len, D_SSM, dbz), lambda h: (0, 0, 0)), state_spec),
        out_shape=(jax.ShapeDtypeStruct((s_len, D_SSM, dbz), F32),
                   jax.ShapeDtypeStruct(state.shape, F32)),
        scratch_shapes=[pltpu.VMEM((s_len, CONV_DIM, dbz), F32), pltpu.VMEM((s_len, D_SSM, dbz), F32)],
        compiler_params=_params(("arbitrary",)),
        name="ssd_sample",
    )(*args)


def _page_group(n_pages):
    return math.gcd(n_pages, 8)


def _compress_pages(get_pages, n_pages, tail_rows, pm_ref, w1_ref, w2_ref, pe_ref, x_buf, sec_buf):
    rows_pp = PAGE // CMP_STRIDE
    n = n_pages * rows_pp
    half = N_KV * CMP_HIDDEN
    group = _page_group(n_pages)

    for i in range(n_pages // group):
        perm = _dot_nt(pm_ref[...], get_pages(i))
        for j in range(group):
            r0 = (i * group + j) * rows_pp
            for s in range(CMP_STRIDE):
                x_buf[r0:r0 + rows_pp, s * KV_DIM:(s + 1) * KV_DIM] = (
                    perm[s * rows_pp:(s + 1) * rows_pp, j * KV_DIM:(j + 1) * KV_DIM])
    x_buf[n:n + SUBLANES, :] = tail_rows
    out = _dot(x_buf[...].astype(BF16), w1_ref[...])
    sec_buf[...] = out[:, half:]
    pe_out = _dot(pe_ref[...], w1_ref[...])
    pe_term = pe_out[0:1, :half] + pe_out[SUBLANES:SUBLANES + 1, half:]
    pre = out[0:n, :half] + sec_buf[1:n + 1, :] + pe_term
    return _dot(jax.nn.gelu(pre).astype(BF16), w2_ref[...])


def _compress_kernel(x_ref, pm_ref, w1_ref, w2_ref, pe_ref, o_ref, x_buf, sec_buf, *, n_pages):
    group = _page_group(n_pages)

    def get_pages(i):
        wide = x_ref[0, 0, :, i * group * PAGE:(i + 1) * group * PAGE]
        return jnp.concatenate([wide[:, j * PAGE:(j + 1) * PAGE] for j in range(group)], axis=0)

    tail_rows = jnp.zeros((SUBLANES, CMP_STRIDE * KV_DIM), F32)
    o_ref[0] = _compress_pages(get_pages, n_pages, tail_rows, pm_ref, w1_ref, w2_ref, pe_ref, x_buf, sec_buf)


def _compress(kvtb4, proj, pm, w1big, w2big, pe2):
    bsz, _, _, t = kvtb4.shape
    n_pages = t // PAGE
    n = t // CMP_STRIDE
    fixed = lambda b: (0, 0)
    return pl.pallas_call(
        functools.partial(_compress_kernel, n_pages=n_pages),
        grid=(bsz,),
        in_specs=[
            pl.BlockSpec((1, 1, KV_DIM, t), lambda b: (b, proj, 0, 0)),
            pl.BlockSpec(pm.shape, fixed),
            pl.BlockSpec(w1big.shape, fixed),
            pl.BlockSpec(w2big.shape, fixed),
            pl.BlockSpec(pe2.shape, fixed),
        ],
        out_specs=pl.BlockSpec((1, n, KV_DIM), lambda b: (b, 0, 0)),
        out_shape=jax.ShapeDtypeStruct((bsz, n, KV_DIM), F32),
        scratch_shapes=[pltpu.VMEM((n + SUBLANES, CMP_STRIDE * KV_DIM), F32),
                        pltpu.VMEM((n + SUBLANES, N_KV * CMP_HIDDEN), F32)],
        compiler_params=_params(("parallel",)),
        name="compress",
    )(kvtb4, pm, w1big, w2big, pe2)


def _page_copy(cache_ref, buf, sem, page, slot, p):
    return pltpu.make_async_copy(cache_ref.at[page], buf.at[slot, pl.ds(p * PAGE, PAGE)], sem.at[slot])


def _gather_pages(pt_ref, streams, n_pages):
    b = pl.program_id(0)
    slot = b % 2

    def issue(seq, into):
        def start(p, carry):
            page = pt_ref[seq, p]
            for cache_ref, buf, sem in streams:
                _page_copy(cache_ref, buf, sem, page, into, p).start()
            return carry
        lax.fori_loop(0, n_pages, start, 0, unroll=_page_group(n_pages))

    @pl.when(b == 0)
    def _():
        issue(b, slot)

    @pl.when(b + 1 < pl.num_programs(0))
    def _():
        issue(b + 1, 1 - slot)

    def wait(p, carry):
        for cache_ref, buf, sem in streams:
            _page_copy(cache_ref, buf, sem, 0, slot, p).wait()
        return carry

    lax.fori_loop(0, n_pages, wait, 0, unroll=_page_group(n_pages))
    return slot


def _compress_paged_kernel(pt_ref, kcache_ref, vcache_ref, ktail_ref, vtail_ref, pm_ref,
                           kw1_ref, kw2_ref, kpe_ref, vw1_ref, vw2_ref, vpe_ref, ko_ref, vo_ref,
                           kbuf, vbuf, kx_buf, vx_buf, ksec_buf, vsec_buf, ksem, vsem, *, n_pages):
    slot = _gather_pages(pt_ref, [(kcache_ref, kbuf, ksem), (vcache_ref, vbuf, vsem)], n_pages)
    rows = _page_group(n_pages) * PAGE
    for buf, tail_ref, w1_ref, w2_ref, pe_ref, x_buf, sec_buf, o_ref in (
            (kbuf, ktail_ref, kw1_ref, kw2_ref, kpe_ref, kx_buf, ksec_buf, ko_ref),
            (vbuf, vtail_ref, vw1_ref, vw2_ref, vpe_ref, vx_buf, vsec_buf, vo_ref)):
        get_pages = lambda i, buf=buf: buf[slot, i * rows:(i + 1) * rows, :].astype(BF16)
        o_ref[0] = _compress_pages(get_pages, n_pages, tail_ref[0], pm_ref, w1_ref, w2_ref, pe_ref,
                                   x_buf, sec_buf)


def _compress_paged(page_table, kcache, vcache, ktail, vtail, kweights, vweights):
    dbz, n_pages = page_table.shape
    n = n_pages * (PAGE // CMP_STRIDE)
    pm = kweights[0]
    consts = (pm,) + tuple(kweights[1:]) + tuple(vweights[1:])
    resident = lambda a: pl.BlockSpec(a.shape, lambda b, pt: (0, 0), pipeline_mode=pl.Buffered(1))
    tail_spec = pl.BlockSpec((1,) + ktail.shape[1:], lambda b, pt: (b, 0, 0))
    out_spec = pl.BlockSpec((1, n, KV_DIM), lambda b, pt: (b, 0, 0))
    grid_spec = pltpu.PrefetchScalarGridSpec(
        num_scalar_prefetch=1,
        grid=(dbz,),
        in_specs=[pl.BlockSpec(memory_space=pl.ANY), pl.BlockSpec(memory_space=pl.ANY), tail_spec, tail_spec]
                 + [resident(a) for a in consts],
        out_specs=(out_spec, out_spec),
        scratch_shapes=[pltpu.VMEM((2, n_pages * PAGE, PAGE), F32)] * 2
                       + [pltpu.VMEM((n + SUBLANES, CMP_STRIDE * KV_DIM), F32)] * 2
                       + [pltpu.VMEM((n + SUBLANES, N_KV * CMP_HIDDEN), F32)] * 2
                       + [pltpu.SemaphoreType.DMA((2,))] * 2,
    )
    out = jax.ShapeDtypeStruct((dbz, n, KV_DIM), F32)
    return pl.pallas_call(
        functools.partial(_compress_paged_kernel, n_pages=n_pages),
        grid_spec=grid_spec,
        out_shape=(out, out),
        compiler_params=_params(("arbitrary",)),
        name="compress_paged",
    )(page_table, kcache, vcache, ktail, vtail, *consts)


RANK_UNROLL = 4


def _select_blocks(imp_t, tpos, n_blocks, score_buf, n_live=None):
    j = _iota(imp_t.shape, 0)
    valid = (j * SEL_BLOCK <= tpos) & (j < n_blocks)
    cur = tpos // SEL_BLOCK
    forced = (j == 0) | (j == cur) | (j == cur - 1)
    score = jnp.where(valid, jnp.where(forced, FORCED_SCORE, imp_t), INVALID_SCORE)
    score_buf[...] = score

    def body(i, cnt):
        row = score_buf[pl.ds(i, 1), :]
        above = jnp.where(row > score, 1.0, 0.0)
        tie = jnp.where(row == score, jnp.where(j > i, 1.0, 0.0), 0.0)
        return cnt + above + tie

    zero = jnp.zeros(imp_t.shape, F32)
    if n_live is None:
        cnt = lax.fori_loop(0, n_blocks, body, zero, unroll=RANK_UNROLL)
    else:
        def group(gi, cnt):
            for u in range(RANK_UNROLL):
                cnt = body(gi * RANK_UNROLL + u, cnt)
            return cnt
        assert imp_t.shape[0] % RANK_UNROLL == 0
        cnt = lax.fori_loop(0, (n_live + RANK_UNROLL - 1) // RANK_UNROLL, group, zero)
    return jnp.where(valid & (cnt < float(min(SEL_TOP_N, n_blocks))), 1.0, 0.0)


def _nsa_prompt_kernel(q_ref, kc_ref, vc_ref, ks_ref, vs_ref, kw_ref, vw_ref, ovl_ref, e_ref, wbias_ref, misc_ref,
                       o_ref, score_buf, *, tq, tk, tw, n_blocks):
    g = pl.program_id(1)
    t0 = pl.program_id(2) * tq
    rows = REP * tq
    q = q_ref[0].reshape(rows, HD)

    kc = kc_ref[0, 0]
    ncp = kc.shape[0]
    qpos_c = t0 + _iota((tq, ncp), 0)
    maskc = ((_iota((tq, ncp), 1) * CMP_STRIDE + (CMP_BLOCK - 1)) <= qpos_c)[None]
    s = _dot_nt(q, kc).reshape(REP, tq, ncp)
    sm = jnp.where(maskc, s, NEG)
    e = jnp.exp(sm - jnp.max(sm, axis=-1, keepdims=True))
    p = e / jnp.sum(e, axis=-1, keepdims=True) * jnp.where(maskc, 1.0, 0.0)
    o_c = _dot(p.reshape(rows, ncp).astype(BF16), vc_ref[0, 0])
    psum = jnp.sum(p, axis=0)

    imp_t = _dot_nt(ovl_ref[...], psum, HI)
    tpos = t0 + _iota(imp_t.shape, 1)
    n_live = jnp.minimum((t0 + tq - 1) // SEL_BLOCK + 1, n_blocks)
    selt = _select_blocks(imp_t, tpos, n_blocks, score_buf, n_live).astype(BF16)

    nsp = selt.shape[0]
    eye = jnp.where(_iota((nsp, nsp), 0) == _iota((nsp, nsp), 1), 1.0, 0.0).astype(BF16)
    bias = ((_dot_tn(selt, eye) - 1.0) * (-NEG)).astype(BF16)
    q_aug = jnp.concatenate([q, jnp.concatenate([bias] * REP, axis=0)], axis=1)
    ones_k = jnp.ones((HD, tk), BF16)
    qpos_k = t0 + _iota((tq, tk), 0)
    lane_k = _iota((tq, tk), 1)
    n_full = t0 // tk

    def step(kt, carry, diagonal):
        m, acc = carry
        k0 = pl.multiple_of(kt * tk, tk)
        k_aug = jnp.concatenate([ks_ref[0, 0, :, pl.ds(k0, tk)], e_ref[kt]], axis=0)
        v_aug = jnp.concatenate([vs_ref[0, 0, :, pl.ds(k0, tk)], ones_k], axis=0)
        s = _dot(q_aug, k_aug).reshape(REP, tq, tk)
        if diagonal:
            s = jnp.where((k0 + lane_k <= qpos_k)[None], s, NEG)
        m_new = jnp.maximum(m, jnp.max(s, axis=-1, keepdims=True))
        e = jnp.exp(s - m_new)
        pv = _dot_nt(e.reshape(rows, tk).astype(BF16), v_aug).reshape(REP, tq, 2 * HD)
        return m_new, jnp.exp(m - m_new) * acc + pv

    init = (jnp.full((REP, tq, 1), NEG, F32), jnp.zeros((REP, tq, 2 * HD), F32))
    carry = lax.fori_loop(0, n_full, functools.partial(step, diagonal=False), init)
    _, acc_s = step(n_full, carry, True)
    o_s = acc_s[..., :HD] / acc_s[..., HD:]

    wk = WINDOW + tw
    q3 = q.reshape(REP, tq, HD)
    ones_w = jnp.ones((HD, wk), BF16)
    o_w = []
    for w in range(tq // tw):
        t0w = t0 + w * tw
        start = pl.multiple_of(jnp.maximum(t0w - WINDOW, 0), tw)
        kw = kw_ref[0, 0, :, pl.ds(start, wk)]
        vw_aug = jnp.concatenate([vw_ref[0, 0, :, pl.ds(start, wk)], ones_w], axis=0)
        case = jnp.minimum(t0w // tw, WINDOW // tw)
        qw = q3[:, w * tw:(w + 1) * tw].reshape(REP * tw, HD)
        sm = _dot(qw, kw).reshape(REP, tw, wk) + wbias_ref[case][None]
        e = jnp.exp(sm - jnp.max(sm, axis=-1, keepdims=True))
        acc_w = _dot_nt(e.reshape(REP * tw, wk).astype(BF16), vw_aug).reshape(REP, tw, 2 * HD)
        o_w.append(acc_w[..., :HD] / acc_w[..., HD:])
    o_w = o_w[0] if len(o_w) == 1 else jnp.concatenate(o_w, axis=1)

    gates = jax.nn.sigmoid(misc_ref[0])
    gate_lane = _iota(gates.shape, 1) - (N_SSM_HEADS + 3 * REP * g)
    o_c = o_c.reshape(REP, tq, HD)
    for r in range(REP):
        gh = [jnp.sum(jnp.where(gate_lane == 3 * r + br, gates, 0.0), axis=-1, keepdims=True)
              for br in range(3)]
        o = gh[0] * o_c[r] + gh[1] * o_s[r] + gh[2] * o_w[r]
        o_ref[0, :, r * HD:(r + 1) * HD] = o.astype(o_ref.dtype)


NSA_WINDOW_TILE = 128


def _window_bias(tw):
    c = np.arange(WINDOW // tw + 1)[:, None, None]
    d = np.arange(WINDOW + tw)[None, None, :] - c * tw - np.arange(tw)[None, :, None]
    return jnp.asarray(np.where((d <= 0) & (d > -WINDOW), 0.0, NEG).astype(np.float32))


def _nsa_prompt(qh, kch, vch, kvtb, ovl_t, e3, misc, n_blocks, tq, tk):
    bsz, nh, t, _ = qh.shape
    ncp = kch.shape[2]
    nsp = ovl_t.shape[0]
    kv_spec = lambda proj: pl.BlockSpec((1, 1, HD, t), lambda b, g, i: (b, N_KV * proj + g, 0, 0))
    cmp_spec = pl.BlockSpec((1, 1, ncp, HD), lambda b, g, i: (b, g, 0, 0))
    tw = min(tq, NSA_WINDOW_TILE)
    wbias = _window_bias(tw)
    return pl.pallas_call(
        functools.partial(_nsa_prompt_kernel, tq=tq, tk=tk, tw=tw, n_blocks=n_blocks),
        grid=(bsz, N_KV, t // tq),
        in_specs=[
            pl.BlockSpec((1, REP, tq, HD), lambda b, g, i: (b, g, i, 0)),
            cmp_spec, cmp_spec,
            kv_spec(2), kv_spec(3), kv_spec(4), kv_spec(5),
            pl.BlockSpec(ovl_t.shape, lambda b, g, i: (0, 0)),
            pl.BlockSpec(e3.shape, lambda b, g, i: (0, 0, 0)),
            pl.BlockSpec(wbias.shape, lambda b, g, i: (0, 0, 0)),
            pl.BlockSpec((1, tq, LANES), lambda b, g, i: (b, i, 0)),
        ],
        out_specs=pl.BlockSpec((1, tq, REP * HD), lambda b, g, i: (b, i, g)),
        out_shape=jax.ShapeDtypeStruct((bsz, t, nh * HD), BF16),
        scratch_shapes=[pltpu.VMEM((nsp, tq), F32)],
        compiler_params=_params(("parallel", "parallel", "arbitrary")),
        name="nsa_prompt",
    )(qh, kch, vch, kvtb, kvtb, kvtb, kvtb, ovl_t, e3, wbias, misc)


def _out_proj_kernel(y_ref, o_ref, h_ref, w_ref, g_ref, b_ref, out_ref, *, alpha):
    mixed = _dot(y_ref[...].astype(BF16), w_ref[0:D_SSM, :]) + _dot(o_ref[...], w_ref[D_SSM:, :])
    out_ref[...] = _layer_norm(alpha * h_ref[...] + mixed, g_ref[...], b_ref[...])


def _out_proj_ln(y, o, h, w, g, b, alpha):
    m, d = h.shape
    tm = min(512, m)
    row = lambda i: (i, 0)
    fixed = lambda i: (0, 0)
    return pl.pallas_call(
        functools.partial(_out_proj_kernel, alpha=alpha),
        grid=(m // tm,),
        in_specs=[
            pl.BlockSpec((tm, D_SSM), row),
            pl.BlockSpec((tm, D_NSA), row),
            pl.BlockSpec((tm, d), row),
            pl.BlockSpec(w.shape, fixed),
            pl.BlockSpec((1, d), fixed),
            pl.BlockSpec((1, d), fixed),
        ],
        out_specs=pl.BlockSpec((tm, d), row),
        out_shape=jax.ShapeDtypeStruct((m, d), F32),
        compiler_params=_params(("parallel",)),
        name="out_proj_ln",
    )(y, o, h, w, g, b)


def _row_group(shape):
    return _iota(shape, 0) // (shape[0] // N_KV)


def _cmp_attn_sample_kernel(q_ref, kc_ref, vc_ref, ssum_ref, oc_ref, psum_ref, *, past, s_len):
    nb, rows, _ = q_ref.shape
    nc = kc_ref.shape[1]
    t_row = (_iota((rows, nc), 0) // REP) % s_len
    cidx = _iota((rows, nc), 1)
    maskc = (cidx * CMP_STRIDE + (CMP_BLOCK - 1)) <= past + t_row
    maskf = jnp.where(maskc, 1.0, 0.0)
    own = (_iota((rows, KV_DIM), 1) // HD) == _row_group((rows, KV_DIM))
    for i in range(nb):
        s = _dot_nt(q_ref[i], kc_ref[i].astype(BF16))
        sm = jnp.where(maskc, s, NEG)
        e = jnp.exp(sm - jnp.max(sm, axis=-1, keepdims=True))
        p = e / jnp.sum(e, axis=-1, keepdims=True) * maskf
        o = _dot(p.astype(BF16), vc_ref[i].astype(BF16))
        oc_ref[i] = jnp.where(own, o, 0.0)
        psum_ref[i] = _dot(ssum_ref[...], p, HI)


def _cmp_attn_sample(qbd, kc, vc, ssum, past, s_len):
    dbz, rows, _ = qbd.shape
    nc = kc.shape[1]
    ng = ssum.shape[0]
    nb = math.gcd(dbz, 4)
    per_b = lambda b: (b, 0, 0)
    return pl.pallas_call(
        functools.partial(_cmp_attn_sample_kernel, past=past, s_len=s_len),
        grid=(dbz // nb,),
        in_specs=[
            pl.BlockSpec((nb, rows, KV_DIM), per_b),
            pl.BlockSpec((nb, nc, KV_DIM), per_b),
            pl.BlockSpec((nb, nc, KV_DIM), per_b),
            pl.BlockSpec(ssum.shape, lambda b: (0, 0)),
        ],
        out_specs=(pl.BlockSpec((nb, rows, KV_DIM), per_b), pl.BlockSpec((nb, ng, nc), per_b)),
        out_shape=(jax.ShapeDtypeStruct((dbz, rows, KV_DIM), F32),
                   jax.ShapeDtypeStruct((dbz, ng, nc), F32)),
        compiler_params=_params(("parallel",)),
        name="cmp_attn_sample",
    )(qbd, kc, vc, ssum)


def _select_sample_kernel(psum_ref, ovl_ref, selt_ref, score_buf, *, past, s_len, n_blocks):
    imp_t = _dot_nt(ovl_ref[...], psum_ref[...], HI)
    tpos = past + _iota(imp_t.shape, 1) % s_len
    selt_ref[...] = _select_blocks(imp_t, tpos, n_blocks, score_buf)


def _select_sample(psum_all, ovl_t, past, s_len, n_blocks):
    nsp = ovl_t.shape[0]
    cols = psum_all.shape[0]
    return pl.pallas_call(
        functools.partial(_select_sample_kernel, past=past, s_len=s_len, n_blocks=n_blocks),
        out_shape=jax.ShapeDtypeStruct((nsp, cols), F32),
        scratch_shapes=[pltpu.VMEM((nsp, cols), F32)],
        compiler_params=pltpu.CompilerParams(vmem_limit_bytes=VMEM_LIMIT),
        name="select_sample",
    )(psum_all, ovl_t)


def _online_update(state, sm, v_t):
    m, l, acc = state
    m_new = jnp.maximum(m, jnp.max(sm, axis=-1, keepdims=True))
    alpha = jnp.exp(m - m_new)
    e = jnp.exp(sm - m_new)
    l = alpha * l + jnp.sum(e, axis=-1, keepdims=True)
    return m_new, l, alpha * acc + _dot_nt(e.astype(BF16), v_t)


def _sel_win_sample_kernel(pt_ref, kcache_ref, vcache_ref, q_ref, sel_ref, e_ref, kt_ref, vt_ref,
                           kw_ref, vw_ref, kwn_ref, vwn_ref, oc_ref, gate_ref, o_ref, kbuf, vbuf, ksem, vsem,
                           *, n_pages, s_len, n_chunks, chunk):
    slot = _gather_pages(pt_ref, [(kcache_ref, kbuf, ksem), (vcache_ref, vbuf, vsem)], n_pages)

    q = q_ref[0]
    rows = q.shape[0]
    sel = sel_ref[0]
    sel_main = sel[:, :LANES]
    past = n_pages * PAGE
    t_col = (_iota((rows, 1), 0) // REP) % s_len
    pages_pc = chunk // PAGE

    def chunk_t(buf, c):
        return jnp.concatenate([buf[slot, (c * pages_pc + j) * PAGE:(c * pages_pc + j + 1) * PAGE, :]
                                for j in range(pages_pc)], axis=1).astype(BF16)

    state = (jnp.full((rows, 1), NEG, F32), jnp.zeros((rows, 1), F32), jnp.zeros((rows, KV_DIM), F32))
    for c in range(n_chunks):
        blk = _dot(sel_main, e_ref[c])
        kpos = c * chunk + _iota((rows, chunk), 1)
        ok = jnp.where(kpos <= past + t_col, blk, 0.0) > 0.5
        state = _online_update(state, jnp.where(ok, _dot(q, chunk_t(kbuf, c)), NEG), chunk_t(vbuf, c))
    lane_t = _iota((rows, kt_ref.shape[2]), 1)
    sel_new = sel[:, LANES:LANES + 1].astype(F32)
    ok = jnp.where(lane_t <= t_col, jnp.where(lane_t < s_len, sel_new, 0.0), 0.0) > 0.5
    _, l, acc = _online_update(state, jnp.where(ok, _dot(q, kt_ref[0].astype(BF16)), NEG),
                               vt_ref[0].astype(BF16))
    o_s = acc / l

    wb = kw_ref.shape[2]
    iw = _iota((rows, wb), 1)
    ok_c = iw > t_col + (wb - WINDOW)
    state = (jnp.full((rows, 1), NEG, F32), jnp.zeros((rows, 1), F32), jnp.zeros((rows, KV_DIM), F32))
    state = _online_update(state, jnp.where(ok_c, _dot(q, kw_ref[0].astype(BF16)), NEG), vw_ref[0].astype(BF16))
    ok_n = jnp.where(lane_t <= t_col, jnp.where(lane_t < s_len, 1.0, 0.0), 0.0) > 0.5
    _, l, acc = _online_update(state, jnp.where(ok_n, _dot(q, kwn_ref[0].astype(BF16)), NEG),
                               vwn_ref[0].astype(BF16))
    o_w = acc / l

    gates = jax.nn.sigmoid(gate_ref[0])
    o = gates[:, 0:1] * oc_ref[0] + gates[:, 1:2] * o_s + gates[:, 2:3] * o_w
    own = (_iota(o.shape, 1) // HD) == _row_group(o.shape)
    o = jnp.where(own, o, 0.0)
    o_ref[0] = o[:, :HD] + o[:, HD:]


def _sel_win_sample(page_table, kcache, vcache, qbd, selx, e4, ktail, vtail, kwin, vwin, kwnew, vwnew,
                    oc, graw, s_len):
    dbz, n_pages = page_table.shape
    rows = qbd.shape[1]
    n_chunks, _, chunk = e4.shape
    per_b = lambda b, pt: (b, 0, 0)
    blk = lambda a: pl.BlockSpec((1,) + a.shape[1:], per_b)
    grid_spec = pltpu.PrefetchScalarGridSpec(
        num_scalar_prefetch=1,
        grid=(dbz,),
        in_specs=[
            pl.BlockSpec(memory_space=pl.ANY),
            pl.BlockSpec(memory_space=pl.ANY),
            blk(qbd), blk(selx),
            pl.BlockSpec(e4.shape, lambda b, pt: (0, 0, 0)),
            blk(ktail), blk(vtail), blk(kwin), blk(vwin), blk(kwnew), blk(vwnew), blk(oc), blk(graw),
        ],
        out_specs=pl.BlockSpec((1, rows, HD), per_b),
        scratch_shapes=[
            pltpu.VMEM((2, n_pages * PAGE, PAGE), F32),
            pltpu.VMEM((2, n_pages * PAGE, PAGE), F32),
            pltpu.SemaphoreType.DMA((2,)),
            pltpu.SemaphoreType.DMA((2,)),
        ],
    )
    return pl.pallas_call(
        functools.partial(_sel_win_sample_kernel, n_pages=n_pages, s_len=s_len,
                          n_chunks=n_chunks, chunk=chunk),
        grid_spec=grid_spec,
        out_shape=jax.ShapeDtypeStruct((dbz, rows, HD), F32),
        compiler_params=_params(("arbitrary",)),
        name="sel_win_sample",
    )(page_table, kcache, vcache, qbd, selx, e4, ktail, vtail, kwin, vwin, kwnew, vwnew, oc, graw)


def _rope_tables(pos):
    half = HD // 2
    inv = ROPE_THETA ** (-np.arange(half, dtype=np.float64) / half)
    ang = pos.astype(np.float64)[:, None] * inv[None, :]
    cos = np.cos(ang)
    sin = np.sin(ang)
    reps = LANES // HD
    cos_l = np.tile(np.concatenate([cos, cos], axis=-1), (1, reps))
    sin_l = np.tile(np.concatenate([-sin, sin], axis=-1), (1, reps))
    return tuple(jnp.asarray(np.ascontiguousarray(a), F32) for a in (cos_l, sin_l, cos.T, sin.T))


def _overlap_t(nc, ncp, ns, nsp):
    c_start = np.arange(ncp) * CMP_STRIDE
    s_start = np.arange(nsp) * SEL_BLOCK
    ovl = ((c_start[None, :] + CMP_BLOCK > s_start[:, None]) & (c_start[None, :] < s_start[:, None] + SEL_BLOCK))
    ovl = ovl & (np.arange(ncp)[None, :] < nc) & (np.arange(nsp)[:, None] < ns)
    return jnp.asarray(ovl.astype(np.float32))


def _expander(n_rows, n_chunks, chunk, dtype=BF16):
    key_block = (np.arange(n_chunks)[:, None] * chunk + np.arange(chunk)[None, :]) // SEL_BLOCK
    e = key_block[:, None, :] == np.arange(n_rows)[None, :, None]
    return jnp.asarray(e.astype(np.float32)).astype(dtype)


def _page_permutation():
    rows_pp = PAGE // CMP_STRIDE
    r = np.arange(PAGE)
    src = (r % rows_pp) * CMP_STRIDE + r // rows_pp
    return jnp.asarray((src[:, None] == np.arange(PAGE)[None, :]).astype(np.float32)).astype(BF16)


def _round_up(x, m):
    return -(-x // m) * m


def _compress_weights(w1, w2, pe):
    eye = jnp.eye(N_KV, dtype=F32)
    halves = []
    for half in range(2):
        wh = w1[half * CMP_STRIDE:(half + 1) * CMP_STRIDE]
        big = jnp.einsum("sdh,gk->sgdkh", wh, eye)
        halves.append(big.reshape(CMP_STRIDE * KV_DIM, N_KV * CMP_HIDDEN))
    w1big = jnp.concatenate(halves, axis=1).astype(BF16)
    w2big = jnp.einsum("hd,gk->ghkd", w2, eye).reshape(N_KV * CMP_HIDDEN, KV_DIM).astype(BF16)
    pe_rows = []
    for half in range(2):
        ph = pe[half * CMP_STRIDE:(half + 1) * CMP_STRIDE]
        row = jnp.broadcast_to(ph[:, None, :], (CMP_STRIDE, N_KV, HD)).reshape(1, CMP_STRIDE * KV_DIM)
        pe_rows.append(jnp.broadcast_to(row, (SUBLANES, CMP_STRIDE * KV_DIM)))
    pe2 = jnp.concatenate(pe_rows, axis=0).astype(BF16)
    return _page_permutation(), w1big, w2big, pe2


def _layer_weights(w, l):
    sizes = [D_SSM, CONV_DIM, N_SSM_HEADS, D_NSA, N_KV_PROJ * KV_DIM, N_HEADS * 3]
    offs = np.cumsum([0] + sizes)
    w_in = w["w_in"][l]
    seg = lambda i: w_in[:, offs[i]:offs[i + 1]]
    pad = LANES - sizes[2] - sizes[5]
    w_in_r = jnp.concatenate([seg(0), seg(1), seg(3), seg(2), seg(5),
                              jnp.zeros((w_in.shape[0], pad), F32)], axis=1).astype(BF16)
    bias_misc = jnp.concatenate([jnp.zeros((sizes[2],), F32), w["b_gate"][l], jnp.zeros((pad,), F32)])[None, :]
    lane_pad = lambda v: jnp.concatenate([v, jnp.zeros((LANES - v.shape[0],), F32)])[None, :]
    lw = {
        "w_in": w_in_r,
        "w_kv_t": seg(4).T.astype(BF16),
        "bias_misc": bias_misc,
        "conv_w": w["conv_w"][l],
        "conv_b_row": w["conv_b"][l][None, :],
        "dtb_row": lane_pad(w["dt_bias"][l]),
        "alog_row": lane_pad(w["a_log"][l]),
        "dtb_col": w["dt_bias"][l][:, None],
        "alog_col": w["a_log"][l][:, None],
        "dskip_row": jnp.repeat(w["d_skip"][l], HD)[None, :],
        "normw_row": w["ssm_norm_w"][l][None, :],
        "w_out": w["w_out"][l].astype(BF16),
    }
    lw["cmp_k"] = _compress_weights(w["cmp_k_w1"][l], w["cmp_k_w2"][l], w["cmp_k_pe"][l])
    lw["cmp_v"] = _compress_weights(w["cmp_v_w1"][l], w["cmp_v_w2"][l], w["cmp_v_pe"][l])
    for i in (1, 2, 3):
        lw[f"ln{i}"] = (w[f"ln{i}_g"][l][None, :], w[f"ln{i}_b"][l][None, :])
    for i in (1, 2):
        lw[f"ffn{i}"] = (w[f"ffn{i}_w_gate"][l].astype(BF16), w[f"ffn{i}_w_up"][l].astype(BF16),
                         w[f"ffn{i}_w_down"][l].astype(BF16))
    return lw


def _heads_major(x, bsz, t, n):
    return x.reshape(bsz, t, n, HD).transpose(0, 2, 1, 3)


def _mix_prompt(h, lw, bsz, t):
    m = bsz * t
    z, xbc, misc, qh, kvtb, *kvt = _in_proj(h, lw["w_in"], lw["w_kv_t"], lw["bias_misc"],
                                            _rope_tables(np.arange(t)), bsz, t)
    kv_rows = [a.reshape(bsz, N_KV, HD, t).transpose(0, 3, 1, 2) for a in kvt]

    chunk = min(128, t)
    misc3 = misc.reshape(bsz, t, LANES)
    xbc3 = xbc.reshape(bsz, t, CONV_DIM)
    y_ssd, h_ssm = _ssd(z.reshape(bsz, t, D_SSM), xbc3, misc3, misc3.transpose(0, 2, 1), lw, chunk)
    conv_state = xbc3[:, t - (CONV_TAPS - 1):]

    n_str = t // CMP_STRIDE
    nc = n_str - 1
    ns = -(-t // SEL_BLOCK)
    nsp = _round_up(ns, SUBLANES)
    kvtb4 = kvtb.reshape(bsz, N_KV_PROJ, KV_DIM, t)
    kc = _compress(kvtb4, 0, *lw["cmp_k"])
    vc = _compress(kvtb4, 1, *lw["cmp_v"])
    kch = _heads_major(kc, bsz, n_str, N_KV).astype(BF16)
    vch = _heads_major(vc, bsz, n_str, N_KV).astype(BF16)
    tq = min(512, t)
    tk = min(512, t)
    e3 = _expander(nsp, t // tk, tk)
    o = _nsa_prompt(qh, kch, vch, kvtb.reshape(bsz, N_KV_PROJ * N_KV, HD, t), _overlap_t(nc, n_str, ns, nsp),
                    e3, misc3, ns, tq, tk)

    wb = min(WINDOW, t)
    state = tuple(kv_rows[:4]) + (kv_rows[4][:, t - wb:], kv_rows[5][:, t - wb:], h_ssm, conv_state)
    return y_ssd.reshape(m, D_SSM), o.reshape(m, D_NSA), state


def _mix_sample(h, lw, l, dbz, s_len, caches, state_ssm, state_conv, page_table):
    cache_k_cmp, cache_v_cmp, cache_k_slc, cache_v_slc, cache_k_win, cache_v_win = caches
    m = dbz * s_len
    n_pages = page_table.shape[1]
    past = n_pages * PAGE
    pos = past + np.arange(s_len)
    z, xbc, misc, qh, _, *kvt = _in_proj(h, lw["w_in"], lw["w_kv_t"], lw["bias_misc"],
                                         _rope_tables(np.tile(pos, dbz)), 1, m)
    kvs = [a.reshape(KV_DIM, dbz, s_len) for a in kvt]
    new_rows = [kvs[i].transpose(1, 2, 0) for i in range(N_KV_PROJ)]
    new_t = [kvs[i].transpose(1, 0, 2) for i in range(N_KV_PROJ)]
    k_c, v_c, k_s, v_s, k_w, v_w = [r.reshape(dbz, s_len, N_KV, HD) for r in new_rows]

    misc3 = misc.reshape(dbz, s_len, LANES)
    xbc3 = xbc.reshape(dbz, s_len, CONV_DIM)
    xh = jnp.concatenate([state_conv[l], xbc3], axis=1)
    y_t, h_new = _ssd_sample(xh.transpose(1, 2, 0), z.reshape(dbz, s_len, D_SSM).transpose(1, 2, 0),
                             misc3[:, :, :N_SSM_HEADS].transpose(1, 2, 0), state_ssm[l].transpose(1, 2, 3, 0),
                             lw, dbz, s_len)
    y_ssd = y_t.transpose(2, 0, 1)
    h_ssm = h_new.transpose(3, 0, 1, 2)
    conv_state = xh[:, -(CONV_TAPS - 1):]

    width = CMP_STRIDE * KV_DIM
    n_pool = cache_k_cmp.shape[1]
    pages_t = lambda cache: cache[l].transpose(0, 2, 3, 1).reshape(n_pool, KV_DIM, PAGE)

    def tail_rows(new):
        flat = new.reshape(dbz, 1, s_len * KV_DIM)
        return jnp.pad(flat, ((0, 0), (0, SUBLANES - 1), (0, width - s_len * KV_DIM)))

    kc, vc = _compress_paged(page_table, pages_t(cache_k_cmp), pages_t(cache_v_cmp),
                             tail_rows(new_rows[0]), tail_rows(new_rows[1]), lw["cmp_k"], lw["cmp_v"])
    nc = kc.shape[1]
    total = past + s_len
    ns = -(-total // SEL_BLOCK)
    nsp = _round_up(ns, SUBLANES)

    rows = N_KV * s_len * REP
    qg = qh.reshape(N_KV, REP, dbz, s_len, HD).transpose(2, 0, 3, 1, 4)
    qbd = jnp.einsum("bgtrd,gk->bgtrkd", qg, jnp.eye(N_KV, dtype=BF16)).reshape(dbz, rows, KV_DIM)
    ng = N_KV * s_len
    ssum = jnp.asarray((np.arange(ng)[:, None] == np.arange(rows)[None, :] // REP).astype(np.float32))
    oc, psum = _cmp_attn_sample(qbd, kc, vc, ssum, past, s_len)
    selt = _select_sample(psum.reshape(dbz * ng, nc), _overlap_t(nc, nc, ns, nsp), past, s_len, ns)
    sel = selt.T.reshape(dbz, ng, 1, nsp)
    sel = jnp.broadcast_to(sel, (dbz, ng, REP, nsp)).reshape(dbz, rows, nsp)
    n_cached = past // SEL_BLOCK
    selx = jnp.concatenate([sel[:, :, :n_cached], jnp.zeros((dbz, rows, LANES - n_cached), F32),
                            sel[:, :, n_cached:n_cached + 1], jnp.zeros((dbz, rows, LANES - 1), F32)],
                           axis=-1).astype(BF16)
    chunk_k = min(2048, past)
    e4 = _expander(LANES, past // chunk_k, chunk_k)
    pad_lanes = lambda a: jnp.pad(a, ((0, 0), (0, 0), (0, LANES - s_len))).astype(BF16)
    wb = cache_k_win.shape[2]
    win_t = lambda cache: cache[l].transpose(0, 2, 3, 1).reshape(dbz, KV_DIM, wb)
    gate = misc3[:, :, N_SSM_HEADS:N_SSM_HEADS + 3 * N_HEADS].reshape(dbz, s_len, N_KV, REP, 3)
    graw = jnp.pad(gate.transpose(0, 2, 1, 3, 4).reshape(dbz, rows, 3), ((0, 0), (0, 0), (0, LANES - 3)))
    o = _sel_win_sample(page_table, pages_t(cache_k_slc), pages_t(cache_v_slc), qbd, selx, e4,
                        pad_lanes(new_t[2]), pad_lanes(new_t[3]), win_t(cache_k_win), win_t(cache_v_win),
                        pad_lanes(new_t[4]), pad_lanes(new_t[5]), oc, graw, s_len)
    o_nsa = o.reshape(dbz, N_KV, s_len, REP, HD).transpose(0, 2, 1, 3, 4).reshape(m, D_NSA).astype(BF16)

    kw_full = jnp.concatenate([cache_k_win[l], k_w], axis=1)
    vw_full = jnp.concatenate([cache_v_win[l], v_w], axis=1)
    state = (k_c, v_c, k_s, v_s, kw_full[:, -wb:], vw_full[:, -wb:], h_ssm, conv_state)
    return y_ssd.reshape(m, D_SSM), o_nsa, state


def _layer(x, lw, alpha, mix_fn):
    h1 = _ffn_ln(x, *lw["ffn1"], *lw["ln1"], alpha)
    y_ssd, o_nsa, state = mix_fn(h1)
    h2 = _out_proj_ln(y_ssd, o_nsa, h1, lw["w_out"], *lw["ln2"], alpha)
    return _ffn_ln(h2, *lw["ffn2"], *lw["ln3"], alpha), state


def kernel(x_prompt, x_sample, cache_k_cmp, cache_v_cmp, cache_k_slc, cache_v_slc, cache_k_win, cache_v_win, state_ssm, state_conv, page_table, w_in, b_gate, conv_w, conv_b, dt_bias, a_log, d_skip, ssm_norm_w, cmp_k_w1, cmp_k_w2, cmp_k_pe, cmp_v_w1, cmp_v_w2, cmp_v_pe, w_out, ln1_g, ln1_b, ln2_g, ln2_b, ln3_g, ln3_b, ffn1_w_gate, ffn1_w_up, ffn1_w_down, ffn2_w_gate, ffn2_w_up, ffn2_w_down):
    weights = dict(w_in=w_in, b_gate=b_gate, conv_w=conv_w, conv_b=conv_b, dt_bias=dt_bias, a_log=a_log,
                   d_skip=d_skip, ssm_norm_w=ssm_norm_w, cmp_k_w1=cmp_k_w1, cmp_k_w2=cmp_k_w2,
                   cmp_k_pe=cmp_k_pe, cmp_v_w1=cmp_v_w1, cmp_v_w2=cmp_v_w2, cmp_v_pe=cmp_v_pe, w_out=w_out,
                   ln1_g=ln1_g, ln1_b=ln1_b, ln2_g=ln2_g, ln2_b=ln2_b, ln3_g=ln3_g, ln3_b=ln3_b,
                   ffn1_w_gate=ffn1_w_gate, ffn1_w_up=ffn1_w_up, ffn1_w_down=ffn1_w_down,
                   ffn2_w_gate=ffn2_w_gate, ffn2_w_up=ffn2_w_up, ffn2_w_down=ffn2_w_down)
    depth = w_in.shape[0]
    bsz, t, d = x_prompt.shape
    dbz, s_len, _ = x_sample.shape
    alpha = (2.0 * depth) ** 0.25
    caches = (cache_k_cmp, cache_v_cmp, cache_k_slc, cache_v_slc, cache_k_win, cache_v_win)
    y_p = x_prompt.reshape(bsz * t, d)
    y_s = x_sample.reshape(dbz * s_len, d)
    p_states, s_states = [], []
    for l in range(depth):
        lw = _layer_weights(weights, l)
        y_p, st_p = _layer(y_p, lw, alpha, lambda h: _mix_prompt(h, lw, bsz, t))
        y_s, st_s = _layer(y_s, lw, alpha, lambda h: _mix_sample(h, lw, l, dbz, s_len, caches, state_ssm,
                                                                 state_conv, page_table))
        p_states.append(st_p)
        s_states.append(st_s)
    p_st = [jnp.stack(a) for a in zip(*p_states)]
    s_st = [jnp.stack(a) for a in zip(*s_states)]
    outs = [y_p.reshape(bsz, t, d), y_s.reshape(dbz, s_len, d)]
    for p, s in zip(p_st[:6], s_st[:6]):
        outs += [p, s]
    outs += [p_st[6], s_st[6], p_st[7], s_st[7]]
    return tuple(outs)
```

```python
import functools
import math

import numpy as np
import jax
import jax.numpy as jnp
from jax import lax
from jax.experimental import pallas as pl
from jax.experimental.pallas import tpu as pltpu

F32 = jnp.float32
BF16 = jnp.bfloat16
HI = lax.Precision.HIGHEST

HD = 64
N_SSM_HEADS = 8
N_SSM_GROUPS = 2
D_SSM = 512
CONV_DIM = 768
CONV_TAPS = 4
N_HEADS = 8
N_KV = 2
REP = N_HEADS // N_KV
D_NSA = 512
KV_DIM = N_KV * HD
N_KV_PROJ = 6
CMP_STRIDE = 16
CMP_BLOCK = 32
CMP_HIDDEN = 128
SEL_BLOCK = 64
SEL_TOP_N = 16
WINDOW = 512
PAGE = 128
ROPE_THETA = 10000.0
LN_EPS = 1e-5
RMS_EPS = 1e-5
NEG = -1e30
FORCED_SCORE = 1e30
INVALID_SCORE = -1.0

LANES = 128
SUBLANES = 8
VMEM_LIMIT = 56 * 1024 * 1024

NT_DIMS = (((1,), (1,)), ((), ()))
TN_DIMS = (((0,), (0,)), ((), ()))


def _params(sem):
    return pltpu.CompilerParams(dimension_semantics=sem, vmem_limit_bytes=VMEM_LIMIT)


def _dot(a, b, precision=None):
    return jnp.dot(a, b, preferred_element_type=F32, precision=precision)


def _dot_nt(a, b, precision=None):
    return lax.dot_general(a, b, NT_DIMS, preferred_element_type=F32, precision=precision)


def _dot_tn(a, b, precision=None):
    return lax.dot_general(a, b, TN_DIMS, preferred_element_type=F32, precision=precision)


def _iota(shape, dim):
    return lax.broadcasted_iota(jnp.int32, shape, dim)


def _layer_norm(y, g, b):
    mu = jnp.mean(y, axis=-1, keepdims=True)
    yc = y - mu
    var = jnp.mean(yc * yc, axis=-1, keepdims=True)
    return yc * lax.rsqrt(var + LN_EPS) * g + b


def _ffn_ln_kernel(x_ref, wg_ref, wu_ref, wd_ref, g_ref, b_ref, o_ref, y_ref, *, alpha, n_split, skip_flush_matmuls):
    step = pl.program_id(0)

    @pl.when(step == 0)
    def _():
        y_ref[...] = jnp.zeros_like(y_ref)

    o_ref[...] = _layer_norm(y_ref[...], g_ref[...], b_ref[...])

    def matmuls():
        x = x_ref[...]
        xb = x.astype(BF16)
        tf = wg_ref.shape[1] // n_split
        acc = None
        for s in range(n_split):
            gate = _dot(xb, wg_ref[:, s * tf:(s + 1) * tf])
            up = _dot(xb, wu_ref[:, s * tf:(s + 1) * tf])
            act = (jax.nn.silu(gate) * up).astype(BF16)
            part = _dot(act, wd_ref[s * tf:(s + 1) * tf, :])
            acc = part if acc is None else acc + part
        y_ref[...] = alpha * x + 0.5 * acc

    if skip_flush_matmuls:
        pl.when(step < pl.num_programs(0) - 1)(matmuls)
    else:
        matmuls()


def _ffn_ln(x, wg, wu, wd, g, b, alpha):
    m, d = x.shape
    dff = wg.shape[1]
    tm = min(512, m)
    n_i = m // tm
    n_split = 2 if (dff // 2) % LANES == 0 else 1
    resident = lambda a: pl.BlockSpec(a.shape, lambda i: (0, 0), pipeline_mode=pl.Buffered(1))
    return pl.pallas_call(
        functools.partial(_ffn_ln_kernel, alpha=alpha, n_split=n_split, skip_flush_matmuls=n_i < 8),
        grid=(n_i + 1,),
        in_specs=[
            pl.BlockSpec((tm, d), lambda i: (jnp.minimum(i, n_i - 1), 0)),
            resident(wg), resident(wu), resident(wd),
            pl.BlockSpec((1, d), lambda i: (0, 0)),
            pl.BlockSpec((1, d), lambda i: (0, 0)),
        ],
        out_specs=pl.BlockSpec((tm, d), lambda i: (jnp.maximum(i - 1, 0), 0)),
        out_shape=jax.ShapeDtypeStruct((m, d), F32),
        scratch_shapes=[pltpu.VMEM((tm, d), F32)],
        compiler_params=_params(("arbitrary",)),
        name="ffn_ln",
    )(x, wg, wu, wd, g, b)


OFF_Z, OFF_XBC, OFF_Q, OFF_MISC, W_IN_COLS = 0, 512, 1280, 1792, 1920


def _in_proj_kernel(h_ref, w_ref, wkv_ref, bias_ref, cos_ref, sin_ref, cost_ref, sint_ref,
                    z_ref, xbc_ref, misc_ref, qh_ref, kvtb_ref, *kvt_refs, q_scale):
    hb = h_ref[...].astype(BF16)
    z_ref[...] = _dot(hb, w_ref[:, OFF_Z:OFF_XBC])
    xbc_ref[...] = _dot(hb, w_ref[:, OFF_XBC:OFF_Q])
    qm = _dot(hb, w_ref[:, OFF_Q:W_IN_COLS])
    misc_ref[...] = qm[:, OFF_MISC - OFF_Q:] + bias_ref[...]
    cos = cos_ref[...]
    sin = sin_ref[...]
    first_half = (_iota(cos.shape, 1) & (HD - 1)) < (HD // 2)
    for c in range(D_NSA // LANES):
        x = qm[:, c * LANES:(c + 1) * LANES]
        rot = jnp.where(first_half, pltpu.roll(x, LANES - HD // 2, 1), pltpu.roll(x, HD // 2, 1))
        q = ((x * cos + rot * sin) * q_scale).astype(BF16)
        for j in range(LANES // HD):
            qh_ref[0, c * (LANES // HD) + j] = q[:, j * HD:(j + 1) * HD]
    kvt = _dot_nt(wkv_ref[...], hb)
    cost = cost_ref[...]
    sint = sint_ref[...]
    half = HD // 2
    for i in range(N_KV_PROJ):
        blk = kvt[i * KV_DIM:(i + 1) * KV_DIM, :]
        if i % 2 == 0:
            parts = []
            for g in range(N_KV):
                x1 = blk[g * HD:g * HD + half, :]
                x2 = blk[g * HD + half:(g + 1) * HD, :]
                parts += [x1 * cost - x2 * sint, x2 * cost + x1 * sint]
            blk = jnp.concatenate(parts, axis=0)
        kvt_refs[i][0] = blk
        kvtb_ref[0, i * KV_DIM:(i + 1) * KV_DIM, :] = blk.astype(BF16)


def _in_proj(h, w, wkv, bias, tables, n_seq, t_seq):
    cos, sin, cost, sint = tables
    m, d = h.shape
    tm = min(512, t_seq)
    n_pos = t_seq // tm
    row = lambda i: (i, 0)
    fixed = lambda i: (0, 0)
    out_shapes = (
        jax.ShapeDtypeStruct((m, D_SSM), F32),
        jax.ShapeDtypeStruct((m, CONV_DIM), F32),
        jax.ShapeDtypeStruct((m, LANES), F32),
        jax.ShapeDtypeStruct((n_seq, N_HEADS, t_seq, HD), BF16),
        jax.ShapeDtypeStruct((n_seq, N_KV_PROJ * KV_DIM, t_seq), BF16),
    ) + (jax.ShapeDtypeStruct((n_seq, KV_DIM, t_seq), F32),) * N_KV_PROJ
    seq_t = lambda i: (i // n_pos, 0, i % n_pos)
    return pl.pallas_call(
        functools.partial(_in_proj_kernel, q_scale=HD ** -0.5),
        grid=(m // tm,),
        in_specs=[
            pl.BlockSpec((tm, d), row),
            pl.BlockSpec(w.shape, fixed),
            pl.BlockSpec(wkv.shape, fixed),
            pl.BlockSpec((1, LANES), fixed),
            pl.BlockSpec((tm, LANES), lambda i: (i % n_pos, 0)),
            pl.BlockSpec((tm, LANES), lambda i: (i % n_pos, 0)),
            pl.BlockSpec((HD // 2, tm), lambda i: (0, i % n_pos)),
            pl.BlockSpec((HD // 2, tm), lambda i: (0, i % n_pos)),
        ],
        out_specs=(
            pl.BlockSpec((tm, D_SSM), row),
            pl.BlockSpec((tm, CONV_DIM), row),
            pl.BlockSpec((tm, LANES), row),
            pl.BlockSpec((1, N_HEADS, tm, HD), lambda i: (i // n_pos, 0, i % n_pos, 0)),
            pl.BlockSpec((1, N_KV_PROJ * KV_DIM, tm), seq_t),
        ) + (pl.BlockSpec((1, KV_DIM, tm), seq_t),) * N_KV_PROJ,
        out_shape=out_shapes,
        compiler_params=_params(("parallel",)),
        name="in_proj",
    )(h, w, wkv, bias, cos, sin, cost, sint)


def _ssd_kernel(z_ref, xbc_ref, misc_ref, misct_ref, convw_ref, convb_ref,
                dtb_row_ref, alog_row_ref, dtb_col_ref, alog_col_ref, dskip_ref, normw_ref,
                y_ref, hout_ref, xbuf, hstate, ybuf, *, chunk, nseq):
    c = pl.program_id(1)

    @pl.when(c == 0)
    def _():
        xbuf[:, 0:SUBLANES, :] = jnp.zeros((nseq, SUBLANES, CONV_DIM), F32)
        hstate[...] = jnp.zeros_like(hstate)

    for i in range(nseq):
        _ssd_chunk(z_ref.at[i], xbc_ref.at[i], misc_ref.at[i], misct_ref.at[i], convw_ref, convb_ref,
                   dtb_row_ref, alog_row_ref, dtb_col_ref, alog_col_ref, dskip_ref, normw_ref,
                   y_ref.at[i], xbuf.at[i], hstate.at[i], ybuf.at[i], chunk)

    @pl.when(c == pl.num_programs(1) - 1)
    def _():
        hout_ref[...] = hstate[...]


def _ssd_chunk(z_ref, xbc_ref, misc_ref, misct_ref, convw_ref, convb_ref,
               dtb_row_ref, alog_row_ref, dtb_col_ref, alog_col_ref, dskip_ref, normw_ref,
               y_ref, xbuf, hstate, ybuf, chunk):
    L = chunk
    H = SUBLANES
    x = xbc_ref[...]
    xbuf[H:H + L, :] = x
    conv = convb_ref[...] + convw_ref[CONV_TAPS - 1:CONV_TAPS, :] * x
    for k in range(1, CONV_TAPS):
        conv = conv + convw_ref[CONV_TAPS - 1 - k:CONV_TAPS - k, :] * xbuf[H - k:H - k + L, :]
    xbuf[0:H, :] = x[L - H:L, :]
    xact = jax.nn.silu(conv)
    bm = xact[:, D_SSM:D_SSM + N_SSM_GROUPS * HD]
    cm = xact[:, D_SSM + N_SSM_GROUPS * HD:]

    dt = jax.nn.softplus(misc_ref[...] + dtb_row_ref[...])
    dtt = jax.nn.softplus(misct_ref[0:SUBLANES, :] + dtb_col_ref[...])
    a_row = -jnp.exp(alog_row_ref[...])
    a_col = -jnp.exp(alog_col_ref[...])
    lane_ok = _iota(dt.shape, 1) < N_SSM_HEADS
    da = jnp.where(lane_ok, dt * a_row, 0.0)
    dat = dtt * a_col
    ri = _iota((L, L), 0)
    ci = _iota((L, L), 1)
    causal = ri >= ci
    tri = jnp.where(causal, 1.0, 0.0).astype(F32)
    cum = _dot(tri, da, HI)
    cumt = _dot_nt(dat, tri, HI)
    cum_last = cum[L - 1:L, :]

    rep = N_SSM_HEADS // N_SSM_GROUPS
    for g in range(N_SSM_GROUPS):
        cm_g = cm[:, g * HD:(g + 1) * HD].astype(BF16)
        bm_g = bm[:, g * HD:(g + 1) * HD].astype(BF16)
        cb = _dot_nt(cm_g, bm_g)
        for r in range(rep):
            h = g * rep + r
            ch = cum[:, h:h + 1]
            seg = ch - cumt[h:h + 1, :]
            decay = jnp.where(causal, jnp.exp(jnp.where(causal, seg, 0.0)), 0.0)
            xs_h = xact[:, h * HD:(h + 1) * HD]
            xdt = xs_h * dt[:, h:h + 1]
            hprev = hstate[h]
            y_h = (_dot((cb * decay).astype(BF16), xdt.astype(BF16))
                   + _dot_nt(cm_g, hprev.astype(BF16)) * jnp.exp(ch))
            cl = cum_last[:, h:h + 1]
            tail = jnp.exp(cl - ch)
            hstate[h] = hprev * jnp.exp(cl) + _dot_tn((xdt * tail).astype(BF16), bm_g)
            ybuf[:, h * HD:(h + 1) * HD] = y_h

    xs = xact[:, :D_SSM]
    y = (ybuf[...] + dskip_ref[...] * xs) * jax.nn.silu(z_ref[...])
    y = y * lax.rsqrt(jnp.mean(y * y, axis=-1, keepdims=True) + RMS_EPS) * normw_ref[...]
    y_ref[...] = y


def _ssd(z, xbc, misc, misct, lw, chunk):
    bsz, t, _ = z.shape
    nc = t // chunk
    nseq = 1
    seq = lambda b, c: (b, c, 0)
    per_b4 = lambda b, c: (b, 0, 0, 0)
    fixed = lambda b, c: (0, 0)
    return pl.pallas_call(
        functools.partial(_ssd_kernel, chunk=chunk, nseq=nseq),
        grid=(bsz // nseq, nc),
        in_specs=[
            pl.BlockSpec((nseq, chunk, D_SSM), seq),
            pl.BlockSpec((nseq, chunk, CONV_DIM), seq),
            pl.BlockSpec((nseq, chunk, LANES), seq),
            pl.BlockSpec((nseq, LANES, chunk), lambda b, c: (b, 0, c)),
            pl.BlockSpec((CONV_TAPS, CONV_DIM), fixed),
            pl.BlockSpec((1, CONV_DIM), fixed),
            pl.BlockSpec((1, LANES), fixed),
            pl.BlockSpec((1, LANES), fixed),
            pl.BlockSpec((SUBLANES, 1), fixed),
            pl.BlockSpec((SUBLANES, 1), fixed),
            pl.BlockSpec((1, D_SSM), fixed),
            pl.BlockSpec((1, D_SSM), fixed),
        ],
        out_specs=(
            pl.BlockSpec((nseq, chunk, D_SSM), seq),
            pl.BlockSpec((nseq, N_SSM_HEADS, HD, HD), per_b4),
        ),
        out_shape=(
            jax.ShapeDtypeStruct((bsz, t, D_SSM), F32),
            jax.ShapeDtypeStruct((bsz, N_SSM_HEADS, HD, HD), F32),
        ),
        scratch_shapes=[
            pltpu.VMEM((nseq, SUBLANES + chunk, CONV_DIM), F32),
            pltpu.VMEM((nseq, N_SSM_HEADS, HD, HD), F32),
            pltpu.VMEM((nseq, chunk, D_SSM), F32),
        ],
        compiler_params=_params(("parallel", "arbitrary")),
        name="ssd",
    )(z, xbc, misc, misct, lw["conv_w"], lw["conv_b_row"], lw["dtb_row"], lw["alog_row"],
      lw["dtb_col"], lw["alog_col"], lw["dskip_row"], lw["normw_row"])


def _ssd_sample_kernel(xh_ref, z_ref, dtraw_ref, state_ref, convw_ref, convb_ref, dtb_ref, alog_ref,
                       dskip_ref, normw_ref, y_ref, hout_ref, xact, ypre, *, s_len):
    h = pl.program_id(0)

    @pl.when(h == 0)
    def _():
        for t in range(s_len):
            conv = convb_ref[...]
            for k in range(CONV_TAPS):
                conv = conv + convw_ref[k] * xh_ref[t + k]
            xact[t] = jax.nn.silu(conv)

    a = -jnp.exp(alog_ref[pl.ds(h, 1), :])
    dts = [jax.nn.softplus(dtraw_ref[t, pl.ds(h, 1), :] + dtb_ref[pl.ds(h, 1), :]) for t in range(s_len)]
    das = [jnp.exp(dt * a) for dt in dts]
    g = h // (N_SSM_HEADS // N_SSM_GROUPS)
    b_lo = pl.multiple_of(D_SSM + g * HD, HD)
    c_lo = pl.multiple_of(D_SSM + N_SSM_GROUPS * HD + g * HD, HD)
    bs = [xact[t, pl.ds(b_lo, HD), :] for t in range(s_len)]
    cs = [xact[t, pl.ds(c_lo, HD), :] for t in range(s_len)]

    def body(p, carry):
        hs = state_ref[0, p]
        row = h * HD + p
        for t in range(s_len):
            x = xact[t, pl.ds(row, 1), :]
            hs = hs * das[t] + (x * dts[t]) * bs[t]
            y = jnp.sum(cs[t] * hs, axis=0, keepdims=True)
            ypre[t, pl.ds(row, 1), :] = y + dskip_ref[pl.ds(row, 1), :] * x
        hout_ref[0, p] = hs
        return carry

    lax.fori_loop(0, HD, body, 0)

    @pl.when(h == pl.num_programs(0) - 1)
    def _():
        for t in range(s_len):
            y = ypre[t] * jax.nn.silu(z_ref[t])
            y = y * lax.rsqrt(jnp.mean(y * y, axis=0, keepdims=True) + RMS_EPS) * normw_ref[...]
            y_ref[t] = y


def _ssd_sample(xh, zt, dtraw, state, lw, dbz, s_len):
    lane_b = lambda v: jnp.broadcast_to(v[..., None], v.shape + (dbz,))
    full = lambda a: pl.BlockSpec(a.shape, lambda h: (0,) * a.ndim)
    args = (xh, zt, dtraw, state, lane_b(lw["conv_w"]), lane_b(lw["conv_b_row"][0]), lane_b(lw["dtb_col"][:, 0]),
            lane_b(lw["alog_col"][:, 0]), lane_b(lw["dskip_row"][0]), lane_b(lw["normw_row"][0]))
    state_spec = pl.BlockSpec((1, HD, HD, dbz), lambda h: (h, 0, 0, 0))
    in_specs = [full(a) for a in args]
    in_specs[3] = state_spec
    return pl.pallas_call(
        functools.partial(_ssd_sample_kernel, s_len=s_len),
        grid=(N_SSM_HEADS,),
        in_specs=in_specs,
        out_specs=(pl.BlockSpec((s_len, D_SSM, dbz), lambda h: (0, 0, 0)), state_spec),
        out_shape=(jax.ShapeDtypeStruct((s_len, D_SSM, dbz), F32),
                   jax.ShapeDtypeStruct(state.shape, F32)),
        scratch_shapes=[pltpu.VMEM((s_len, CONV_DIM, dbz), F32), pltpu.VMEM((s_len, D_SSM, dbz), F32)],
        compiler_params=_params(("arbitrary",)),
        name="ssd_sample",
    )(*args)


CMP_IN = CMP_STRIDE * HD


def _page_group(n_pages):
    return math.gcd(n_pages, 8)


def _compress_pages(get_pages, n_pages, tail_rows, pm_ref, w1_ref, w2_ref, pe_ref, x_buf, sec_buf):
    rows_pp = PAGE // CMP_STRIDE
    n = n_pages * rows_pp
    nr = n + SUBLANES
    group = _page_group(n_pages)
    low = _iota((rows_pp, KV_DIM), 1) < HD

    for i in range(n_pages // group):
        perm = _dot_nt(pm_ref[...], get_pages(i))
        for j in range(group):
            r0 = (i * group + j) * rows_pp
            for b in range(CMP_STRIDE // 2):
                even = perm[2 * b * rows_pp:(2 * b + 1) * rows_pp, j * KV_DIM:(j + 1) * KV_DIM]
                odd = perm[(2 * b + 1) * rows_pp:(2 * b + 2) * rows_pp, j * KV_DIM:(j + 1) * KV_DIM]
                x_buf[0, r0:r0 + rows_pp, b * LANES:(b + 1) * LANES] = jnp.where(low, even, pltpu.roll(odd, HD, 1))
                x_buf[1, r0:r0 + rows_pp, b * LANES:(b + 1) * LANES] = jnp.where(low, pltpu.roll(even, HD, 1), odd)
    x_buf[:, n:nr, :] = tail_rows
    out = _dot(x_buf[...].reshape(N_KV * nr, CMP_IN).astype(BF16), w1_ref[...])
    sec_buf[...] = out[:, CMP_HIDDEN:]
    pe_out = _dot(pe_ref[...], w1_ref[...])
    pe_term = pe_out[0:1, :CMP_HIDDEN] + pe_out[SUBLANES:SUBLANES + 1, CMP_HIDDEN:]
    pre = jnp.concatenate([out[g * nr:g * nr + n, :CMP_HIDDEN] + sec_buf[g * nr + 1:g * nr + n + 1, :] + pe_term
                           for g in range(N_KV)], axis=1)
    return _dot(jax.nn.gelu(pre).astype(BF16), w2_ref[...])


def _compress_scratch(n):
    return [pltpu.VMEM((N_KV, n + SUBLANES, CMP_IN), F32), pltpu.VMEM((N_KV * (n + SUBLANES), CMP_HIDDEN), F32)]


def _compress_kernel(x_ref, pm_ref, w1_ref, w2_ref, pe_ref, o_ref, x_buf, sec_buf, *, n_pages):
    group = _page_group(n_pages)

    def get_pages(i):
        wide = x_ref[0, 0, :, i * group * PAGE:(i + 1) * group * PAGE]
        return jnp.concatenate([wide[:, j * PAGE:(j + 1) * PAGE] for j in range(group)], axis=0)

    tail_rows = jnp.zeros((N_KV, SUBLANES, CMP_IN), F32)
    o_ref[0] = _compress_pages(get_pages, n_pages, tail_rows, pm_ref, w1_ref, w2_ref, pe_ref, x_buf, sec_buf)


def _compress(kvtb4, proj, pm, w1big, w2big, pe2):
    bsz, _, _, t = kvtb4.shape
    n_pages = t // PAGE
    n = t // CMP_STRIDE
    fixed = lambda b: (0, 0)
    return pl.pallas_call(
        functools.partial(_compress_kernel, n_pages=n_pages),
        grid=(bsz,),
        in_specs=[
            pl.BlockSpec((1, 1, KV_DIM, t), lambda b: (b, proj, 0, 0)),
            pl.BlockSpec(pm.shape, fixed),
            pl.BlockSpec(w1big.shape, fixed),
            pl.BlockSpec(w2big.shape, fixed),
            pl.BlockSpec(pe2.shape, fixed),
        ],
        out_specs=pl.BlockSpec((1, n, KV_DIM), lambda b: (b, 0, 0)),
        out_shape=jax.ShapeDtypeStruct((bsz, n, KV_DIM), F32),
        scratch_shapes=_compress_scratch(n),
        compiler_params=_params(("parallel",)),
        name="compress",
    )(kvtb4, pm, w1big, w2big, pe2)


def _page_copy(cache_ref, buf, sem, page, slot, p):
    return pltpu.make_async_copy(cache_ref.at[page], buf.at[slot, pl.ds(p * PAGE, PAGE)], sem.at[slot])


def _gather_pages(pt_ref, streams, n_pages):
    b = pl.program_id(0)
    slot = b % 2

    def issue(seq, into):
        def start(p, carry):
            page = pt_ref[seq, p]
            for cache_ref, buf, sem in streams:
                _page_copy(cache_ref, buf, sem, page, into, p).start()
            return carry
        lax.fori_loop(0, n_pages, start, 0, unroll=_page_group(n_pages))

    @pl.when(b == 0)
    def _():
        issue(b, slot)

    @pl.when(b + 1 < pl.num_programs(0))
    def _():
        issue(b + 1, 1 - slot)

    def wait(p, carry):
        for cache_ref, buf, sem in streams:
            _page_copy(cache_ref, buf, sem, 0, slot, p).wait()
        return carry

    lax.fori_loop(0, n_pages, wait, 0, unroll=_page_group(n_pages))
    return slot


def _compress_paged_kernel(pt_ref, kcache_ref, vcache_ref, ktail_ref, vtail_ref, pm_ref,
                           kw1_ref, kw2_ref, kpe_ref, vw1_ref, vw2_ref, vpe_ref, ko_ref, vo_ref,
                           kbuf, vbuf, kx_buf, vx_buf, ksec_buf, vsec_buf, ksem, vsem, *, n_pages):
    slot = _gather_pages(pt_ref, [(kcache_ref, kbuf, ksem), (vcache_ref, vbuf, vsem)], n_pages)
    rows = _page_group(n_pages) * PAGE
    for buf, tail_ref, w1_ref, w2_ref, pe_ref, x_buf, sec_buf, o_ref in (
            (kbuf, ktail_ref, kw1_ref, kw2_ref, kpe_ref, kx_buf, ksec_buf, ko_ref),
            (vbuf, vtail_ref, vw1_ref, vw2_ref, vpe_ref, vx_buf, vsec_buf, vo_ref)):
        get_pages = lambda i, buf=buf: buf[slot, i * rows:(i + 1) * rows, :].astype(BF16)
        o_ref[0] = _compress_pages(get_pages, n_pages, tail_ref[0], pm_ref, w1_ref, w2_ref, pe_ref,
                                   x_buf, sec_buf)


def _compress_paged(page_table, kcache, vcache, ktail, vtail, kweights, vweights):
    dbz, n_pages = page_table.shape
    n = n_pages * (PAGE // CMP_STRIDE)
    pm = kweights[0]
    consts = (pm,) + tuple(kweights[1:]) + tuple(vweights[1:])
    resident = lambda a: pl.BlockSpec(a.shape, lambda b, pt: (0, 0), pipeline_mode=pl.Buffered(1))
    tail_spec = pl.BlockSpec((1,) + ktail.shape[1:], lambda b, pt: (b, 0, 0, 0))
    out_spec = pl.BlockSpec((1, n, KV_DIM), lambda b, pt: (b, 0, 0))
    grid_spec = pltpu.PrefetchScalarGridSpec(
        num_scalar_prefetch=1,
        grid=(dbz,),
        in_specs=[pl.BlockSpec(memory_space=pl.ANY), pl.BlockSpec(memory_space=pl.ANY), tail_spec, tail_spec]
                 + [resident(a) for a in consts],
        out_specs=(out_spec, out_spec),
        scratch_shapes=[pltpu.VMEM((2, n_pages * PAGE, PAGE), F32)] * 2
                       + _compress_scratch(n)[:1] * 2 + _compress_scratch(n)[1:] * 2
                       + [pltpu.SemaphoreType.DMA((2,))] * 2,
    )
    out = jax.ShapeDtypeStruct((dbz, n, KV_DIM), F32)
    return pl.pallas_call(
        functools.partial(_compress_paged_kernel, n_pages=n_pages),
        grid_spec=grid_spec,
        out_shape=(out, out),
        compiler_params=_params(("arbitrary",)),
        name="compress_paged",
    )(page_table, kcache, vcache, ktail, vtail, *consts)


RANK_UNROLL = 4


def _select_blocks(imp_t, tpos, n_blocks, score_buf, n_live=None):
    j = _iota(imp_t.shape, 0)
    valid = (j * SEL_BLOCK <= tpos) & (j < n_blocks)
    cur = tpos // SEL_BLOCK
    forced = (j == 0) | (j == cur) | (j == cur - 1)
    score = jnp.where(valid, jnp.where(forced, FORCED_SCORE, imp_t), INVALID_SCORE)
    score_buf[...] = score

    def body(i, cnt):
        row = score_buf[pl.ds(i, 1), :]
        above = jnp.where(row > score, 1.0, 0.0)
        tie = jnp.where(row == score, jnp.where(j > i, 1.0, 0.0), 0.0)
        return cnt + above + tie

    zero = jnp.zeros(imp_t.shape, F32)
    if n_live is None:
        cnt = lax.fori_loop(0, n_blocks, body, zero, unroll=RANK_UNROLL)
    else:
        def group(gi, cnt):
            for u in range(RANK_UNROLL):
                cnt = body(gi * RANK_UNROLL + u, cnt)
            return cnt
        assert imp_t.shape[0] % RANK_UNROLL == 0
        cnt = lax.fori_loop(0, (n_live + RANK_UNROLL - 1) // RANK_UNROLL, group, zero)
    return jnp.where(valid & (cnt < float(min(SEL_TOP_N, n_blocks))), 1.0, 0.0)


def _nsa_prompt_kernel(q_ref, kc_ref, vc_ref, ks_ref, vs_ref, kw_ref, vw_ref, ovl_ref, e_ref, wbias_ref, misc_ref,
                       o_ref, score_buf, *, tq, tk, tw, n_blocks):
    g = pl.program_id(1)
    t0 = pl.program_id(2) * tq
    rows = REP * tq
    q = q_ref[0].reshape(rows, HD)

    kc = kc_ref[0, 0]
    ncp = kc.shape[0]
    qpos_c = t0 + _iota((tq, ncp), 0)
    maskc = ((_iota((tq, ncp), 1) * CMP_STRIDE + (CMP_BLOCK - 1)) <= qpos_c)[None]
    s = _dot_nt(q, kc).reshape(REP, tq, ncp)
    sm = jnp.where(maskc, s, NEG)
    e = jnp.exp(sm - jnp.max(sm, axis=-1, keepdims=True))
    p = e / jnp.sum(e, axis=-1, keepdims=True) * jnp.where(maskc, 1.0, 0.0)
    o_c = _dot(p.reshape(rows, ncp).astype(BF16), vc_ref[0, 0])
    psum = jnp.sum(p, axis=0)

    imp_t = _dot_nt(ovl_ref[...], psum, HI)
    tpos = t0 + _iota(imp_t.shape, 1)
    n_live = jnp.minimum((t0 + tq - 1) // SEL_BLOCK + 1, n_blocks)
    selt = _select_blocks(imp_t, tpos, n_blocks, score_buf, n_live).astype(BF16)

    nsp = selt.shape[0]
    eye = jnp.where(_iota((nsp, nsp), 0) == _iota((nsp, nsp), 1), 1.0, 0.0).astype(BF16)
    bias = ((_dot_tn(selt, eye) - 1.0) * (-NEG)).astype(BF16)
    q_aug = jnp.concatenate([q, jnp.concatenate([bias] * REP, axis=0)], axis=1)
    ones_k = jnp.ones((HD, tk), BF16)
    qpos_k = t0 + _iota((tq, tk), 0)
    lane_k = _iota((tq, tk), 1)
    n_full = t0 // tk

    def step(kt, carry, diagonal):
        m, acc = carry
        k0 = pl.multiple_of(kt * tk, tk)
        k_aug = jnp.concatenate([ks_ref[0, 0, :, pl.ds(k0, tk)], e_ref[kt]], axis=0)
        v_aug = jnp.concatenate([vs_ref[0, 0, :, pl.ds(k0, tk)], ones_k], axis=0)
        s = _dot(q_aug, k_aug).reshape(REP, tq, tk)
        if diagonal:
            s = jnp.where((k0 + lane_k <= qpos_k)[None], s, NEG)
        m_new = jnp.maximum(m, jnp.max(s, axis=-1, keepdims=True))
        e = jnp.exp(s - m_new)
        pv = _dot_nt(e.reshape(rows, tk).astype(BF16), v_aug).reshape(REP, tq, 2 * HD)
        return m_new, jnp.exp(m - m_new) * acc + pv

    init = (jnp.full((REP, tq, 1), NEG, F32), jnp.zeros((REP, tq, 2 * HD), F32))
    carry = lax.fori_loop(0, n_full, functools.partial(step, diagonal=False), init)
    _, acc_s = step(n_full, carry, True)
    o_s = acc_s[..., :HD] / acc_s[..., HD:]

    wk = WINDOW + tw
    q3 = q.reshape(REP, tq, HD)
    ones_w = jnp.ones((HD, wk), BF16)
    o_w = []
    for w in range(tq // tw):
        t0w = t0 + w * tw
        start = pl.multiple_of(jnp.maximum(t0w - WINDOW, 0), tw)
        kw = kw_ref[0, 0, :, pl.ds(start, wk)]
        vw_aug = jnp.concatenate([vw_ref[0, 0, :, pl.ds(start, wk)], ones_w], axis=0)
        case = jnp.minimum(t0w // tw, WINDOW // tw)
        qw = q3[:, w * tw:(w + 1) * tw].reshape(REP * tw, HD)
        sm = _dot(qw, kw).reshape(REP, tw, wk) + wbias_ref[case][None]
        e = jnp.exp(sm - jnp.max(sm, axis=-1, keepdims=True))
        acc_w = _dot_nt(e.reshape(REP * tw, wk).astype(BF16), vw_aug).reshape(REP, tw, 2 * HD)
        o_w.append(acc_w[..., :HD] / acc_w[..., HD:])
    o_w = o_w[0] if len(o_w) == 1 else jnp.concatenate(o_w, axis=1)

    gates = jax.nn.sigmoid(misc_ref[0])
    gate_lane = _iota(gates.shape, 1) - (N_SSM_HEADS + 3 * REP * g)
    o_c = o_c.reshape(REP, tq, HD)
    for r in range(REP):
        gh = [jnp.sum(jnp.where(gate_lane == 3 * r + br, gates, 0.0), axis=-1, keepdims=True)
              for br in range(3)]
        o = gh[0] * o_c[r] + gh[1] * o_s[r] + gh[2] * o_w[r]
        o_ref[0, :, r * HD:(r + 1) * HD] = o.astype(o_ref.dtype)


NSA_WINDOW_TILE = 128


def _window_bias(tw):
    c = np.arange(WINDOW // tw + 1)[:, None, None]
    d = np.arange(WINDOW + tw)[None, None, :] - c * tw - np.arange(tw)[None, :, None]
    return jnp.asarray(np.where((d <= 0) & (d > -WINDOW), 0.0, NEG).astype(np.float32))


def _nsa_prompt(qh, kch, vch, kvtb, ovl_t, e3, misc, n_blocks, tq, tk):
    bsz, nh, t, _ = qh.shape
    ncp = kch.shape[2]
    nsp = ovl_t.shape[0]
    kv_spec = lambda proj: pl.BlockSpec((1, 1, HD, t), lambda b, g, i: (b, N_KV * proj + g, 0, 0))
    cmp_spec = pl.BlockSpec((1, 1, ncp, HD), lambda b, g, i: (b, g, 0, 0))
    tw = min(tq, NSA_WINDOW_TILE)
    wbias = _window_bias(tw)
    return pl.pallas_call(
        functools.partial(_nsa_prompt_kernel, tq=tq, tk=tk, tw=tw, n_blocks=n_blocks),
        grid=(bsz, N_KV, t // tq),
        in_specs=[
            pl.BlockSpec((1, REP, tq, HD), lambda b, g, i: (b, g, i, 0)),
            cmp_spec, cmp_spec,
            kv_spec(2), kv_spec(3), kv_spec(4), kv_spec(5),
            pl.BlockSpec(ovl_t.shape, lambda b, g, i: (0, 0)),
            pl.BlockSpec(e3.shape, lambda b, g, i: (0, 0, 0)),
            pl.BlockSpec(wbias.shape, lambda b, g, i: (0, 0, 0)),
            pl.BlockSpec((1, tq, LANES), lambda b, g, i: (b, i, 0)),
        ],
        out_specs=pl.BlockSpec((1, tq, REP * HD), lambda b, g, i: (b, i, g)),
        out_shape=jax.ShapeDtypeStruct((bsz, t, nh * HD), BF16),
        scratch_shapes=[pltpu.VMEM((nsp, tq), F32)],
        compiler_params=_params(("parallel", "parallel", "arbitrary")),
        name="nsa_prompt",
    )(qh, kch, vch, kvtb, kvtb, kvtb, kvtb, ovl_t, e3, wbias, misc)


def _out_proj_kernel(y_ref, o_ref, h_ref, w_ref, g_ref, b_ref, out_ref, *, alpha):
    mixed = _dot(y_ref[...].astype(BF16), w_ref[0:D_SSM, :]) + _dot(o_ref[...], w_ref[D_SSM:, :])
    out_ref[...] = _layer_norm(alpha * h_ref[...] + mixed, g_ref[...], b_ref[...])


def _out_proj_ln(y, o, h, w, g, b, alpha):
    m, d = h.shape
    tm = min(512, m)
    row = lambda i: (i, 0)
    fixed = lambda i: (0, 0)
    return pl.pallas_call(
        functools.partial(_out_proj_kernel, alpha=alpha),
        grid=(m // tm,),
        in_specs=[
            pl.BlockSpec((tm, D_SSM), row),
            pl.BlockSpec((tm, D_NSA), row),
            pl.BlockSpec((tm, d), row),
            pl.BlockSpec(w.shape, fixed),
            pl.BlockSpec((1, d), fixed),
            pl.BlockSpec((1, d), fixed),
        ],
        out_specs=pl.BlockSpec((tm, d), row),
        out_shape=jax.ShapeDtypeStruct((m, d), F32),
        compiler_params=_params(("parallel",)),
        name="out_proj_ln",
    )(y, o, h, w, g, b)


def _row_group(shape):
    return _iota(shape, 0) // (shape[0] // N_KV)


def _cmp_attn_sample_kernel(q_ref, kc_ref, vc_ref, ssum_ref, oc_ref, psum_ref, *, past, s_len):
    nb, rows, _ = q_ref.shape
    nc = kc_ref.shape[1]
    t_row = (_iota((rows, nc), 0) // REP) % s_len
    cidx = _iota((rows, nc), 1)
    maskc = (cidx * CMP_STRIDE + (CMP_BLOCK - 1)) <= past + t_row
    maskf = jnp.where(maskc, 1.0, 0.0)
    own = (_iota((rows, KV_DIM), 1) // HD) == _row_group((rows, KV_DIM))
    for i in range(nb):
        s = _dot_nt(q_ref[i], kc_ref[i].astype(BF16))
        sm = jnp.where(maskc, s, NEG)
        e = jnp.exp(sm - jnp.max(sm, axis=-1, keepdims=True))
        p = e / jnp.sum(e, axis=-1, keepdims=True) * maskf
        o = _dot(p.astype(BF16), vc_ref[i].astype(BF16))
        oc_ref[i] = jnp.where(own, o, 0.0)
        psum_ref[i] = _dot(ssum_ref[...], p, HI)


def _cmp_attn_sample(qbd, kc, vc, ssum, past, s_len):
    dbz, rows, _ = qbd.shape
    nc = kc.shape[1]
    ng = ssum.shape[0]
    nb = math.gcd(dbz, 4)
    per_b = lambda b: (b, 0, 0)
    return pl.pallas_call(
        functools.partial(_cmp_attn_sample_kernel, past=past, s_len=s_len),
        grid=(dbz // nb,),
        in_specs=[
            pl.BlockSpec((nb, rows, KV_DIM), per_b),
            pl.BlockSpec((nb, nc, KV_DIM), per_b),
            pl.BlockSpec((nb, nc, KV_DIM), per_b),
            pl.BlockSpec(ssum.shape, lambda b: (0, 0)),
        ],
        out_specs=(pl.BlockSpec((nb, rows, KV_DIM), per_b), pl.BlockSpec((nb, ng, nc), per_b)),
        out_shape=(jax.ShapeDtypeStruct((dbz, rows, KV_DIM), F32),
                   jax.ShapeDtypeStruct((dbz, ng, nc), F32)),
        compiler_params=_params(("parallel",)),
        name="cmp_attn_sample",
    )(qbd, kc, vc, ssum)


def _select_sample_kernel(psum_ref, ovl_ref, selt_ref, score_buf, *, past, s_len, n_blocks):
    imp_t = _dot_nt(ovl_ref[...], psum_ref[...], HI)
    tpos = past + _iota(imp_t.shape, 1) % s_len
    selt_ref[...] = _select_blocks(imp_t, tpos, n_blocks, score_buf)


def _select_sample(psum_all, ovl_t, past, s_len, n_blocks):
    nsp = ovl_t.shape[0]
    cols = psum_all.shape[0]
    return pl.pallas_call(
        functools.partial(_select_sample_kernel, past=past, s_len=s_len, n_blocks=n_blocks),
        out_shape=jax.ShapeDtypeStruct((nsp, cols), F32),
        scratch_shapes=[pltpu.VMEM((nsp, cols), F32)],
        compiler_params=pltpu.CompilerParams(vmem_limit_bytes=VMEM_LIMIT),
        name="select_sample",
    )(psum_all, ovl_t)


def _online_update(state, sm, v_t):
    m, l, acc = state
    m_new = jnp.maximum(m, jnp.max(sm, axis=-1, keepdims=True))
    alpha = jnp.exp(m - m_new)
    e = jnp.exp(sm - m_new)
    l = alpha * l + jnp.sum(e, axis=-1, keepdims=True)
    return m_new, l, alpha * acc + _dot_nt(e.astype(BF16), v_t)


def _sel_win_sample_kernel(pt_ref, kcache_ref, vcache_ref, q_ref, sel_ref, e_ref, kt_ref, vt_ref,
                           kw_ref, vw_ref, kwn_ref, vwn_ref, oc_ref, gate_ref, o_ref, kbuf, vbuf, ksem, vsem,
                           *, n_pages, s_len, n_chunks, chunk):
    slot = _gather_pages(pt_ref, [(kcache_ref, kbuf, ksem), (vcache_ref, vbuf, vsem)], n_pages)

    q = q_ref[0]
    rows = q.shape[0]
    sel = sel_ref[0]
    sel_main = sel[:, :LANES]
    past = n_pages * PAGE
    t_col = (_iota((rows, 1), 0) // REP) % s_len
    pages_pc = chunk // PAGE

    def chunk_t(buf, c):
        return jnp.concatenate([buf[slot, (c * pages_pc + j) * PAGE:(c * pages_pc + j + 1) * PAGE, :]
                                for j in range(pages_pc)], axis=1).astype(BF16)

    state = (jnp.full((rows, 1), NEG, F32), jnp.zeros((rows, 1), F32), jnp.zeros((rows, KV_DIM), F32))
    for c in range(n_chunks):
        blk = _dot(sel_main, e_ref[c])
        kpos = c * chunk + _iota((rows, chunk), 1)
        ok = jnp.where(kpos <= past + t_col, blk, 0.0) > 0.5
        state = _online_update(state, jnp.where(ok, _dot(q, chunk_t(kbuf, c)), NEG), chunk_t(vbuf, c))
    lane_t = _iota((rows, kt_ref.shape[2]), 1)
    sel_new = sel[:, LANES:LANES + 1].astype(F32)
    ok = jnp.where(lane_t <= t_col, jnp.where(lane_t < s_len, sel_new, 0.0), 0.0) > 0.5
    _, l, acc = _online_update(state, jnp.where(ok, _dot(q, kt_ref[0].astype(BF16)), NEG),
                               vt_ref[0].astype(BF16))
    o_s = acc / l

    wb = kw_ref.shape[2]
    iw = _iota((rows, wb), 1)
    ok_c = iw > t_col + (wb - WINDOW)
    state = (jnp.full((rows, 1), NEG, F32), jnp.zeros((rows, 1), F32), jnp.zeros((rows, KV_DIM), F32))
    state = _online_update(state, jnp.where(ok_c, _dot(q, kw_ref[0].astype(BF16)), NEG), vw_ref[0].astype(BF16))
    ok_n = jnp.where(lane_t <= t_col, jnp.where(lane_t < s_len, 1.0, 0.0), 0.0) > 0.5
    _, l, acc = _online_update(state, jnp.where(ok_n, _dot(q, kwn_ref[0].astype(BF16)), NEG),
                               vwn_ref[0].astype(BF16))
    o_w = acc / l

    gates = jax.nn.sigmoid(gate_ref[0])
    o = gates[:, 0:1] * oc_ref[0] + gates[:, 1:2] * o_s + gates[:, 2:3] * o_w
    own = (_iota(o.shape, 1) // HD) == _row_group(o.shape)
    o = jnp.where(own, o, 0.0)
    o_ref[0] = o[:, :HD] + o[:, HD:]


def _sel_win_sample(page_table, kcache, vcache, qbd, selx, e4, ktail, vtail, kwin, vwin, kwnew, vwnew,
                    oc, graw, s_len):
    dbz, n_pages = page_table.shape
    rows = qbd.shape[1]
    n_chunks, _, chunk = e4.shape
    per_b = lambda b, pt: (b, 0, 0)
    blk = lambda a: pl.BlockSpec((1,) + a.shape[1:], per_b)
    grid_spec = pltpu.PrefetchScalarGridSpec(
        num_scalar_prefetch=1,
        grid=(dbz,),
        in_specs=[
            pl.BlockSpec(memory_space=pl.ANY),
            pl.BlockSpec(memory_space=pl.ANY),
            blk(qbd), blk(selx),
            pl.BlockSpec(e4.shape, lambda b, pt: (0, 0, 0)),
            blk(ktail), blk(vtail), blk(kwin), blk(vwin), blk(kwnew), blk(vwnew), blk(oc), blk(graw),
        ],
        out_specs=pl.BlockSpec((1, rows, HD), per_b),
        scratch_shapes=[
            pltpu.VMEM((2, n_pages * PAGE, PAGE), F32),
            pltpu.VMEM((2, n_pages * PAGE, PAGE), F32),
            pltpu.SemaphoreType.DMA((2,)),
            pltpu.SemaphoreType.DMA((2,)),
        ],
    )
    return pl.pallas_call(
        functools.partial(_sel_win_sample_kernel, n_pages=n_pages, s_len=s_len,
                          n_chunks=n_chunks, chunk=chunk),
        grid_spec=grid_spec,
        out_shape=jax.ShapeDtypeStruct((dbz, rows, HD), F32),
        compiler_params=_params(("arbitrary",)),
        name="sel_win_sample",
    )(page_table, kcache, vcache, qbd, selx, e4, ktail, vtail, kwin, vwin, kwnew, vwnew, oc, graw)


def _rope_tables(pos):
    half = HD // 2
    inv = ROPE_THETA ** (-np.arange(half, dtype=np.float64) / half)
    ang = pos.astype(np.float64)[:, None] * inv[None, :]
    cos = np.cos(ang)
    sin = np.sin(ang)
    reps = LANES // HD
    cos_l = np.tile(np.concatenate([cos, cos], axis=-1), (1, reps))
    sin_l = np.tile(np.concatenate([-sin, sin], axis=-1), (1, reps))
    return tuple(jnp.asarray(np.ascontiguousarray(a), F32) for a in (cos_l, sin_l, cos.T, sin.T))


def _overlap_t(nc, ncp, ns, nsp):
    c_start = np.arange(ncp) * CMP_STRIDE
    s_start = np.arange(nsp) * SEL_BLOCK
    ovl = ((c_start[None, :] + CMP_BLOCK > s_start[:, None]) & (c_start[None, :] < s_start[:, None] + SEL_BLOCK))
    ovl = ovl & (np.arange(ncp)[None, :] < nc) & (np.arange(nsp)[:, None] < ns)
    return jnp.asarray(ovl.astype(np.float32))


def _expander(n_rows, n_chunks, chunk, dtype=BF16):
    key_block = (np.arange(n_chunks)[:, None] * chunk + np.arange(chunk)[None, :]) // SEL_BLOCK
    e = key_block[:, None, :] == np.arange(n_rows)[None, :, None]
    return jnp.asarray(e.astype(np.float32)).astype(dtype)


def _page_permutation():
    rows_pp = PAGE // CMP_STRIDE
    r = np.arange(PAGE)
    src = (r % rows_pp) * CMP_STRIDE + r // rows_pp
    return jnp.asarray((src[:, None] == np.arange(PAGE)[None, :]).astype(np.float32)).astype(BF16)


def _round_up(x, m):
    return -(-x // m) * m


def _compress_weights(w1, w2, pe):
    eye = jnp.eye(N_KV, dtype=F32)
    halves = [w1[half * CMP_STRIDE:(half + 1) * CMP_STRIDE].reshape(CMP_IN, CMP_HIDDEN) for half in range(2)]
    w1d = jnp.concatenate(halves, axis=1).astype(BF16)
    w2big = jnp.einsum("hd,gk->ghkd", w2, eye).reshape(N_KV * CMP_HIDDEN, KV_DIM).astype(BF16)
    pe_rows = [jnp.broadcast_to(pe[half * CMP_STRIDE:(half + 1) * CMP_STRIDE].reshape(1, CMP_IN), (SUBLANES, CMP_IN))
               for half in range(2)]
    pe2 = jnp.concatenate(pe_rows, axis=0).astype(BF16)
    return _page_permutation(), w1d, w2big, pe2


def _layer_weights(w, l):
    sizes = [D_SSM, CONV_DIM, N_SSM_HEADS, D_NSA, N_KV_PROJ * KV_DIM, N_HEADS * 3]
    offs = np.cumsum([0] + sizes)
    w_in = w["w_in"][l]
    seg = lambda i: w_in[:, offs[i]:offs[i + 1]]
    pad = LANES - sizes[2] - sizes[5]
    w_in_r = jnp.concatenate([seg(0), seg(1), seg(3), seg(2), seg(5),
                              jnp.zeros((w_in.shape[0], pad), F32)], axis=1).astype(BF16)
    bias_misc = jnp.concatenate([jnp.zeros((sizes[2],), F32), w["b_gate"][l], jnp.zeros((pad,), F32)])[None, :]
    lane_pad = lambda v: jnp.concatenate([v, jnp.zeros((LANES - v.shape[0],), F32)])[None, :]
    lw = {
        "w_in": w_in_r,
        "w_kv_t": seg(4).T.astype(BF16),
        "bias_misc": bias_misc,
        "conv_w": w["conv_w"][l],
        "conv_b_row": w["conv_b"][l][None, :],
        "dtb_row": lane_pad(w["dt_bias"][l]),
        "alog_row": lane_pad(w["a_log"][l]),
        "dtb_col": w["dt_bias"][l][:, None],
        "alog_col": w["a_log"][l][:, None],
        "dskip_row": jnp.repeat(w["d_skip"][l], HD)[None, :],
        "normw_row": w["ssm_norm_w"][l][None, :],
        "w_out": w["w_out"][l].astype(BF16),
    }
    lw["cmp_k"] = _compress_weights(w["cmp_k_w1"][l], w["cmp_k_w2"][l], w["cmp_k_pe"][l])
    lw["cmp_v"] = _compress_weights(w["cmp_v_w1"][l], w["cmp_v_w2"][l], w["cmp_v_pe"][l])
    for i in (1, 2, 3):
        lw[f"ln{i}"] = (w[f"ln{i}_g"][l][None, :], w[f"ln{i}_b"][l][None, :])
    for i in (1, 2):
        lw[f"ffn{i}"] = (w[f"ffn{i}_w_gate"][l].astype(BF16), w[f"ffn{i}_w_up"][l].astype(BF16),
                         w[f"ffn{i}_w_down"][l].astype(BF16))
    return lw


def _heads_major(x, bsz, t, n):
    return x.reshape(bsz, t, n, HD).transpose(0, 2, 1, 3)


def _mix_prompt(h, lw, bsz, t):
    m = bsz * t
    z, xbc, misc, qh, kvtb, *kvt = _in_proj(h, lw["w_in"], lw["w_kv_t"], lw["bias_misc"],
                                            _rope_tables(np.arange(t)), bsz, t)
    kv_rows = [a.reshape(bsz, N_KV, HD, t).transpose(0, 3, 1, 2) for a in kvt]

    chunk = min(128, t)
    misc3 = misc.reshape(bsz, t, LANES)
    xbc3 = xbc.reshape(bsz, t, CONV_DIM)
    y_ssd, h_ssm = _ssd(z.reshape(bsz, t, D_SSM), xbc3, misc3, misc3.transpose(0, 2, 1), lw, chunk)
    conv_state = xbc3[:, t - (CONV_TAPS - 1):]

    n_str = t // CMP_STRIDE
    nc = n_str - 1
    ns = -(-t // SEL_BLOCK)
    nsp = _round_up(ns, SUBLANES)
    kvtb4 = kvtb.reshape(bsz, N_KV_PROJ, KV_DIM, t)
    kc = _compress(kvtb4, 0, *lw["cmp_k"])
    vc = _compress(kvtb4, 1, *lw["cmp_v"])
    kch = _heads_major(kc, bsz, n_str, N_KV).astype(BF16)
    vch = _heads_major(vc, bsz, n_str, N_KV).astype(BF16)
    tq = min(512, t)
    tk = min(512, t)
    e3 = _expander(nsp, t // tk, tk)
    o = _nsa_prompt(qh, kch, vch, kvtb.reshape(bsz, N_KV_PROJ * N_KV, HD, t), _overlap_t(nc, n_str, ns, nsp),
                    e3, misc3, ns, tq, tk)

    wb = min(WINDOW, t)
    state = tuple(kv_rows[:4]) + (kv_rows[4][:, t - wb:], kv_rows[5][:, t - wb:], h_ssm, conv_state)
    return y_ssd.reshape(m, D_SSM), o.reshape(m, D_NSA), state


def _mix_sample(h, lw, l, dbz, s_len, caches, state_ssm, state_conv, page_table):
    cache_k_cmp, cache_v_cmp, cache_k_slc, cache_v_slc, cache_k_win, cache_v_win = caches
    m = dbz * s_len
    n_pages = page_table.shape[1]
    past = n_pages * PAGE
    pos = past + np.arange(s_len)
    z, xbc, misc, qh, _, *kvt = _in_proj(h, lw["w_in"], lw["w_kv_t"], lw["bias_misc"],
                                         _rope_tables(np.tile(pos, dbz)), 1, m)
    kvs = [a.reshape(KV_DIM, dbz, s_len) for a in kvt]
    new_rows = [kvs[i].transpose(1, 2, 0) for i in range(N_KV_PROJ)]
    new_t = [kvs[i].transpose(1, 0, 2) for i in range(N_KV_PROJ)]
    k_c, v_c, k_s, v_s, k_w, v_w = [r.reshape(dbz, s_len, N_KV, HD) for r in new_rows]

    misc3 = misc.reshape(dbz, s_len, LANES)
    xbc3 = xbc.reshape(dbz, s_len, CONV_DIM)
    xh = jnp.concatenate([state_conv[l], xbc3], axis=1)
    y_t, h_new = _ssd_sample(xh.transpose(1, 2, 0), z.reshape(dbz, s_len, D_SSM).transpose(1, 2, 0),
                             misc3[:, :, :N_SSM_HEADS].transpose(1, 2, 0), state_ssm[l].transpose(1, 2, 3, 0),
                             lw, dbz, s_len)
    y_ssd = y_t.transpose(2, 0, 1)
    h_ssm = h_new.transpose(3, 0, 1, 2)
    conv_state = xh[:, -(CONV_TAPS - 1):]

    n_pool = cache_k_cmp.shape[1]
    pages_t = lambda cache: cache[l].transpose(0, 2, 3, 1).reshape(n_pool, KV_DIM, PAGE)

    def tail_rows(new):
        flat = new.reshape(dbz, s_len, N_KV, HD).transpose(0, 2, 1, 3).reshape(dbz, N_KV, 1, s_len * HD)
        return jnp.pad(flat, ((0, 0), (0, 0), (0, SUBLANES - 1), (0, CMP_IN - s_len * HD)))

    kc, vc = _compress_paged(page_table, pages_t(cache_k_cmp), pages_t(cache_v_cmp),
                             tail_rows(new_rows[0]), tail_rows(new_rows[1]), lw["cmp_k"], lw["cmp_v"])
    nc = kc.shape[1]
    total = past + s_len
    ns = -(-total // SEL_BLOCK)
    nsp = _round_up(ns, SUBLANES)

    rows = N_KV * s_len * REP
    qg = qh.reshape(N_KV, REP, dbz, s_len, HD).transpose(2, 0, 3, 1, 4)
    qbd = jnp.einsum("bgtrd,gk->bgtrkd", qg, jnp.eye(N_KV, dtype=BF16)).reshape(dbz, rows, KV_DIM)
    ng = N_KV * s_len
    ssum = jnp.asarray((np.arange(ng)[:, None] == np.arange(rows)[None, :] // REP).astype(np.float32))
    oc, psum = _cmp_attn_sample(qbd, kc, vc, ssum, past, s_len)
    selt = _select_sample(psum.reshape(dbz * ng, nc), _overlap_t(nc, nc, ns, nsp), past, s_len, ns)
    sel = selt.T.reshape(dbz, ng, 1, nsp)
    sel = jnp.broadcast_to(sel, (dbz, ng, REP, nsp)).reshape(dbz, rows, nsp)
    n_cached = past // SEL_BLOCK
    selx = jnp.concatenate([sel[:, :, :n_cached], jnp.zeros((dbz, rows, LANES - n_cached), F32),
                            sel[:, :, n_cached:n_cached + 1], jnp.zeros((dbz, rows, LANES - 1), F32)],
                           axis=-1).astype(BF16)
    chunk_k = min(2048, past)
    e4 = _expander(LANES, past // chunk_k, chunk_k)
    pad_lanes = lambda a: jnp.pad(a, ((0, 0), (0, 0), (0, LANES - s_len))).astype(BF16)
    wb = cache_k_win.shape[2]
    win_t = lambda cache: cache[l].transpose(0, 2, 3, 1).reshape(dbz, KV_DIM, wb)
    gate = misc3[:, :, N_SSM_HEADS:N_SSM_HEADS + 3 * N_HEADS].reshape(dbz, s_len, N_KV, REP, 3)
    graw = jnp.pad(gate.transpose(0, 2, 1, 3, 4).reshape(dbz, rows, 3), ((0, 0), (0, 0), (0, LANES - 3)))
    o = _sel_win_sample(page_table, pages_t(cache_k_slc), pages_t(cache_v_slc), qbd, selx, e4,
                        pad_lanes(new_t[2]), pad_lanes(new_t[3]), win_t(cache_k_win), win_t(cache_v_win),
                        pad_lanes(new_t[4]), pad_lanes(new_t[5]), oc, graw, s_len)
    o_nsa = o.reshape(dbz, N_KV, s_len, REP, HD).transpose(0, 2, 1, 3, 4).reshape(m, D_NSA).astype(BF16)

    kw_full = jnp.concatenate([cache_k_win[l], k_w], axis=1)
    vw_full = jnp.concatenate([cache_v_win[l], v_w], axis=1)
    state = (k_c, v_c, k_s, v_s, kw_full[:, -wb:], vw_full[:, -wb:], h_ssm, conv_state)
    return y_ssd.reshape(m, D_SSM), o_nsa, state


def _layer(x, lw, alpha, mix_fn):
    h1 = _ffn_ln(x, *lw["ffn1"], *lw["ln1"], alpha)
    y_ssd, o_nsa, state = mix_fn(h1)
    h2 = _out_proj_ln(y_ssd, o_nsa, h1, lw["w_out"], *lw["ln2"], alpha)
    return _ffn_ln(h2, *lw["ffn2"], *lw["ln3"], alpha), state


def kernel(x_prompt, x_sample, cache_k_cmp, cache_v_cmp, cache_k_slc, cache_v_slc, cache_k_win, cache_v_win, state_ssm, state_conv, page_table, w_in, b_gate, conv_w, conv_b, dt_bias, a_log, d_skip, ssm_norm_w, cmp_k_w1, cmp_k_w2, cmp_k_pe, cmp_v_w1, cmp_v_w2, cmp_v_pe, w_out, ln1_g, ln1_b, ln2_g, ln2_b, ln3_g, ln3_b, ffn1_w_gate, ffn1_w_up, ffn1_w_down, ffn2_w_gate, ffn2_w_up, ffn2_w_down):
    weights = dict(w_in=w_in, b_gate=b_gate, conv_w=conv_w, conv_b=conv_b, dt_bias=dt_bias, a_log=a_log,
                   d_skip=d_skip, ssm_norm_w=ssm_norm_w, cmp_k_w1=cmp_k_w1, cmp_k_w2=cmp_k_w2,
                   cmp_k_pe=cmp_k_pe, cmp_v_w1=cmp_v_w1, cmp_v_w2=cmp_v_w2, cmp_v_pe=cmp_v_pe, w_out=w_out,
                   ln1_g=ln1_g, ln1_b=ln1_b, ln2_g=ln2_g, ln2_b=ln2_b, ln3_g=ln3_g, ln3_b=ln3_b,
                   ffn1_w_gate=ffn1_w_gate, ffn1_w_up=ffn1_w_up, ffn1_w_down=ffn1_w_down,
                   ffn2_w_gate=ffn2_w_gate, ffn2_w_up=ffn2_w_up, ffn2_w_down=ffn2_w_down)
    depth = w_in.shape[0]
    bsz, t, d = x_prompt.shape
    dbz, s_len, _ = x_sample.shape
    alpha = (2.0 * depth) ** 0.25
    caches = (cache_k_cmp, cache_v_cmp, cache_k_slc, cache_v_slc, cache_k_win, cache_v_win)
    y_p = x_prompt.reshape(bsz * t, d)
    y_s = x_sample.reshape(dbz * s_len, d)
    p_states, s_states = [], []
    for l in range(depth):
        lw = _layer_weights(weights, l)
        y_p, st_p = _layer(y_p, lw, alpha, lambda h: _mix_prompt(h, lw, bsz, t))
        y_s, st_s = _layer(y_s, lw, alpha, lambda h: _mix_sample(h, lw, l, dbz, s_len, caches, state_ssm,
                                                                 state_conv, page_table))
        p_states.append(st_p)
        s_states.append(st_s)
    p_st = [jnp.stack(a) for a in zip(*p_states)]
    s_st = [jnp.stack(a) for a in zip(*s_states)]
    outs = [y_p.reshape(bsz, t, d), y_s.reshape(dbz, s_len, d)]
    for p, s in zip(p_st[:6], s_st[:6]):
        outs += [p, s]
    outs += [p_st[6], s_st[6], p_st[7], s_st[7]]
    return tuple(outs)
```

```python
import functools
import math

import numpy as np
import jax
import jax.numpy as jnp
from jax import lax
from jax.experimental import pallas as pl
from jax.experimental.pallas import tpu as pltpu

F32 = jnp.float32
BF16 = jnp.bfloat16
HI = lax.Precision.HIGHEST

HD = 64
N_SSM_HEADS = 8
N_SSM_GROUPS = 2
D_SSM = 512
CONV_DIM = 768
CONV_TAPS = 4
N_HEADS = 8
N_KV = 2
REP = N_HEADS // N_KV
D_NSA = 512
KV_DIM = N_KV * HD
N_KV_PROJ = 6
CMP_STRIDE = 16
CMP_BLOCK = 32
CMP_HIDDEN = 128
SEL_BLOCK = 64
SEL_TOP_N = 16
WINDOW = 512
PAGE = 128
ROPE_THETA = 10000.0
LN_EPS = 1e-5
RMS_EPS = 1e-5
NEG = -1e30
FORCED_SCORE = 1e30
INVALID_SCORE = -1.0

LANES = 128
SUBLANES = 8
VMEM_LIMIT = 56 * 1024 * 1024

NT_DIMS = (((1,), (1,)), ((), ()))
TN_DIMS = (((0,), (0,)), ((), ()))


def _params(sem):
    return pltpu.CompilerParams(dimension_semantics=sem, vmem_limit_bytes=VMEM_LIMIT)


def _dot(a, b, precision=None):
    return jnp.dot(a, b, preferred_element_type=F32, precision=precision)


def _dot_nt(a, b, precision=None):
    return lax.dot_general(a, b, NT_DIMS, preferred_element_type=F32, precision=precision)


def _dot_tn(a, b, precision=None):
    return lax.dot_general(a, b, TN_DIMS, preferred_element_type=F32, precision=precision)


def _iota(shape, dim):
    return lax.broadcasted_iota(jnp.int32, shape, dim)


def _layer_norm(y, g, b):
    mu = jnp.mean(y, axis=-1, keepdims=True)
    yc = y - mu
    var = jnp.mean(yc * yc, axis=-1, keepdims=True)
    return yc * lax.rsqrt(var + LN_EPS) * g + b


def _ffn_ln_kernel(x_ref, wg_ref, wu_ref, wd_ref, g_ref, b_ref, o_ref, y_ref, *, alpha, n_split, skip_flush_matmuls):
    step = pl.program_id(0)

    @pl.when(step == 0)
    def _():
        y_ref[...] = jnp.zeros_like(y_ref)

    o_ref[...] = _layer_norm(y_ref[...], g_ref[...], b_ref[...])

    def matmuls():
        x = x_ref[...]
        xb = x.astype(BF16)
        tf = wg_ref.shape[1] // n_split
        acc = None
        for s in range(n_split):
            gate = _dot(xb, wg_ref[:, s * tf:(s + 1) * tf])
            up = _dot(xb, wu_ref[:, s * tf:(s + 1) * tf])
            act = (jax.nn.silu(gate) * up).astype(BF16)
            part = _dot(act, wd_ref[s * tf:(s + 1) * tf, :])
            acc = part if acc is None else acc + part
        y_ref[...] = alpha * x + 0.5 * acc

    if skip_flush_matmuls:
        pl.when(step < pl.num_programs(0) - 1)(matmuls)
    else:
        matmuls()


def _ffn_ln(x, wg, wu, wd, g, b, alpha):
    m, d = x.shape
    dff = wg.shape[1]
    tm = min(512, m)
    n_i = m // tm
    n_split = 2 if (dff // 2) % LANES == 0 else 1
    resident = lambda a: pl.BlockSpec(a.shape, lambda i: (0, 0), pipeline_mode=pl.Buffered(1))
    return pl.pallas_call(
        functools.partial(_ffn_ln_kernel, alpha=alpha, n_split=n_split, skip_flush_matmuls=n_i < 8),
        grid=(n_i + 1,),
        in_specs=[
            pl.BlockSpec((tm, d), lambda i: (jnp.minimum(i, n_i - 1), 0)),
            resident(wg), resident(wu), resident(wd),
            pl.BlockSpec((1, d), lambda i: (0, 0)),
            pl.BlockSpec((1, d), lambda i: (0, 0)),
        ],
        out_specs=pl.BlockSpec((tm, d), lambda i: (jnp.maximum(i - 1, 0), 0)),
        out_shape=jax.ShapeDtypeStruct((m, d), F32),
        scratch_shapes=[pltpu.VMEM((tm, d), F32)],
        compiler_params=_params(("arbitrary",)),
        name="ffn_ln",
    )(x, wg, wu, wd, g, b)


OFF_Z, OFF_XBC, OFF_Q, OFF_MISC, W_IN_COLS = 0, 512, 1280, 1792, 1920


def _in_proj_kernel(h_ref, w_ref, wkv_ref, bias_ref, cos_ref, sin_ref, cost_ref, sint_ref,
                    z_ref, xbc_ref, misc_ref, qh_ref, kvtb_ref, *kvt_refs, q_scale):
    hb = h_ref[...].astype(BF16)
    z_ref[...] = _dot(hb, w_ref[:, OFF_Z:OFF_XBC])
    xbc_ref[...] = _dot(hb, w_ref[:, OFF_XBC:OFF_Q])
    qm = _dot(hb, w_ref[:, OFF_Q:W_IN_COLS])
    misc_ref[...] = qm[:, OFF_MISC - OFF_Q:] + bias_ref[...]
    cos = cos_ref[...]
    sin = sin_ref[...]
    first_half = (_iota(cos.shape, 1) & (HD - 1)) < (HD // 2)
    for c in range(D_NSA // LANES):
        x = qm[:, c * LANES:(c + 1) * LANES]
        rot = jnp.where(first_half, pltpu.roll(x, LANES - HD // 2, 1), pltpu.roll(x, HD // 2, 1))
        q = ((x * cos + rot * sin) * q_scale).astype(BF16)
        for j in range(LANES // HD):
            qh_ref[0, c * (LANES // HD) + j] = q[:, j * HD:(j + 1) * HD]
    kvt = _dot_nt(wkv_ref[...], hb)
    cost = cost_ref[...]
    sint = sint_ref[...]
    half = HD // 2
    for i in range(N_KV_PROJ):
        blk = kvt[i * KV_DIM:(i + 1) * KV_DIM, :]
        if i % 2 == 0:
            parts = []
            for g in range(N_KV):
                x1 = blk[g * HD:g * HD + half, :]
                x2 = blk[g * HD + half:(g + 1) * HD, :]
                parts += [x1 * cost - x2 * sint, x2 * cost + x1 * sint]
            blk = jnp.concatenate(parts, axis=0)
        kvt_refs[i][0] = blk
        kvtb_ref[0, i * KV_DIM:(i + 1) * KV_DIM, :] = blk.astype(BF16)


def _in_proj(h, w, wkv, bias, tables, n_seq, t_seq):
    cos, sin, cost, sint = tables
    m, d = h.shape
    tm = min(512, t_seq)
    n_pos = t_seq // tm
    row = lambda i: (i, 0)
    fixed = lambda i: (0, 0)
    out_shapes = (
        jax.ShapeDtypeStruct((m, D_SSM), F32),
        jax.ShapeDtypeStruct((m, CONV_DIM), F32),
        jax.ShapeDtypeStruct((m, LANES), F32),
        jax.ShapeDtypeStruct((n_seq, N_HEADS, t_seq, HD), BF16),
        jax.ShapeDtypeStruct((n_seq, N_KV_PROJ * KV_DIM, t_seq), BF16),
    ) + (jax.ShapeDtypeStruct((n_seq, KV_DIM, t_seq), F32),) * N_KV_PROJ
    seq_t = lambda i: (i // n_pos, 0, i % n_pos)
    return pl.pallas_call(
        functools.partial(_in_proj_kernel, q_scale=HD ** -0.5),
        grid=(m // tm,),
        in_specs=[
            pl.BlockSpec((tm, d), row),
            pl.BlockSpec(w.shape, fixed),
            pl.BlockSpec(wkv.shape, fixed),
            pl.BlockSpec((1, LANES), fixed),
            pl.BlockSpec((tm, LANES), lambda i: (i % n_pos, 0)),
            pl.BlockSpec((tm, LANES), lambda i: (i % n_pos, 0)),
            pl.BlockSpec((HD // 2, tm), lambda i: (0, i % n_pos)),
            pl.BlockSpec((HD // 2, tm), lambda i: (0, i % n_pos)),
        ],
        out_specs=(
            pl.BlockSpec((tm, D_SSM), row),
            pl.BlockSpec((tm, CONV_DIM), row),
            pl.BlockSpec((tm, LANES), row),
            pl.BlockSpec((1, N_HEADS, tm, HD), lambda i: (i // n_pos, 0, i % n_pos, 0)),
            pl.BlockSpec((1, N_KV_PROJ * KV_DIM, tm), seq_t),
        ) + (pl.BlockSpec((1, KV_DIM, tm), seq_t),) * N_KV_PROJ,
        out_shape=out_shapes,
        compiler_params=_params(("parallel",)),
        name="in_proj",
    )(h, w, wkv, bias, cos, sin, cost, sint)


def _ssd_kernel(z_ref, xbc_ref, misc_ref, misct_ref, convw_ref, convb_ref,
                dtb_row_ref, alog_row_ref, dtb_col_ref, alog_col_ref, dskip_ref, normw_ref,
                y_ref, hout_ref, xbuf, hstate, ybuf, *, chunk, nseq):
    c = pl.program_id(1)

    @pl.when(c == 0)
    def _():
        xbuf[:, 0:SUBLANES, :] = jnp.zeros((nseq, SUBLANES, CONV_DIM), F32)
        hstate[...] = jnp.zeros_like(hstate)

    for i in range(nseq):
        _ssd_chunk(z_ref.at[i], xbc_ref.at[i], misc_ref.at[i], misct_ref.at[i], convw_ref, convb_ref,
                   dtb_row_ref, alog_row_ref, dtb_col_ref, alog_col_ref, dskip_ref, normw_ref,
                   y_ref.at[i], xbuf.at[i], hstate.at[i], ybuf.at[i], chunk)

    @pl.when(c == pl.num_programs(1) - 1)
    def _():
        hout_ref[...] = hstate[...]


def _ssd_chunk(z_ref, xbc_ref, misc_ref, misct_ref, convw_ref, convb_ref,
               dtb_row_ref, alog_row_ref, dtb_col_ref, alog_col_ref, dskip_ref, normw_ref,
               y_ref, xbuf, hstate, ybuf, chunk):
    L = chunk
    H = SUBLANES
    x = xbc_ref[...]
    xbuf[H:H + L, :] = x
    conv = convb_ref[...] + convw_ref[CONV_TAPS - 1:CONV_TAPS, :] * x
    for k in range(1, CONV_TAPS):
        conv = conv + convw_ref[CONV_TAPS - 1 - k:CONV_TAPS - k, :] * xbuf[H - k:H - k + L, :]
    xbuf[0:H, :] = x[L - H:L, :]
    xact = jax.nn.silu(conv)
    bm = xact[:, D_SSM:D_SSM + N_SSM_GROUPS * HD]
    cm = xact[:, D_SSM + N_SSM_GROUPS * HD:]

    dt = jax.nn.softplus(misc_ref[...] + dtb_row_ref[...])
    dtt = jax.nn.softplus(misct_ref[0:SUBLANES, :] + dtb_col_ref[...])
    a_row = -jnp.exp(alog_row_ref[...])
    a_col = -jnp.exp(alog_col_ref[...])
    lane_ok = _iota(dt.shape, 1) < N_SSM_HEADS
    da = jnp.where(lane_ok, dt * a_row, 0.0)
    dat = dtt * a_col
    ri = _iota((L, L), 0)
    ci = _iota((L, L), 1)
    causal = ri >= ci
    tri = jnp.where(causal, 1.0, 0.0).astype(F32)
    cum = _dot(tri, da, HI)
    cumt = _dot_nt(dat, tri, HI)
    cum_last = cum[L - 1:L, :]

    rep = N_SSM_HEADS // N_SSM_GROUPS
    for g in range(N_SSM_GROUPS):
        cm_g = cm[:, g * HD:(g + 1) * HD].astype(BF16)
        bm_g = bm[:, g * HD:(g + 1) * HD].astype(BF16)
        cb = _dot_nt(cm_g, bm_g)
        for r in range(rep):
            h = g * rep + r
            ch = cum[:, h:h + 1]
            seg = ch - cumt[h:h + 1, :]
            decay = jnp.where(causal, jnp.exp(jnp.where(causal, seg, 0.0)), 0.0)
            xs_h = xact[:, h * HD:(h + 1) * HD]
            xdt = xs_h * dt[:, h:h + 1]
            hprev = hstate[h]
            y_h = (_dot((cb * decay).astype(BF16), xdt.astype(BF16))
                   + _dot_nt(cm_g, hprev.astype(BF16)) * jnp.exp(ch))
            cl = cum_last[:, h:h + 1]
            tail = jnp.exp(cl - ch)
            hstate[h] = hprev * jnp.exp(cl) + _dot_tn((xdt * tail).astype(BF16), bm_g)
            ybuf[:, h * HD:(h + 1) * HD] = y_h

    xs = xact[:, :D_SSM]
    y = (ybuf[...] + dskip_ref[...] * xs) * jax.nn.silu(z_ref[...])
    y = y * lax.rsqrt(jnp.mean(y * y, axis=-1, keepdims=True) + RMS_EPS) * normw_ref[...]
    y_ref[...] = y


def _ssd(z, xbc, misc, misct, lw, chunk):
    bsz, t, _ = z.shape
    nc = t // chunk
    nseq = 1
    seq = lambda b, c: (b, c, 0)
    per_b4 = lambda b, c: (b, 0, 0, 0)
    fixed = lambda b, c: (0, 0)
    return pl.pallas_call(
        functools.partial(_ssd_kernel, chunk=chunk, nseq=nseq),
        grid=(bsz // nseq, nc),
        in_specs=[
            pl.BlockSpec((nseq, chunk, D_SSM), seq),
            pl.BlockSpec((nseq, chunk, CONV_DIM), seq),
            pl.BlockSpec((nseq, chunk, LANES), seq),
            pl.BlockSpec((nseq, LANES, chunk), lambda b, c: (b, 0, c)),
            pl.BlockSpec((CONV_TAPS, CONV_DIM), fixed),
            pl.BlockSpec((1, CONV_DIM), fixed),
            pl.BlockSpec((1, LANES), fixed),
            pl.BlockSpec((1, LANES), fixed),
            pl.BlockSpec((SUBLANES, 1), fixed),
            pl.BlockSpec((SUBLANES, 1), fixed),
            pl.BlockSpec((1, D_SSM), fixed),
            pl.BlockSpec((1, D_SSM), fixed),
        ],
        out_specs=(
            pl.BlockSpec((nseq, chunk, D_SSM), seq),
            pl.BlockSpec((nseq, N_SSM_HEADS, HD, HD), per_b4),
        ),
        out_shape=(
            jax.ShapeDtypeStruct((bsz, t, D_SSM), F32),
            jax.ShapeDtypeStruct((bsz, N_SSM_HEADS, HD, HD), F32),
        ),
        scratch_shapes=[
            pltpu.VMEM((nseq, SUBLANES + chunk, CONV_DIM), F32),
            pltpu.VMEM((nseq, N_SSM_HEADS, HD, HD), F32),
            pltpu.VMEM((nseq, chunk, D_SSM), F32),
        ],
        compiler_params=_params(("parallel", "arbitrary")),
        name="ssd",
    )(z, xbc, misc, misct, lw["conv_w"], lw["conv_b_row"], lw["dtb_row"], lw["alog_row"],
      lw["dtb_col"], lw["alog_col"], lw["dskip_row"], lw["normw_row"])


def _ssd_sample_kernel(xh_ref, z_ref, dtraw_ref, state_ref, convw_ref, convb_ref, dtb_ref, alog_ref,
                       dskip_ref, normw_ref, y_ref, hout_ref, xact, ypre, *, s_len):
    h = pl.program_id(0)

    @pl.when(h == 0)
    def _():
        for t in range(s_len):
            conv = convb_ref[...]
            for k in range(CONV_TAPS):
                conv = conv + convw_ref[k] * xh_ref[t + k]
            xact[t] = jax.nn.silu(conv)

    a = -jnp.exp(alog_ref[pl.ds(h, 1), :])
    dts = [jax.nn.softplus(dtraw_ref[t, pl.ds(h, 1), :] + dtb_ref[pl.ds(h, 1), :]) for t in range(s_len)]
    das = [jnp.exp(dt * a) for dt in dts]
    g = h // (N_SSM_HEADS // N_SSM_GROUPS)
    b_lo = pl.multiple_of(D_SSM + g * HD, HD)
    c_lo = pl.multiple_of(D_SSM + N_SSM_GROUPS * HD + g * HD, HD)
    bs = [xact[t, pl.ds(b_lo, HD), :] for t in range(s_len)]
    cs = [xact[t, pl.ds(c_lo, HD), :] for t in range(s_len)]

    def body(p, carry):
        hs = state_ref[0, p]
        row = h * HD + p
        for t in range(s_len):
            x = xact[t, pl.ds(row, 1), :]
            hs = hs * das[t] + (x * dts[t]) * bs[t]
            y = jnp.sum(cs[t] * hs, axis=0, keepdims=True)
            ypre[t, pl.ds(row, 1), :] = y + dskip_ref[pl.ds(row, 1), :] * x
        hout_ref[0, p] = hs
        return carry

    lax.fori_loop(0, HD, body, 0)

    @pl.when(h == pl.num_programs(0) - 1)
    def _():
        for t in range(s_len):
            y = ypre[t] * jax.nn.silu(z_ref[t])
            y = y * lax.rsqrt(jnp.mean(y * y, axis=0, keepdims=True) + RMS_EPS) * normw_ref[...]
            y_ref[t] = y


def _ssd_sample(xh, zt, dtraw, state, lw, dbz, s_len):
    lane_b = lambda v: jnp.broadcast_to(v[..., None], v.shape + (dbz,))
    full = lambda a: pl.BlockSpec(a.shape, lambda h: (0,) * a.ndim)
    args = (xh, zt, dtraw, state, lane_b(lw["conv_w"]), lane_b(lw["conv_b_row"][0]), lane_b(lw["dtb_col"][:, 0]),
            lane_b(lw["alog_col"][:, 0]), lane_b(lw["dskip_row"][0]), lane_b(lw["normw_row"][0]))
    state_spec = pl.BlockSpec((1, HD, HD, dbz), lambda h: (h, 0, 0, 0))
    in_specs = [full(a) for a in args]
    in_specs[3] = state_spec
    return pl.pallas_call(
        functools.partial(_ssd_sample_kernel, s_len=s_len),
        grid=(N_SSM_HEADS,),
        in_specs=in_specs,
        out_specs=(pl.BlockSpec((s_len, D_SSM, dbz), lambda h: (0, 0, 0)), state_spec),
        out_shape=(jax.ShapeDtypeStruct((s_len, D_SSM, dbz), F32),
                   jax.ShapeDtypeStruct(state.shape, F32)),
        scratch_shapes=[pltpu.VMEM((s_len, CONV_DIM, dbz), F32), pltpu.VMEM((s_len, D_SSM, dbz), F32)],
        compiler_params=_params(("arbitrary",)),
        name="ssd_sample",
    )(*args)


CMP_IN = CMP_STRIDE * HD


def _page_group(n_pages):
    return math.gcd(n_pages, 8)


def _compress_pages(get_pages, n_pages, tail_rows, pm_ref, w1_ref, w2_ref, pe_ref, x_buf, sec_buf):
    rows_pp = PAGE // CMP_STRIDE
    n = n_pages * rows_pp
    nr = n + SUBLANES
    group = _page_group(n_pages)
    low = _iota((rows_pp, KV_DIM), 1) < HD

    for i in range(n_pages // group):
        perm = _dot_nt(pm_ref[...], get_pages(i))
        for j in range(group):
            r0 = (i * group + j) * rows_pp
            for b in range(CMP_STRIDE // 2):
                even = perm[2 * b * rows_pp:(2 * b + 1) * rows_pp, j * KV_DIM:(j + 1) * KV_DIM]
                odd = perm[(2 * b + 1) * rows_pp:(2 * b + 2) * rows_pp, j * KV_DIM:(j + 1) * KV_DIM]
                x_buf[0, r0:r0 + rows_pp, b * LANES:(b + 1) * LANES] = jnp.where(low, even, pltpu.roll(odd, HD, 1))
                x_buf[1, r0:r0 + rows_pp, b * LANES:(b + 1) * LANES] = jnp.where(low, pltpu.roll(even, HD, 1), odd)
    x_buf[:, n:nr, :] = tail_rows
    out = _dot(x_buf[...].reshape(N_KV * nr, CMP_IN).astype(BF16), w1_ref[...])
    sec_buf[...] = out[:, CMP_HIDDEN:]
    pe_out = _dot(pe_ref[...], w1_ref[...])
    pe_term = pe_out[0:1, :CMP_HIDDEN] + pe_out[SUBLANES:SUBLANES + 1, CMP_HIDDEN:]
    pre = jnp.concatenate([out[g * nr:g * nr + n, :CMP_HIDDEN] + sec_buf[g * nr + 1:g * nr + n + 1, :] + pe_term
                           for g in range(N_KV)], axis=1)
    return _dot(jax.nn.gelu(pre).astype(BF16), w2_ref[...])


def _compress_scratch(n):
    return [pltpu.VMEM((N_KV, n + SUBLANES, CMP_IN), F32), pltpu.VMEM((N_KV * (n + SUBLANES), CMP_HIDDEN), F32)]


def _compress_kernel(x_ref, pm_ref, w1_ref, w2_ref, pe_ref, o_ref, x_buf, sec_buf, *, n_pages):
    group = _page_group(n_pages)

    def get_pages(i):
        wide = x_ref[0, 0, :, i * group * PAGE:(i + 1) * group * PAGE]
        return jnp.concatenate([wide[:, j * PAGE:(j + 1) * PAGE] for j in range(group)], axis=0)

    tail_rows = jnp.zeros((N_KV, SUBLANES, CMP_IN), F32)
    o_ref[0] = _compress_pages(get_pages, n_pages, tail_rows, pm_ref, w1_ref, w2_ref, pe_ref, x_buf, sec_buf)


def _compress(kvtb4, proj, pm, w1big, w2big, pe2):
    bsz, _, _, t = kvtb4.shape
    n_pages = t // PAGE
    n = t // CMP_STRIDE
    fixed = lambda b: (0, 0)
    return pl.pallas_call(
        functools.partial(_compress_kernel, n_pages=n_pages),
        grid=(bsz,),
        in_specs=[
            pl.BlockSpec((1, 1, KV_DIM, t), lambda b: (b, proj, 0, 0)),
            pl.BlockSpec(pm.shape, fixed),
            pl.BlockSpec(w1big.shape, fixed),
            pl.BlockSpec(w2big.shape, fixed),
            pl.BlockSpec(pe2.shape, fixed),
        ],
        out_specs=pl.BlockSpec((1, n, KV_DIM), lambda b: (b, 0, 0)),
        out_shape=jax.ShapeDtypeStruct((bsz, n, KV_DIM), F32),
        scratch_shapes=_compress_scratch(n),
        compiler_params=_params(("parallel",)),
        name="compress",
    )(kvtb4, pm, w1big, w2big, pe2)


def _page_copy(cache_ref, buf, sem, page, slot, p):
    return pltpu.make_async_copy(cache_ref.at[page], buf.at[slot, pl.ds(p * PAGE, PAGE)], sem.at[slot])


def _gather_pages(pt_ref, streams, n_pages):
    b = pl.program_id(0)
    slot = b % 2

    def issue(seq, into):
        def start(p, carry):
            page = pt_ref[seq, p]
            for k, (cache_ref, buf, sem) in enumerate(streams):
                _page_copy(cache_ref, buf, sem, page, into, p).start(priority=k % 2)
            return carry
        lax.fori_loop(0, n_pages, start, 0, unroll=_page_group(n_pages))

    @pl.when(b == 0)
    def _():
        issue(b, slot)

    @pl.when(b + 1 < pl.num_programs(0))
    def _():
        issue(b + 1, 1 - slot)

    def wait(p, carry):
        for cache_ref, buf, sem in streams:
            _page_copy(cache_ref, buf, sem, 0, slot, p).wait()
        return carry

    lax.fori_loop(0, n_pages, wait, 0, unroll=_page_group(n_pages))
    return slot


def _compress_paged_kernel(pt_ref, kcache_ref, vcache_ref, ktail_ref, vtail_ref, pm_ref,
                           kw1_ref, kw2_ref, kpe_ref, vw1_ref, vw2_ref, vpe_ref, ko_ref, vo_ref,
                           kbuf, vbuf, kx_buf, vx_buf, ksec_buf, vsec_buf, ksem, vsem, *, n_pages):
    slot = _gather_pages(pt_ref, [(kcache_ref, kbuf, ksem), (vcache_ref, vbuf, vsem)], n_pages)
    rows = _page_group(n_pages) * PAGE
    for buf, tail_ref, w1_ref, w2_ref, pe_ref, x_buf, sec_buf, o_ref in (
            (kbuf, ktail_ref, kw1_ref, kw2_ref, kpe_ref, kx_buf, ksec_buf, ko_ref),
            (vbuf, vtail_ref, vw1_ref, vw2_ref, vpe_ref, vx_buf, vsec_buf, vo_ref)):
        get_pages = lambda i, buf=buf: buf[slot, i * rows:(i + 1) * rows, :].astype(BF16)
        o_ref[0] = _compress_pages(get_pages, n_pages, tail_ref[0], pm_ref, w1_ref, w2_ref, pe_ref,
                                   x_buf, sec_buf)


def _compress_paged(page_table, kcache, vcache, ktail, vtail, kweights, vweights):
    dbz, n_pages = page_table.shape
    n = n_pages * (PAGE // CMP_STRIDE)
    pm = kweights[0]
    consts = (pm,) + tuple(kweights[1:]) + tuple(vweights[1:])
    resident = lambda a: pl.BlockSpec(a.shape, lambda b, pt: (0, 0), pipeline_mode=pl.Buffered(1))
    tail_spec = pl.BlockSpec((1,) + ktail.shape[1:], lambda b, pt: (b, 0, 0, 0))
    out_spec = pl.BlockSpec((1, n, KV_DIM), lambda b, pt: (b, 0, 0))
    grid_spec = pltpu.PrefetchScalarGridSpec(
        num_scalar_prefetch=1,
        grid=(dbz,),
        in_specs=[pl.BlockSpec(memory_space=pl.ANY), pl.BlockSpec(memory_space=pl.ANY), tail_spec, tail_spec]
                 + [resident(a) for a in consts],
        out_specs=(out_spec, out_spec),
        scratch_shapes=[pltpu.VMEM((2, n_pages * PAGE, PAGE), F32)] * 2
                       + _compress_scratch(n)[:1] * 2 + _compress_scratch(n)[1:] * 2
                       + [pltpu.SemaphoreType.DMA((2,))] * 2,
    )
    out = jax.ShapeDtypeStruct((dbz, n, KV_DIM), F32)
    return pl.pallas_call(
        functools.partial(_compress_paged_kernel, n_pages=n_pages),
        grid_spec=grid_spec,
        out_shape=(out, out),
        compiler_params=_params(("arbitrary",)),
        name="compress_paged",
    )(page_table, kcache, vcache, ktail, vtail, *consts)


RANK_UNROLL = 4


def _select_blocks(imp_t, tpos, n_blocks, score_buf, n_live=None):
    j = _iota(imp_t.shape, 0)
    valid = (j * SEL_BLOCK <= tpos) & (j < n_blocks)
    cur = tpos // SEL_BLOCK
    forced = (j == 0) | (j == cur) | (j == cur - 1)
    score = jnp.where(valid, jnp.where(forced, FORCED_SCORE, imp_t), INVALID_SCORE)
    score_buf[...] = score

    def body(i, cnt):
        row = score_buf[pl.ds(i, 1), :]
        above = jnp.where(row > score, 1.0, 0.0)
        tie = jnp.where(row == score, jnp.where(j > i, 1.0, 0.0), 0.0)
        return cnt + above + tie

    zero = jnp.zeros(imp_t.shape, F32)
    if n_live is None:
        cnt = lax.fori_loop(0, n_blocks, body, zero, unroll=RANK_UNROLL)
    else:
        def group(gi, cnt):
            for u in range(RANK_UNROLL):
                cnt = body(gi * RANK_UNROLL + u, cnt)
            return cnt
        assert imp_t.shape[0] % RANK_UNROLL == 0
        cnt = lax.fori_loop(0, (n_live + RANK_UNROLL - 1) // RANK_UNROLL, group, zero)
    return jnp.where(valid & (cnt < float(min(SEL_TOP_N, n_blocks))), 1.0, 0.0)


def _nsa_prompt_kernel(q_ref, kc_ref, vc_ref, ks_ref, vs_ref, kw_ref, vw_ref, ovl_ref, e_ref, wbias_ref, misc_ref,
                       o_ref, score_buf, *, tq, tk, tw, n_blocks):
    g = pl.program_id(1)
    t0 = pl.program_id(2) * tq
    rows = REP * tq
    q = q_ref[0].reshape(rows, HD)

    kc = kc_ref[0, 0]
    ncp = kc.shape[0]
    qpos_c = t0 + _iota((tq, ncp), 0)
    maskc = ((_iota((tq, ncp), 1) * CMP_STRIDE + (CMP_BLOCK - 1)) <= qpos_c)[None]
    s = _dot_nt(q, kc).reshape(REP, tq, ncp)
    sm = jnp.where(maskc, s, NEG)
    e = jnp.exp(sm - jnp.max(sm, axis=-1, keepdims=True))
    p = e / jnp.sum(e, axis=-1, keepdims=True) * jnp.where(maskc, 1.0, 0.0)
    o_c = _dot(p.reshape(rows, ncp).astype(BF16), vc_ref[0, 0])
    psum = jnp.sum(p, axis=0)

    imp_t = _dot_nt(ovl_ref[...], psum, HI)
    tpos = t0 + _iota(imp_t.shape, 1)
    n_live = jnp.minimum((t0 + tq - 1) // SEL_BLOCK + 1, n_blocks)
    selt = _select_blocks(imp_t, tpos, n_blocks, score_buf, n_live).astype(BF16)

    nsp = selt.shape[0]
    eye = jnp.where(_iota((nsp, nsp), 0) == _iota((nsp, nsp), 1), 1.0, 0.0).astype(BF16)
    bias = ((_dot_tn(selt, eye) - 1.0) * (-NEG)).astype(BF16)
    q_aug = jnp.concatenate([q, jnp.concatenate([bias] * REP, axis=0)], axis=1)
    ones_k = jnp.ones((HD, tk), BF16)
    qpos_k = t0 + _iota((tq, tk), 0)
    lane_k = _iota((tq, tk), 1)
    n_full = t0 // tk

    def step(kt, carry, diagonal):
        m, acc = carry
        k0 = pl.multiple_of(kt * tk, tk)
        k_aug = jnp.concatenate([ks_ref[0, 0, :, pl.ds(k0, tk)], e_ref[kt]], axis=0)
        v_aug = jnp.concatenate([vs_ref[0, 0, :, pl.ds(k0, tk)], ones_k], axis=0)
        s = _dot(q_aug, k_aug).reshape(REP, tq, tk)
        if diagonal:
            s = jnp.where((k0 + lane_k <= qpos_k)[None], s, NEG)
        m_new = jnp.maximum(m, jnp.max(s, axis=-1, keepdims=True))
        e = jnp.exp(s - m_new)
        pv = _dot_nt(e.reshape(rows, tk).astype(BF16), v_aug).reshape(REP, tq, 2 * HD)
        return m_new, jnp.exp(m - m_new) * acc + pv

    init = (jnp.full((REP, tq, 1), NEG, F32), jnp.zeros((REP, tq, 2 * HD), F32))
    carry = lax.fori_loop(0, n_full, functools.partial(step, diagonal=False), init)
    _, acc_s = step(n_full, carry, True)
    o_s = acc_s[..., :HD] / acc_s[..., HD:]

    wk = WINDOW + tw
    q3 = q.reshape(REP, tq, HD)
    ones_w = jnp.ones((HD, wk), BF16)
    o_w = []
    for w in range(tq // tw):
        t0w = t0 + w * tw
        start = pl.multiple_of(jnp.maximum(t0w - WINDOW, 0), tw)
        kw = kw_ref[0, 0, :, pl.ds(start, wk)]
        vw_aug = jnp.concatenate([vw_ref[0, 0, :, pl.ds(start, wk)], ones_w], axis=0)
        case = jnp.minimum(t0w // tw, WINDOW // tw)
        qw = q3[:, w * tw:(w + 1) * tw].reshape(REP * tw, HD)
        sm = _dot(qw, kw).reshape(REP, tw, wk) + wbias_ref[case][None]
        e = jnp.exp(sm - jnp.max(sm, axis=-1, keepdims=True))
        acc_w = _dot_nt(e.reshape(REP * tw, wk).astype(BF16), vw_aug).reshape(REP, tw, 2 * HD)
        o_w.append(acc_w[..., :HD] / acc_w[..., HD:])
    o_w = o_w[0] if len(o_w) == 1 else jnp.concatenate(o_w, axis=1)

    gates = jax.nn.sigmoid(misc_ref[0])
    gate_lane = _iota(gates.shape, 1) - (N_SSM_HEADS + 3 * REP * g)
    o_c = o_c.reshape(REP, tq, HD)
    for r in range(REP):
        gh = [jnp.sum(jnp.where(gate_lane == 3 * r + br, gates, 0.0), axis=-1, keepdims=True)
              for br in range(3)]
        o = gh[0] * o_c[r] + gh[1] * o_s[r] + gh[2] * o_w[r]
        o_ref[0, :, r * HD:(r + 1) * HD] = o.astype(o_ref.dtype)


NSA_WINDOW_TILE = 128


def _window_bias(tw):
    c = np.arange(WINDOW // tw + 1)[:, None, None]
    d = np.arange(WINDOW + tw)[None, None, :] - c * tw - np.arange(tw)[None, :, None]
    return jnp.asarray(np.where((d <= 0) & (d > -WINDOW), 0.0, NEG).astype(np.float32))


def _nsa_prompt(qh, kch, vch, kvtb, ovl_t, e3, misc, n_blocks, tq, tk):
    bsz, nh, t, _ = qh.shape
    ncp = kch.shape[2]
    nsp = ovl_t.shape[0]
    kv_spec = lambda proj: pl.BlockSpec((1, 1, HD, t), lambda b, g, i: (b, N_KV * proj + g, 0, 0))
    cmp_spec = pl.BlockSpec((1, 1, ncp, HD), lambda b, g, i: (b, g, 0, 0))
    tw = min(tq, NSA_WINDOW_TILE)
    wbias = _window_bias(tw)
    return pl.pallas_call(
        functools.partial(_nsa_prompt_kernel, tq=tq, tk=tk, tw=tw, n_blocks=n_blocks),
        grid=(bsz, N_KV, t // tq),
        in_specs=[
            pl.BlockSpec((1, REP, tq, HD), lambda b, g, i: (b, g, i, 0)),
            cmp_spec, cmp_spec,
            kv_spec(2), kv_spec(3), kv_spec(4), kv_spec(5),
            pl.BlockSpec(ovl_t.shape, lambda b, g, i: (0, 0)),
            pl.BlockSpec(e3.shape, lambda b, g, i: (0, 0, 0)),
            pl.BlockSpec(wbias.shape, lambda b, g, i: (0, 0, 0)),
            pl.BlockSpec((1, tq, LANES), lambda b, g, i: (b, i, 0)),
        ],
        out_specs=pl.BlockSpec((1, tq, REP * HD), lambda b, g, i: (b, i, g)),
        out_shape=jax.ShapeDtypeStruct((bsz, t, nh * HD), BF16),
        scratch_shapes=[pltpu.VMEM((nsp, tq), F32)],
        compiler_params=_params(("parallel", "parallel", "arbitrary")),
        name="nsa_prompt",
    )(qh, kch, vch, kvtb, kvtb, kvtb, kvtb, ovl_t, e3, wbias, misc)


def _out_proj_kernel(y_ref, o_ref, h_ref, w_ref, g_ref, b_ref, out_ref, *, alpha):
    mixed = _dot(y_ref[...].astype(BF16), w_ref[0:D_SSM, :]) + _dot(o_ref[...], w_ref[D_SSM:, :])
    out_ref[...] = _layer_norm(alpha * h_ref[...] + mixed, g_ref[...], b_ref[...])


def _out_proj_ln(y, o, h, w, g, b, alpha):
    m, d = h.shape
    tm = min(512, m)
    row = lambda i: (i, 0)
    fixed = lambda i: (0, 0)
    return pl.pallas_call(
        functools.partial(_out_proj_kernel, alpha=alpha),
        grid=(m // tm,),
        in_specs=[
            pl.BlockSpec((tm, D_SSM), row),
            pl.BlockSpec((tm, D_NSA), row),
            pl.BlockSpec((tm, d), row),
            pl.BlockSpec(w.shape, fixed),
            pl.BlockSpec((1, d), fixed),
            pl.BlockSpec((1, d), fixed),
        ],
        out_specs=pl.BlockSpec((tm, d), row),
        out_shape=jax.ShapeDtypeStruct((m, d), F32),
        compiler_params=_params(("parallel",)),
        name="out_proj_ln",
    )(y, o, h, w, g, b)


def _row_group(shape):
    return _iota(shape, 0) // (shape[0] // N_KV)


def _cmp_attn_sample_kernel(q_ref, kc_ref, vc_ref, ssum_ref, oc_ref, psum_ref, *, past, s_len):
    nb, rows, _ = q_ref.shape
    nc = kc_ref.shape[1]
    t_row = (_iota((rows, nc), 0) // REP) % s_len
    cidx = _iota((rows, nc), 1)
    maskc = (cidx * CMP_STRIDE + (CMP_BLOCK - 1)) <= past + t_row
    maskf = jnp.where(maskc, 1.0, 0.0)
    own = (_iota((rows, KV_DIM), 1) // HD) == _row_group((rows, KV_DIM))
    for i in range(nb):
        s = _dot_nt(q_ref[i], kc_ref[i].astype(BF16))
        sm = jnp.where(maskc, s, NEG)
        e = jnp.exp(sm - jnp.max(sm, axis=-1, keepdims=True))
        p = e / jnp.sum(e, axis=-1, keepdims=True) * maskf
        o = _dot(p.astype(BF16), vc_ref[i].astype(BF16))
        oc_ref[i] = jnp.where(own, o, 0.0)
        psum_ref[i] = _dot(ssum_ref[...], p, HI)


def _cmp_attn_sample(qbd, kc, vc, ssum, past, s_len):
    dbz, rows, _ = qbd.shape
    nc = kc.shape[1]
    ng = ssum.shape[0]
    nb = math.gcd(dbz, 4)
    per_b = lambda b: (b, 0, 0)
    return pl.pallas_call(
        functools.partial(_cmp_attn_sample_kernel, past=past, s_len=s_len),
        grid=(dbz // nb,),
        in_specs=[
            pl.BlockSpec((nb, rows, KV_DIM), per_b),
            pl.BlockSpec((nb, nc, KV_DIM), per_b),
            pl.BlockSpec((nb, nc, KV_DIM), per_b),
            pl.BlockSpec(ssum.shape, lambda b: (0, 0)),
        ],
        out_specs=(pl.BlockSpec((nb, rows, KV_DIM), per_b), pl.BlockSpec((nb, ng, nc), per_b)),
        out_shape=(jax.ShapeDtypeStruct((dbz, rows, KV_DIM), F32),
                   jax.ShapeDtypeStruct((dbz, ng, nc), F32)),
        compiler_params=_params(("parallel",)),
        name="cmp_attn_sample",
    )(qbd, kc, vc, ssum)


def _select_sample_kernel(psum_ref, ovl_ref, selt_ref, score_buf, *, past, s_len, n_blocks):
    imp_t = _dot_nt(ovl_ref[...], psum_ref[...], HI)
    tpos = past + _iota(imp_t.shape, 1) % s_len
    selt_ref[...] = _select_blocks(imp_t, tpos, n_blocks, score_buf)


def _select_sample(psum_all, ovl_t, past, s_len, n_blocks):
    nsp = ovl_t.shape[0]
    cols = psum_all.shape[0]
    return pl.pallas_call(
        functools.partial(_select_sample_kernel, past=past, s_len=s_len, n_blocks=n_blocks),
        out_shape=jax.ShapeDtypeStruct((nsp, cols), F32),
        scratch_shapes=[pltpu.VMEM((nsp, cols), F32)],
        compiler_params=pltpu.CompilerParams(vmem_limit_bytes=VMEM_LIMIT),
        name="select_sample",
    )(psum_all, ovl_t)


def _online_update(state, sm, v_t):
    m, l, acc = state
    m_new = jnp.maximum(m, jnp.max(sm, axis=-1, keepdims=True))
    alpha = jnp.exp(m - m_new)
    e = jnp.exp(sm - m_new)
    l = alpha * l + jnp.sum(e, axis=-1, keepdims=True)
    return m_new, l, alpha * acc + _dot_nt(e.astype(BF16), v_t)


def _sel_win_sample_kernel(pt_ref, kcache_ref, vcache_ref, q_ref, sel_ref, e_ref, kt_ref, vt_ref,
                           kw_ref, vw_ref, kwn_ref, vwn_ref, oc_ref, gate_ref, o_ref, kbuf, vbuf, ksem, vsem,
                           *, n_pages, s_len, n_chunks, chunk):
    slot = _gather_pages(pt_ref, [(kcache_ref, kbuf, ksem), (vcache_ref, vbuf, vsem)], n_pages)

    q = q_ref[0]
    rows = q.shape[0]
    sel = sel_ref[0]
    sel_main = sel[:, :LANES]
    past = n_pages * PAGE
    t_col = (_iota((rows, 1), 0) // REP) % s_len
    pages_pc = chunk // PAGE

    def chunk_t(buf, c):
        return jnp.concatenate([buf[slot, (c * pages_pc + j) * PAGE:(c * pages_pc + j + 1) * PAGE, :]
                                for j in range(pages_pc)], axis=1).astype(BF16)

    state = (jnp.full((rows, 1), NEG, F32), jnp.zeros((rows, 1), F32), jnp.zeros((rows, KV_DIM), F32))
    for c in range(n_chunks):
        blk = _dot(sel_main, e_ref[c])
        kpos = c * chunk + _iota((rows, chunk), 1)
        ok = jnp.where(kpos <= past + t_col, blk, 0.0) > 0.5
        state = _online_update(state, jnp.where(ok, _dot(q, chunk_t(kbuf, c)), NEG), chunk_t(vbuf, c))
    lane_t = _iota((rows, kt_ref.shape[2]), 1)
    sel_new = sel[:, LANES:LANES + 1].astype(F32)
    ok = jnp.where(lane_t <= t_col, jnp.where(lane_t < s_len, sel_new, 0.0), 0.0) > 0.5
    _, l, acc = _online_update(state, jnp.where(ok, _dot(q, kt_ref[0].astype(BF16)), NEG),
                               vt_ref[0].astype(BF16))
    o_s = acc / l

    wb = kw_ref.shape[2]
    iw = _iota((rows, wb), 1)
    ok_c = iw > t_col + (wb - WINDOW)
    state = (jnp.full((rows, 1), NEG, F32), jnp.zeros((rows, 1), F32), jnp.zeros((rows, KV_DIM), F32))
    state = _online_update(state, jnp.where(ok_c, _dot(q, kw_ref[0].astype(BF16)), NEG), vw_ref[0].astype(BF16))
    ok_n = jnp.where(lane_t <= t_col, jnp.where(lane_t < s_len, 1.0, 0.0), 0.0) > 0.5
    _, l, acc = _online_update(state, jnp.where(ok_n, _dot(q, kwn_ref[0].astype(BF16)), NEG),
                               vwn_ref[0].astype(BF16))
    o_w = acc / l

    gates = jax.nn.sigmoid(gate_ref[0])
    o = gates[:, 0:1] * oc_ref[0] + gates[:, 1:2] * o_s + gates[:, 2:3] * o_w
    own = (_iota(o.shape, 1) // HD) == _row_group(o.shape)
    o = jnp.where(own, o, 0.0)
    o_ref[0] = o[:, :HD] + o[:, HD:]


def _sel_win_sample(page_table, kcache, vcache, qbd, selx, e4, ktail, vtail, kwin, vwin, kwnew, vwnew,
                    oc, graw, s_len):
    dbz, n_pages = page_table.shape
    rows = qbd.shape[1]
    n_chunks, _, chunk = e4.shape
    per_b = lambda b, pt: (b, 0, 0)
    blk = lambda a: pl.BlockSpec((1,) + a.shape[1:], per_b)
    grid_spec = pltpu.PrefetchScalarGridSpec(
        num_scalar_prefetch=1,
        grid=(dbz,),
        in_specs=[
            pl.BlockSpec(memory_space=pl.ANY),
            pl.BlockSpec(memory_space=pl.ANY),
            blk(qbd), blk(selx),
            pl.BlockSpec(e4.shape, lambda b, pt: (0, 0, 0)),
            blk(ktail), blk(vtail), blk(kwin), blk(vwin), blk(kwnew), blk(vwnew), blk(oc), blk(graw),
        ],
        out_specs=pl.BlockSpec((1, rows, HD), per_b),
        scratch_shapes=[
            pltpu.VMEM((2, n_pages * PAGE, PAGE), F32),
            pltpu.VMEM((2, n_pages * PAGE, PAGE), F32),
            pltpu.SemaphoreType.DMA((2,)),
            pltpu.SemaphoreType.DMA((2,)),
        ],
    )
    return pl.pallas_call(
        functools.partial(_sel_win_sample_kernel, n_pages=n_pages, s_len=s_len,
                          n_chunks=n_chunks, chunk=chunk),
        grid_spec=grid_spec,
        out_shape=jax.ShapeDtypeStruct((dbz, rows, HD), F32),
        compiler_params=_params(("arbitrary",)),
        name="sel_win_sample",
    )(page_table, kcache, vcache, qbd, selx, e4, ktail, vtail, kwin, vwin, kwnew, vwnew, oc, graw)


def _rope_tables(pos):
    half = HD // 2
    inv = ROPE_THETA ** (-np.arange(half, dtype=np.float64) / half)
    ang = pos.astype(np.float64)[:, None] * inv[None, :]
    cos = np.cos(ang)
    sin = np.sin(ang)
    reps = LANES // HD
    cos_l = np.tile(np.concatenate([cos, cos], axis=-1), (1, reps))
    sin_l = np.tile(np.concatenate([-sin, sin], axis=-1), (1, reps))
    return tuple(jnp.asarray(np.ascontiguousarray(a), F32) for a in (cos_l, sin_l, cos.T, sin.T))


def _overlap_t(nc, ncp, ns, nsp):
    c_start = np.arange(ncp) * CMP_STRIDE
    s_start = np.arange(nsp) * SEL_BLOCK
    ovl = ((c_start[None, :] + CMP_BLOCK > s_start[:, None]) & (c_start[None, :] < s_start[:, None] + SEL_BLOCK))
    ovl = ovl & (np.arange(ncp)[None, :] < nc) & (np.arange(nsp)[:, None] < ns)
    return jnp.asarray(ovl.astype(np.float32))


def _expander(n_rows, n_chunks, chunk, dtype=BF16):
    key_block = (np.arange(n_chunks)[:, None] * chunk + np.arange(chunk)[None, :]) // SEL_BLOCK
    e = key_block[:, None, :] == np.arange(n_rows)[None, :, None]
    return jnp.asarray(e.astype(np.float32)).astype(dtype)


def _page_permutation():
    rows_pp = PAGE // CMP_STRIDE
    r = np.arange(PAGE)
    src = (r % rows_pp) * CMP_STRIDE + r // rows_pp
    return jnp.asarray((src[:, None] == np.arange(PAGE)[None, :]).astype(np.float32)).astype(BF16)


def _round_up(x, m):
    return -(-x // m) * m


def _compress_weights(w1, w2, pe):
    eye = jnp.eye(N_KV, dtype=F32)
    halves = [w1[half * CMP_STRIDE:(half + 1) * CMP_STRIDE].reshape(CMP_IN, CMP_HIDDEN) for half in range(2)]
    w1d = jnp.concatenate(halves, axis=1).astype(BF16)
    w2big = jnp.einsum("hd,gk->ghkd", w2, eye).reshape(N_KV * CMP_HIDDEN, KV_DIM).astype(BF16)
    pe_rows = [jnp.broadcast_to(pe[half * CMP_STRIDE:(half + 1) * CMP_STRIDE].reshape(1, CMP_IN), (SUBLANES, CMP_IN))
               for half in range(2)]
    pe2 = jnp.concatenate(pe_rows, axis=0).astype(BF16)
    return _page_permutation(), w1d, w2big, pe2


def _layer_weights(w, l):
    sizes = [D_SSM, CONV_DIM, N_SSM_HEADS, D_NSA, N_KV_PROJ * KV_DIM, N_HEADS * 3]
    offs = np.cumsum([0] + sizes)
    w_in = w["w_in"][l]
    seg = lambda i: w_in[:, offs[i]:offs[i + 1]]
    pad = LANES - sizes[2] - sizes[5]
    w_in_r = jnp.concatenate([seg(0), seg(1), seg(3), seg(2), seg(5),
                              jnp.zeros((w_in.shape[0], pad), F32)], axis=1).astype(BF16)
    bias_misc = jnp.concatenate([jnp.zeros((sizes[2],), F32), w["b_gate"][l], jnp.zeros((pad,), F32)])[None, :]
    lane_pad = lambda v: jnp.concatenate([v, jnp.zeros((LANES - v.shape[0],), F32)])[None, :]
    lw = {
        "w_in": w_in_r,
        "w_kv_t": seg(4).T.astype(BF16),
        "bias_misc": bias_misc,
        "conv_w": w["conv_w"][l],
        "conv_b_row": w["conv_b"][l][None, :],
        "dtb_row": lane_pad(w["dt_bias"][l]),
        "alog_row": lane_pad(w["a_log"][l]),
        "dtb_col": w["dt_bias"][l][:, None],
        "alog_col": w["a_log"][l][:, None],
        "dskip_row": jnp.repeat(w["d_skip"][l], HD)[None, :],
        "normw_row": w["ssm_norm_w"][l][None, :],
        "w_out": w["w_out"][l].astype(BF16),
    }
    lw["cmp_k"] = _compress_weights(w["cmp_k_w1"][l], w["cmp_k_w2"][l], w["cmp_k_pe"][l])
    lw["cmp_v"] = _compress_weights(w["cmp_v_w1"][l], w["cmp_v_w2"][l], w["cmp_v_pe"][l])
    for i in (1, 2, 3):
        lw[f"ln{i}"] = (w[f"ln{i}_g"][l][None, :], w[f"ln{i}_b"][l][None, :])
    for i in (1, 2):
        lw[f"ffn{i}"] = (w[f"ffn{i}_w_gate"][l].astype(BF16), w[f"ffn{i}_w_up"][l].astype(BF16),
                         w[f"ffn{i}_w_down"][l].astype(BF16))
    return lw


def _heads_major(x, bsz, t, n):
    return x.reshape(bsz, t, n, HD).transpose(0, 2, 1, 3)


def _mix_prompt(h, lw, bsz, t):
    m = bsz * t
    z, xbc, misc, qh, kvtb, *kvt = _in_proj(h, lw["w_in"], lw["w_kv_t"], lw["bias_misc"],
                                            _rope_tables(np.arange(t)), bsz, t)
    kv_rows = [a.reshape(bsz, N_KV, HD, t).transpose(0, 3, 1, 2) for a in kvt]

    chunk = min(128, t)
    misc3 = misc.reshape(bsz, t, LANES)
    xbc3 = xbc.reshape(bsz, t, CONV_DIM)
    y_ssd, h_ssm = _ssd(z.reshape(bsz, t, D_SSM), xbc3, misc3, misc3.transpose(0, 2, 1), lw, chunk)
    conv_state = xbc3[:, t - (CONV_TAPS - 1):]

    n_str = t // CMP_STRIDE
    nc = n_str - 1
    ns = -(-t // SEL_BLOCK)
    nsp = _round_up(ns, SUBLANES)
    kvtb4 = kvtb.reshape(bsz, N_KV_PROJ, KV_DIM, t)
    kc = _compress(kvtb4, 0, *lw["cmp_k"])
    vc = _compress(kvtb4, 1, *lw["cmp_v"])
    kch = _heads_major(kc, bsz, n_str, N_KV).astype(BF16)
    vch = _heads_major(vc, bsz, n_str, N_KV).astype(BF16)
    tq = min(512, t)
    tk = min(512, t)
    e3 = _expander(nsp, t // tk, tk)
    o = _nsa_prompt(qh, kch, vch, kvtb.reshape(bsz, N_KV_PROJ * N_KV, HD, t), _overlap_t(nc, n_str, ns, nsp),
                    e3, misc3, ns, tq, tk)

    wb = min(WINDOW, t)
    state = tuple(kv_rows[:4]) + (kv_rows[4][:, t - wb:], kv_rows[5][:, t - wb:], h_ssm, conv_state)
    return y_ssd.reshape(m, D_SSM), o.reshape(m, D_NSA), state


def _mix_sample(h, lw, l, dbz, s_len, caches, state_ssm, state_conv, page_table):
    cache_k_cmp, cache_v_cmp, cache_k_slc, cache_v_slc, cache_k_win, cache_v_win = caches
    m = dbz * s_len
    n_pages = page_table.shape[1]
    past = n_pages * PAGE
    pos = past + np.arange(s_len)
    z, xbc, misc, qh, _, *kvt = _in_proj(h, lw["w_in"], lw["w_kv_t"], lw["bias_misc"],
                                         _rope_tables(np.tile(pos, dbz)), 1, m)
    kvs = [a.reshape(KV_DIM, dbz, s_len) for a in kvt]
    new_rows = [kvs[i].transpose(1, 2, 0) for i in range(N_KV_PROJ)]
    new_t = [kvs[i].transpose(1, 0, 2) for i in range(N_KV_PROJ)]
    k_c, v_c, k_s, v_s, k_w, v_w = [r.reshape(dbz, s_len, N_KV, HD) for r in new_rows]

    misc3 = misc.reshape(dbz, s_len, LANES)
    xbc3 = xbc.reshape(dbz, s_len, CONV_DIM)
    xh = jnp.concatenate([state_conv[l], xbc3], axis=1)
    y_t, h_new = _ssd_sample(xh.transpose(1, 2, 0), z.reshape(dbz, s_len, D_SSM).transpose(1, 2, 0),
                             misc3[:, :, :N_SSM_HEADS].transpose(1, 2, 0), state_ssm[l].transpose(1, 2, 3, 0),
                             lw, dbz, s_len)
    y_ssd = y_t.transpose(2, 0, 1)
    h_ssm = h_new.transpose(3, 0, 1, 2)
    conv_state = xh[:, -(CONV_TAPS - 1):]

    n_pool = cache_k_cmp.shape[1]
    pages_t = lambda cache: cache[l].transpose(0, 2, 3, 1).reshape(n_pool, KV_DIM, PAGE)

    def tail_rows(new):
        flat = new.reshape(dbz, s_len, N_KV, HD).transpose(0, 2, 1, 3).reshape(dbz, N_KV, 1, s_len * HD)
        return jnp.pad(flat, ((0, 0), (0, 0), (0, SUBLANES - 1), (0, CMP_IN - s_len * HD)))

    kc, vc = _compress_paged(page_table, pages_t(cache_k_cmp), pages_t(cache_v_cmp),
                             tail_rows(new_rows[0]), tail_rows(new_rows[1]), lw["cmp_k"], lw["cmp_v"])
    nc = kc.shape[1]
    total = past + s_len
    ns = -(-total // SEL_BLOCK)
    nsp = _round_up(ns, SUBLANES)

    rows = N_KV * s_len * REP
    qg = qh.reshape(N_KV, REP, dbz, s_len, HD).transpose(2, 0, 3, 1, 4)
    qbd = jnp.einsum("bgtrd,gk->bgtrkd", qg, jnp.eye(N_KV, dtype=BF16)).reshape(dbz, rows, KV_DIM)
    ng = N_KV * s_len
    ssum = jnp.asarray((np.arange(ng)[:, None] == np.arange(rows)[None, :] // REP).astype(np.float32))
    oc, psum = _cmp_attn_sample(qbd, kc, vc, ssum, past, s_len)
    selt = _select_sample(psum.reshape(dbz * ng, nc), _overlap_t(nc, nc, ns, nsp), past, s_len, ns)
    sel = selt.T.reshape(dbz, ng, 1, nsp)
    sel = jnp.broadcast_to(sel, (dbz, ng, REP, nsp)).reshape(dbz, rows, nsp)
    n_cached = past // SEL_BLOCK
    selx = jnp.concatenate([sel[:, :, :n_cached], jnp.zeros((dbz, rows, LANES - n_cached), F32),
                            sel[:, :, n_cached:n_cached + 1], jnp.zeros((dbz, rows, LANES - 1), F32)],
                           axis=-1).astype(BF16)
    chunk_k = min(2048, past)
    e4 = _expander(LANES, past // chunk_k, chunk_k)
    pad_lanes = lambda a: jnp.pad(a, ((0, 0), (0, 0), (0, LANES - s_len))).astype(BF16)
    wb = cache_k_win.shape[2]
    win_t = lambda cache: cache[l].transpose(0, 2, 3, 1).reshape(dbz, KV_DIM, wb)
    gate = misc3[:, :, N_SSM_HEADS:N_SSM_HEADS + 3 * N_HEADS].reshape(dbz, s_len, N_KV, REP, 3)
    graw = jnp.pad(gate.transpose(0, 2, 1, 3, 4).reshape(dbz, rows, 3), ((0, 0), (0, 0), (0, LANES - 3)))
    o = _sel_win_sample(page_table, pages_t(cache_k_slc), pages_t(cache_v_slc), qbd, selx, e4,
                        pad_lanes(new_t[2]), pad_lanes(new_t[3]), win_t(cache_k_win), win_t(cache_v_win),
                        pad_lanes(new_t[4]), pad_lanes(new_t[5]), oc, graw, s_len)
    o_nsa = o.reshape(dbz, N_KV, s_len, REP, HD).transpose(0, 2, 1, 3, 4).reshape(m, D_NSA).astype(BF16)

    kw_full = jnp.concatenate([cache_k_win[l], k_w], axis=1)
    vw_full = jnp.concatenate([cache_v_win[l], v_w], axis=1)
    state = (k_c, v_c, k_s, v_s, kw_full[:, -wb:], vw_full[:, -wb:], h_ssm, conv_state)
    return y_ssd.reshape(m, D_SSM), o_nsa, state


def _layer(x, lw, alpha, mix_fn):
    h1 = _ffn_ln(x, *lw["ffn1"], *lw["ln1"], alpha)
    y_ssd, o_nsa, state = mix_fn(h1)
    h2 = _out_proj_ln(y_ssd, o_nsa, h1, lw["w_out"], *lw["ln2"], alpha)
    return _ffn_ln(h2, *lw["ffn2"], *lw["ln3"], alpha), state


def kernel(x_prompt, x_sample, cache_k_cmp, cache_v_cmp, cache_k_slc, cache_v_slc, cache_k_win, cache_v_win, state_ssm, state_conv, page_table, w_in, b_gate, conv_w, conv_b, dt_bias, a_log, d_skip, ssm_norm_w, cmp_k_w1, cmp_k_w2, cmp_k_pe, cmp_v_w1, cmp_v_w2, cmp_v_pe, w_out, ln1_g, ln1_b, ln2_g, ln2_b, ln3_g, ln3_b, ffn1_w_gate, ffn1_w_up, ffn1_w_down, ffn2_w_gate, ffn2_w_up, ffn2_w_down):
    weights = dict(w_in=w_in, b_gate=b_gate, conv_w=conv_w, conv_b=conv_b, dt_bias=dt_bias, a_log=a_log,
                   d_skip=d_skip, ssm_norm_w=ssm_norm_w, cmp_k_w1=cmp_k_w1, cmp_k_w2=cmp_k_w2,
                   cmp_k_pe=cmp_k_pe, cmp_v_w1=cmp_v_w1, cmp_v_w2=cmp_v_w2, cmp_v_pe=cmp_v_pe, w_out=w_out,
                   ln1_g=ln1_g, ln1_b=ln1_b, ln2_g=ln2_g, ln2_b=ln2_b, ln3_g=ln3_g, ln3_b=ln3_b,
                   ffn1_w_gate=ffn1_w_gate, ffn1_w_up=ffn1_w_up, ffn1_w_down=ffn1_w_down,
                   ffn2_w_gate=ffn2_w_gate, ffn2_w_up=ffn2_w_up, ffn2_w_down=ffn2_w_down)
    depth = w_in.shape[0]
    bsz, t, d = x_prompt.shape
    dbz, s_len, _ = x_sample.shape
    alpha = (2.0 * depth) ** 0.25
    caches = (cache_k_cmp, cache_v_cmp, cache_k_slc, cache_v_slc, cache_k_win, cache_v_win)
    y_p = x_prompt.reshape(bsz * t, d)
    y_s = x_sample.reshape(dbz * s_len, d)
    p_states, s_states = [], []
    for l in range(depth):
        lw = _layer_weights(weights, l)
        y_p, st_p = _layer(y_p, lw, alpha, lambda h: _mix_prompt(h, lw, bsz, t))
        y_s, st_s = _layer(y_s, lw, alpha, lambda h: _mix_sample(h, lw, l, dbz, s_len, caches, state_ssm,
                                                                 state_conv, page_table))
        p_states.append(st_p)
        s_states.append(st_s)
    p_st = [jnp.stack(a) for a in zip(*p_states)]
    s_st = [jnp.stack(a) for a in zip(*s_states)]
    outs = [y_p.reshape(bsz, t, d), y_s.reshape(dbz, s_len, d)]
    for p, s in zip(p_st[:6], s_st[:6]):
        outs += [p, s]
    outs += [p_st[6], s_st[6], p_st[7], s_st[7]]
    return tuple(outs)
```
